```python
import functools
import jax, jax.numpy as jnp
from jax import lax
import numpy as np

D_MODEL = 1024
BATCH = 4
SEQ = 8192
DEPTH = 2
DEC_BATCH = 32
DEC_SEQ = 1
PAST_LEN = 16384
PAGE_SIZE = 128

HEAD_DIM = 64
D_MIX = D_MODEL
N_GROUP_HEADS = D_MIX // HEAD_DIM // 4
NSA_HEADS = N_GROUP_HEADS
SB_HEADS = N_GROUP_HEADS
GM_HEADS = N_GROUP_HEADS
POOL_GROUPS = 4
NSA_WIDTH = NSA_HEADS * HEAD_DIM
SB_WIDTH = SB_HEADS * HEAD_DIM
GM_WIDTH = GM_HEADS * HEAD_DIM
POOL_WIDTH = D_MIX - NSA_WIDTH - SB_WIDTH - GM_WIDTH
POOL_CH = POOL_WIDTH // POOL_GROUPS
POOL_WINDOWS = (2, 4, 8, 16)
POOL_MAX = 16
POOL_BUF = POOL_MAX - 1
CMP_STRIDE = 16
CMP_LEN = 2 * CMP_STRIDE
SEL_BLOCK = 64
N_TOPK = 16
WINDOW = 512
CHUNK = 128
Q_BLOCK = 128
D_FF = 2816
ROPE_THETA = 10000.0
EPS = 1e-6
FORCE_SCORE = 1e9
N_MOD = 9
PROJ_SIZES = (NSA_WIDTH, 6 * HEAD_DIM, 3 * NSA_HEADS, 3 * SB_WIDTH, 2 * GM_WIDTH, POOL_WIDTH)
PROJ_SPLITS = tuple(int(s) for s in np.cumsum(PROJ_SIZES)[:-1])
D_PROJ = int(sum(PROJ_SIZES))

kernel_name = 'hybrid_nsa_stickbreak_gmlp_pool_decoder_step'


def rmsnorm(x, g):
    xf = x.astype(jnp.float32)
    y = xf * lax.rsqrt(jnp.mean(xf * xf, axis=-1, keepdims=True) + EPS)
    return (y * g.astype(jnp.float32)).astype(x.dtype)


def layernorm(x, g, b):
    xf = x.astype(jnp.float32)
    xc = xf - jnp.mean(xf, axis=-1, keepdims=True)
    y = xc * lax.rsqrt(jnp.mean(xc * xc, axis=-1, keepdims=True) + EPS)
    return (y * g.astype(jnp.float32) + b.astype(jnp.float32)).astype(x.dtype)


def swiglu(h, w_up, w_down):
    gate, up = jnp.split(h @ w_up, 2, axis=-1)
    return (jax.nn.silu(gate) * up) @ w_down


def rope(x, pos):
    half = x.shape[-1] // 2
    inv = ROPE_THETA ** (-jnp.arange(half, dtype=jnp.float32) / half)
    ang = pos.astype(jnp.float32)[:, None] * inv[None, :]
    ang = ang.reshape((ang.shape[0],) + (1,) * (x.ndim - 3) + (half,))
    cos, sin = jnp.cos(ang), jnp.sin(ang)
    x1 = x[..., :half].astype(jnp.float32)
    x2 = x[..., half:].astype(jnp.float32)
    return jnp.concatenate([x1 * cos - x2 * sin, x2 * cos + x1 * sin], axis=-1).astype(x.dtype)


def masked_softmax(s, mask):
    s = jnp.where(mask, s.astype(jnp.float32), -jnp.inf)
    m = jnp.max(s, axis=-1, keepdims=True)
    m = jnp.where(jnp.isfinite(m), m, 0.0)
    e = jnp.exp(s - m)
    return e / jnp.maximum(jnp.sum(e, axis=-1, keepdims=True), 1e-30)


def take_rows(rows, idx):
    return jax.vmap(lambda r, i: r[i])(rows, idx)


def unblock(o):
    nb, bsz, qb = o.shape[:3]
    return jnp.moveaxis(o, 0, 1).reshape((bsz, nb * qb) + o.shape[3:])


def compress(rows, pe, w):
    bsz, t_len = rows.shape[:2]
    r = rows.reshape(bsz, t_len // CMP_STRIDE, CMP_STRIDE, 2, HEAD_DIM)
    blk = jnp.concatenate([r[:, :-1], r[:, 1:]], axis=2)
    return jnp.einsum('bnlcd,clde->bnce', blk + pe, w)


def nsa_block(q, g, qpos, kv_cmp, fetch_sel, kv_win, kwpos):
    bsz, n_q = q.shape[:2]
    scale = HEAD_DIM ** -0.5
    n_cmp = kv_cmp.shape[1]
    s = jnp.einsum('bqhd,bnd->bhqn', q, kv_cmp[:, :, 0]).astype(jnp.float32) * scale
    blk_end = jnp.arange(n_cmp) * CMP_STRIDE + CMP_LEN - 1
    p_cmp = masked_softmax(s, blk_end[None, :] <= qpos[:, None])
    o_cmp = jnp.einsum('bhqn,bnd->bqhd', p_cmp.astype(q.dtype), kv_cmp[:, :, 1])
    ps = jnp.sum(p_cmp, axis=1)
    pq = jnp.pad(ps, ((0, 0), (0, 0), (1, 0))) + jnp.pad(ps, ((0, 0), (0, 0), (0, 1)))
    ratio = SEL_BLOCK // CMP_STRIDE
    n_sel = (n_cmp + 1) // ratio
    p_slc = pq.reshape(bsz, n_q, n_sel, ratio).sum(-1)
    j = jnp.arange(n_sel)[None, :]
    cur = (qpos // SEL_BLOCK)[:, None]
    forced = (j == 0) | (j == cur) | (j == cur - 1)
    score = jnp.where(j <= cur, jnp.where(forced, FORCE_SCORE, p_slc), -jnp.inf)
    top_s, top_j = lax.top_k(score, min(N_TOPK, n_sel))
    tok = (top_j[..., None] * SEL_BLOCK + jnp.arange(SEL_BLOCK)).reshape(bsz, n_q, -1)
    smask = jnp.repeat(jnp.isfinite(top_s), SEL_BLOCK, axis=-1) & (tok <= qpos[None, :, None])
    kv_sel = fetch_sel(tok)
    s = jnp.einsum('bqhd,bqkd->bhqk', q, kv_sel[..., 0, :]).astype(jnp.float32) * scale
    p_sel = masked_softmax(s, smask[:, None])
    o_sel = jnp.einsum('bhqk,bqkd->bqhd', p_sel.astype(q.dtype), kv_sel[..., 1, :])
    s = jnp.einsum('bqhd,bkd->bhqk', q, kv_win[:, :, 0]).astype(jnp.float32) * scale
    dist = qpos[:, None] - kwpos[None, :]
    p_win = masked_softmax(s, (dist >= 0) & (dist <= WINDOW) & (kwpos[None, :] >= 0))
    o_win = jnp.einsum('bhqk,bkd->bqhd', p_win.astype(q.dtype), kv_win[:, :, 1])
    return g[..., 0:1] * o_cmp + g[..., 1:2] * o_sel + g[..., 2:3] * o_win


def stick_breaking(q, k, v, qpos, kpos):
    z = jnp.einsum('bqhd,bshd->bhqs', q, k).astype(jnp.float32) * HEAD_DIM ** -0.5
    mask = kpos[None, :] < qpos[:, None]
    log_keep = jnp.where(mask, -jax.nn.softplus(z), 0.0)
    after = lax.cumsum(log_keep, axis=3, reverse=True) - log_keep
    a = jnp.where(mask, jnp.exp(jax.nn.log_sigmoid(z) + after), 0.0)
    return jnp.einsum('bhqs,bshd->bqhd', a.astype(v.dtype), v)


def spatial_gating(u, v, w, b):
    bsz, t_len = v.shape[:2]
    pad = (-t_len) % CHUNK
    vp = jnp.pad(v, ((0, 0), (0, pad), (0, 0))).reshape(bsz, -1, CHUNK, GM_HEADS, HEAD_DIM)
    wm = w * jnp.tril(jnp.ones((CHUNK, CHUNK), w.dtype))
    s = jnp.einsum('hij,bcjhd->bcihd', wm, vp) + b.T[None, None, :, :, None]
    return u * s.reshape(bsz, -1, GM_WIDTH)[:, :t_len]


def pool_mix(p_ext, n_prefix, pos0, w, scale):
    bsz, t_ext = p_ext.shape[:2]
    t_len = t_ext - n_prefix
    cs = jnp.cumsum(jnp.pad(p_ext.astype(jnp.float32), ((0, 0), (POOL_MAX, 0), (0, 0))), axis=1)
    base = POOL_MAX + n_prefix
    hi = cs[:, base:base + t_len]
    tpos = pos0 + jnp.arange(t_len)
    means = []
    for gi, wlen in enumerate(POOL_WINDOWS):
        c0, c1 = gi * POOL_CH, (gi + 1) * POOL_CH
        lo = cs[:, base - wlen:base - wlen + t_len, c0:c1]
        cnt = jnp.minimum(wlen, tpos + 1).astype(jnp.float32)[None, :, None]
        means.append((hi[..., c0:c1] - lo) / cnt)
    d = jnp.concatenate(means, axis=-1) - p_ext[:, n_prefix:].astype(jnp.float32)
    d = d.astype(p_ext.dtype).reshape(bsz, t_len, POOL_GROUPS, POOL_CH)
    return jnp.einsum('btgc,gce->btge', d, w).reshape(bsz, t_len, POOL_WIDTH) * scale


def mixer_inputs(proj, pos, sg_ln_g, sg_ln_b):
    bsz, t_len = proj.shape[:2]
    a_q, a_kv, a_g, b_qkv, c_uv, d_in = jnp.split(proj, PROJ_SPLITS, axis=-1)
    q_a = rope(a_q.reshape(bsz, t_len, NSA_HEADS, HEAD_DIM), pos)
    kv = a_kv.reshape(bsz, t_len, 3, 2, HEAD_DIM)
    kv = jnp.stack([rope(kv[:, :, :, 0], pos), kv[:, :, :, 1]], axis=3)
    g_a = jax.nn.sigmoid(a_g.reshape(bsz, t_len, NSA_HEADS, 3))
    qkv_b = b_qkv.reshape(bsz, t_len, 3, SB_HEADS, HEAD_DIM)
    u, v = jnp.split(jax.nn.gelu(c_uv), 2, axis=-1)
    v = layernorm(v, sg_ln_g, sg_ln_b)
    return q_a, kv[:, :, 0], kv[:, :, 1], kv[:, :, 2], g_a, qkv_b, u, v, d_in


def mix_prompt(proj, cmp_pe, cmp_w, sg_ln_g, sg_ln_b, sg_w, sg_b, pool_w, pool_scale):
    bsz, t_len = proj.shape[:2]
    pos = jnp.arange(t_len)
    q_a, kv_cmp_rows, kv_sel_rows, kv_win_rows, g_a, qkv_b, u, v, p_in = mixer_inputs(proj, pos, sg_ln_g, sg_ln_b)
    kv_cmp = compress(kv_cmp_rows, cmp_pe, cmp_w)
    kv_win_pad = jnp.pad(kv_win_rows, ((0, 0), (WINDOW, 0), (0, 0), (0, 0)))
    fetch = functools.partial(take_rows, kv_sel_rows)
    q_b, k_b, v_b = qkv_b[:, :, 0], qkv_b[:, :, 1], qkv_b[:, :, 2]

    def nsa_qblock(i):
        s0 = i * Q_BLOCK
        qpos = s0 + jnp.arange(Q_BLOCK)
        kw = lax.dynamic_slice_in_dim(kv_win_pad, s0, WINDOW + Q_BLOCK, axis=1)
        kwpos = s0 - WINDOW + jnp.arange(WINDOW + Q_BLOCK)
        return nsa_block(lax.dynamic_slice_in_dim(q_a, s0, Q_BLOCK, axis=1),
                         lax.dynamic_slice_in_dim(g_a, s0, Q_BLOCK, axis=1),
                         qpos, kv_cmp, fetch, kw, kwpos)

    def sb_qblock(i):
        s0 = i * Q_BLOCK
        qb = lax.dynamic_slice_in_dim(q_b, s0, Q_BLOCK, axis=1)
        return stick_breaking(qb, k_b, v_b, s0 + jnp.arange(Q_BLOCK), pos)

    blocks = jnp.arange(t_len // Q_BLOCK)
    o_a = unblock(lax.map(nsa_qblock, blocks)).reshape(bsz, t_len, NSA_WIDTH)
    o_b = unblock(lax.map(sb_qblock, blocks)).reshape(bsz, t_len, SB_WIDTH)
    o_c = spatial_gating(u, v, sg_w, sg_b)
    o_d = pool_mix(p_in, 0, 0, pool_w, pool_scale)
    out = jnp.concatenate([o_a, o_b, o_c, o_d], axis=-1)
    n_win = min(WINDOW, t_len)
    state = (kv_cmp_rows, kv_sel_rows, kv_win_rows[:, t_len - n_win:], qkv_b[:, :, 1:], p_in[:, t_len - POOL_BUF:])
    return out, state


def mix_sample(proj, cache_nsa_cmp, cache_nsa_sel, cache_nsa_win, cache_sb, state_pool, page_table, layer,
               cmp_pe, cmp_w, sg_ln_g, sg_ln_b, sg_w, sg_b, pool_w, pool_scale):
    bsz, t_len = proj.shape[:2]
    past_len = page_table.shape[1] * PAGE_SIZE
    pos = past_len + jnp.arange(t_len)
    q_a, kv_cmp_rows, kv_sel_rows, kv_win_rows, g_a, qkv_b, u, v, p_in = mixer_inputs(proj, pos, sg_ln_g, sg_ln_b)
    past_cmp = cache_nsa_cmp[layer, page_table].reshape(bsz, past_len, 2, HEAD_DIM)
    rows = jnp.concatenate([past_cmp, kv_cmp_rows], axis=1)
    rows = jnp.pad(rows, ((0, 0), (0, (-rows.shape[1]) % SEL_BLOCK), (0, 0), (0, 0)))
    kv_cmp = compress(rows, cmp_pe, cmp_w)

    def fetch(tok):
        in_past = tok < past_len
        tp = jnp.clip(tok, 0, past_len - 1)
        phys = jnp.take_along_axis(page_table, (tp // PAGE_SIZE).reshape(bsz, -1), axis=1).reshape(tok.shape)
        rows_past = cache_nsa_sel[layer, phys, tp % PAGE_SIZE]
        rows_new = take_rows(kv_sel_rows, jnp.clip(tok - past_len, 0, t_len - 1))
        return jnp.where(in_past[..., None, None], rows_past, rows_new)

    n_buf = cache_nsa_win.shape[2]
    win = jnp.concatenate([cache_nsa_win[layer], kv_win_rows], axis=1)
    kwpos = past_len - n_buf + jnp.arange(n_buf + t_len)
    o_a = nsa_block(q_a, g_a, pos, kv_cmp, fetch, win, kwpos).reshape(bsz, t_len, NSA_WIDTH)
    past_sb = cache_sb[layer, page_table].reshape(bsz, past_len, 2, SB_HEADS, HEAD_DIM)
    kv_sb = jnp.concatenate([past_sb, qkv_b[:, :, 1:]], axis=1)
    o_b = stick_breaking(qkv_b[:, :, 0], kv_sb[:, :, 0], kv_sb[:, :, 1], pos,
                         jnp.arange(past_len + t_len)).reshape(bsz, t_len, SB_WIDTH)
    o_c = spatial_gating(u, v, sg_w, sg_b)
    p_ext = jnp.concatenate([state_pool[layer], p_in], axis=1)
    o_d = pool_mix(p_ext, POOL_BUF, past_len, pool_w, pool_scale)
    out = jnp.concatenate([o_a, o_b, o_c, o_d], axis=-1)
    state = (kv_cmp_rows, kv_sel_rows, win[:, win.shape[1] - n_buf:], qkv_b[:, :, 1:],
             p_ext[:, p_ext.shape[1] - POOL_BUF:], v)
    return out, state


def trunk_layer(x, c, w_ada, b_ada, norm_g, w_up, w_down, w_in, w_out, mix):
    mod = (jax.nn.silu(c) @ w_ada + b_ada).reshape(c.shape[0], 1, N_MOD, D_MODEL)
    h = rmsnorm(x, norm_g[0]) * (1 + mod[:, :, 1]) + mod[:, :, 0]
    x = x + 0.5 * mod[:, :, 2] * rmsnorm(swiglu(h, w_up[0], w_down[0]), norm_g[1])
    h = rmsnorm(x, norm_g[2]) * (1 + mod[:, :, 4]) + mod[:, :, 3]
    o, state = mix(h @ w_in)
    x = x + mod[:, :, 5] * rmsnorm(o @ w_out, norm_g[3])
    h = rmsnorm(x, norm_g[4]) * (1 + mod[:, :, 7]) + mod[:, :, 6]
    x = x + 0.5 * mod[:, :, 8] * rmsnorm(swiglu(h, w_up[1], w_down[1]), norm_g[5])
    return x, state


def stack_layers(states, i):
    return jnp.stack([s[i] for s in states], axis=0)


def setup_inputs(seed: int = 0) -> dict:
    key = jax.random.key(seed)
    ks = jax.random.split(key, 32)
    f32 = jnp.float32

    def nrm(k, shape, s=1.0):
        return s * jax.random.normal(k, shape, f32)

    n_pages = PAST_LEN // PAGE_SIZE
    n_used = DEC_BATCH * n_pages
    n_phys = n_used + max(1, n_used // 4)
    w_buf = min(WINDOW, PAST_LEN)
    page_table = jax.random.permutation(ks[0], n_phys)[:n_used].reshape(DEC_BATCH, n_pages).astype(jnp.int32)
    return {
        'x_prompt': nrm(ks[1], (BATCH, SEQ, D_MODEL)),
        'x_sample': nrm(ks[2], (DEC_BATCH, DEC_SEQ, D_MODEL)),
        'cache_nsa_cmp': nrm(ks[3], (DEPTH, n_phys, PAGE_SIZE, 2, HEAD_DIM)),
        'cache_nsa_sel': nrm(ks[4], (DEPTH, n_phys, PAGE_SIZE, 2, HEAD_DIM)),
        'cache_nsa_win': nrm(ks[5], (DEPTH, DEC_BATCH, w_buf, 2, HEAD_DIM)),
        'cache_sb': nrm(ks[6], (DEPTH, n_phys, PAGE_SIZE, 2, SB_HEADS, HEAD_DIM)),
        'state_pool': nrm(ks[7], (DEPTH, DEC_BATCH, POOL_BUF, POOL_WIDTH)),
        'page_table': page_table,
        'c_prompt': nrm(ks[8], (BATCH, D_MODEL)),
        'c_sample': nrm(ks[9], (DEC_BATCH, D_MODEL)),
        'w_ada': nrm(ks[10], (DEPTH, D_MODEL, N_MOD * D_MODEL), 0.5 * D_MODEL ** -0.5),
        'b_ada': nrm(ks[11], (DEPTH, N_MOD * D_MODEL), 0.01),
        'norm_g': 1.0 + nrm(ks[12], (DEPTH, 6, D_MODEL), 0.05),
        'w_ffn_up': nrm(ks[13], (DEPTH, 2, D_MODEL, 2 * D_FF), D_MODEL ** -0.5),
        'w_ffn_down': nrm(ks[14], (DEPTH, 2, D_FF, D_MODEL), D_FF ** -0.5),
        'w_in': nrm(ks[15], (DEPTH, D_MODEL, D_PROJ), D_MODEL ** -0.5),
        'w_out': nrm(ks[16], (DEPTH, D_MIX, D_MODEL), D_MIX ** -0.5),
        'cmp_pe': nrm(ks[17], (DEPTH, CMP_LEN, 2, HEAD_DIM), 0.1),
        'cmp_w': nrm(ks[18], (DEPTH, 2, CMP_LEN, HEAD_DIM, HEAD_DIM), (CMP_LEN * HEAD_DIM) ** -0.5),
        'sg_ln_g': 1.0 + nrm(ks[19], (DEPTH, GM_WIDTH), 0.05),
        'sg_ln_b': nrm(ks[20], (DEPTH, GM_WIDTH), 0.02),
        'sg_w': nrm(ks[21], (DEPTH, GM_HEADS, CHUNK, CHUNK), CHUNK ** -0.5),
        'sg_b': 1.0 + nrm(ks[22], (DEPTH, GM_HEADS, CHUNK), 0.1),
        'pool_w': nrm(ks[23], (DEPTH, POOL_GROUPS, POOL_CH, POOL_CH), POOL_CH ** -0.5),
        'pool_scale': 0.5 + nrm(ks[24], (DEPTH, POOL_WIDTH), 0.1),
    }


def reference(x_prompt, x_sample, cache_nsa_cmp, cache_nsa_sel, cache_nsa_win, cache_sb, state_pool, page_table,
              c_prompt, c_sample, w_ada, b_ada, norm_g, w_ffn_up, w_ffn_down, w_in, w_out,
              cmp_pe, cmp_w, sg_ln_g, sg_ln_b, sg_w, sg_b, pool_w, pool_scale):
    y_p, y_s = x_prompt, x_sample
    st_p, st_s = [], []
    for l in range(DEPTH):
        mixer_w = dict(cmp_pe=cmp_pe[l], cmp_w=cmp_w[l], sg_ln_g=sg_ln_g[l], sg_ln_b=sg_ln_b[l],
                       sg_w=sg_w[l], sg_b=sg_b[l], pool_w=pool_w[l], pool_scale=pool_scale[l])
        layer_w = (w_ada[l], b_ada[l], norm_g[l], w_ffn_up[l], w_ffn_down[l], w_in[l], w_out[l])
        y_p, s = trunk_layer(y_p, c_prompt, *layer_w, functools.partial(mix_prompt, **mixer_w))
        st_p.append(s)
        mix_s = functools.partial(mix_sample, cache_nsa_cmp=cache_nsa_cmp, cache_nsa_sel=cache_nsa_sel,
                                  cache_nsa_win=cache_nsa_win, cache_sb=cache_sb, state_pool=state_pool,
                                  page_table=page_table, layer=l, **mixer_w)
        y_s, s = trunk_layer(y_s, c_sample, *layer_w, mix_s)
        st_s.append(s)
    return (y_p, y_s,
            stack_layers(st_p, 0), stack_layers(st_s, 0),
            stack_layers(st_p, 1), stack_layers(st_s, 1),
            stack_layers(st_p, 2), stack_layers(st_s, 2),
            stack_layers(st_p, 3), stack_layers(st_s, 3),
            stack_layers(st_p, 4), stack_layers(st_s, 4),
            stack_layers(st_s, 5))
```

```python
import functools

import numpy as np
import jax
import jax.numpy as jnp
from jax import lax
from jax.experimental import pallas as pl
from jax.experimental.pallas import tpu as pltpu

F32 = jnp.float32
BF = jnp.bfloat16

D_MODEL = 1024
HEAD_DIM = 64
N_HEADS = 4
GROUP_W = N_HEADS * HEAD_DIM
D_FF = 2816
N_MOD = 9
PAGE = 128
CMP_STRIDE = 16
CMP_LEN = 32
SEL_BLOCK = 64
N_TOPK = 16
WINDOW = 512
CHUNK = 128
POOL_WINDOWS = (2, 4, 8, 16)
POOL_MAX = 16
POOL_BUF = POOL_MAX - 1
ROPE_THETA = 10000.0
EPS = 1e-6
FORCE_SCORE = 1e9
NEG = -3.0e38
LANES = 128
VMEM_LIMIT = 56 * 1024 * 1024

C_QA, C_QAS, C_KV, C_KVS, C_G, C_QB, C_KVB, C_UV, C_DIN, C_END = (
    0, 512, 1024, 1408, 1792, 1920, 2176, 2688, 3200, 3456)


def _params(sem, vmem=VMEM_LIMIT):
    return pltpu.CompilerParams(dimension_semantics=sem, vmem_limit_bytes=vmem)


def _dot(a, b):
    return jnp.dot(a, b, preferred_element_type=F32)


def _dot_nt(a, b):
    return lax.dot_general(a, b, (((1,), (1,)), ((), ())), preferred_element_type=F32)


def _rms(x, g):
    return x * lax.rsqrt(jnp.mean(x * x, axis=-1, keepdims=True) + EPS) * g


def _split3(x):
    hi = x.astype(BF)
    r = x - hi.astype(F32)
    mid = r.astype(BF)
    lo = (r - mid.astype(F32)).astype(BF)
    return hi, mid, lo


def _ada_kernel(c_ref, w_ref, b_ref, o_ref):
    c = c_ref[...]
    s = (c * jax.nn.sigmoid(c)).astype(BF)
    o_ref[...] = _dot(s, w_ref[...]) + b_ref[...]


def ada_mod(c_all, w_ada, b_ada):
    r = c_all.shape[0]
    n = w_ada.shape[1]
    tn = 2304
    return pl.pallas_call(
        _ada_kernel,
        grid=(n // tn,),
        in_specs=[pl.BlockSpec((r, D_MODEL), lambda j: (0, 0)),
                  pl.BlockSpec((D_MODEL, tn), lambda j: (0, j)),
                  pl.BlockSpec((1, tn), lambda j: (0, j))],
        out_specs=pl.BlockSpec((r, tn), lambda j: (0, j)),
        out_shape=jax.ShapeDtypeStruct((r, n), F32),
        compiler_params=_params(("parallel",)),
        name="ada_mod",
    )(c_all, w_ada, b_ada)


FF_CHUNK = 256


def _ffn_kernel(x_ref, sh_ref, sc_ref, gt_ref, g1_ref, g2_ref, wu_ref, wd_ref, o_ref):
    x = x_ref[...]
    h = _rms(x, g1_ref[...]) * (1.0 + sc_ref[...]) + sh_ref[...]
    hb = h.astype(BF)
    acc = jnp.zeros(x.shape, F32)
    for c in range(D_FF // FF_CHUNK):
        lo = c * FF_CHUNK
        gate = _dot(hb, wu_ref[:, lo:lo + FF_CHUNK])
        up = _dot(hb, wu_ref[:, D_FF + lo:D_FF + lo + FF_CHUNK])
        a = (gate * jax.nn.sigmoid(gate) * up).astype(BF)
        acc = acc + _dot(a, wd_ref[lo:lo + FF_CHUNK, :])
    o_ref[...] = x + 0.5 * gt_ref[...] * _rms(acc, g2_ref[...])


def _mod_spec(mod, tm):
    if mod.shape[1] == 1:
        return pl.BlockSpec((None, 1, D_MODEL), lambda b, t: (b, 0, 0))
    return pl.BlockSpec((None, tm, D_MODEL), lambda b, t: (b, t, 0))


def _row_spec(tm, w):
    return pl.BlockSpec((None, tm, w), lambda b, t: (b, t, 0))


def _const_spec(shape):
    nd = len(shape)
    return pl.BlockSpec(shape, lambda b, t: (0,) * nd)


def ffn_half(x, shift, scale, gate, g1, g2, w_up, w_down, tm):
    bsz, t_len, _ = x.shape
    return pl.pallas_call(
        _ffn_kernel,
        grid=(bsz, t_len // tm),
        in_specs=[_row_spec(tm, D_MODEL), _mod_spec(shift, tm), _mod_spec(scale, tm), _mod_spec(gate, tm),
                  _const_spec((1, D_MODEL)), _const_spec((1, D_MODEL)),
                  _const_spec((D_MODEL, 2 * D_FF)), _const_spec((D_FF, D_MODEL))],
        out_specs=_row_spec(tm, D_MODEL),
        out_shape=jax.ShapeDtypeStruct(x.shape, F32),
        compiler_params=_params(("parallel", "parallel")),
        name="ffn_half",
    )(x, shift, scale, gate, g1, g2, w_up, w_down)


def _gelu_tanh(x):
    return 0.5 * x * (1.0 + jnp.tanh(np.sqrt(2.0 / np.pi).astype(np.float32) * (x + 0.044715 * (x * x * x))))


PROJ_IN = ("x", "shift", "scale", "g", "cos", "sin", "w", "ln_g", "ln_b")
PROJ_IN_T = ("w_t", "cos_t", "sin_t")
PROJ_OUT = (("qa", 2 * GROUP_W, BF), ("kv_cmp", LANES, F32), ("gate", LANES, F32), ("qb", GROUP_W, BF),
            ("u", GROUP_W, F32), ("v", GROUP_W, F32), ("pin", GROUP_W, F32))
PROJ_OUT_ROWS = (("kv_sel", LANES, F32), ("kv_win", LANES, F32), ("kvb", 2 * GROUP_W, F32))
PROJ_OUT_B16 = (("kv_sel_b", LANES, BF), ("kv_win_b", LANES, BF), ("kvb_b", 2 * GROUP_W, BF))
PROJ_OUT_T = (("kvt_cmp", LANES), ("kvt_sel", LANES), ("kvt_win", LANES), ("kvbt", 2 * GROUP_W))


def _proj_kernel(*refs, feature_major):
    names = PROJ_IN + (PROJ_IN_T if feature_major else ())
    names += tuple(n for n, _, _ in PROJ_OUT + (PROJ_OUT_B16 if feature_major else PROJ_OUT_ROWS))
    names += tuple(n for n, _ in PROJ_OUT_T) if feature_major else ()
    r = dict(zip(names, refs))
    x = r["x"][...]
    h = _rms(x, r["g"][...]) * (1.0 + r["scale"][...]) + r["shift"][...]
    hb = h.astype(BF)
    cos = r["cos"][...]
    sin = r["sin"][...]
    w_ref = r["w"]

    def mm(lo, hi):
        return _dot(hb, w_ref[:, lo:hi])

    for j in range(N_HEADS):
        p = mm(C_QA + LANES * j, C_QA + LANES * (j + 1))
        ps = mm(C_QAS + LANES * j, C_QAS + LANES * (j + 1))
        r["qa"][:, LANES * j:LANES * (j + 1)] = (p * cos + ps * sin).astype(BF)
    lane = lax.broadcasted_iota(jnp.int32, cos.shape, 1)
    ckv = jnp.where(lane < HEAD_DIM, cos, 1.0)
    for j, nm in enumerate(("kv_cmp", "kv_sel", "kv_win")):
        p = mm(C_KV + LANES * j, C_KV + LANES * (j + 1))
        ps = mm(C_KVS + LANES * j, C_KVS + LANES * (j + 1))
        kv = p * ckv + ps * sin
        if nm in r:
            r[nm][...] = kv
        if nm + "_b" in r:
            r[nm + "_b"][...] = kv.astype(BF)
    r["gate"][...] = jax.nn.sigmoid(mm(C_G, C_QB))
    r["qb"][...] = mm(C_QB, C_KVB).astype(BF)
    kvb = mm(C_KVB, C_UV)
    if feature_major:
        r["kvb_b"][...] = kvb.astype(BF)
    else:
        r["kvb"][...] = kvb
    r["u"][...] = _gelu_tanh(mm(C_UV, C_UV + GROUP_W))
    v = _gelu_tanh(mm(C_UV + GROUP_W, C_DIN))
    vc = v - jnp.mean(v, axis=-1, keepdims=True)
    vn = vc * lax.rsqrt(jnp.mean(vc * vc, axis=-1, keepdims=True) + EPS)
    r["v"][...] = vn * r["ln_g"][...] + r["ln_b"][...]
    r["pin"][...] = mm(C_DIN, C_END)
    if feature_major:
        wt_ref = r["w_t"]
        cos_t = r["cos_t"][...]
        sin_t = r["sin_t"][...]
        n_kv = 3 * LANES
        for j, nm in enumerate(("kvt_cmp", "kvt_sel", "kvt_win")):
            p = _dot_nt(wt_ref[LANES * j:LANES * (j + 1), :], hb)
            ps = _dot_nt(wt_ref[n_kv + LANES * j:n_kv + LANES * (j + 1), :], hb)
            r[nm][...] = p * cos_t + ps * sin_t
        r["kvbt"][...] = _dot_nt(wt_ref[2 * n_kv:2 * n_kv + 2 * GROUP_W, :], hb)


def in_proj(x, shift, scale, g, cos, sin, w_all, ln_g, ln_b, tm, t_side=None):
    bsz, t_len, _ = x.shape
    feature_major = t_side is not None
    tab_spec = pl.BlockSpec((tm, LANES), lambda b, t: (t, 0))
    in_specs = [_row_spec(tm, D_MODEL), _mod_spec(shift, tm), _mod_spec(scale, tm),
                _const_spec((1, D_MODEL)), tab_spec, tab_spec,
                _const_spec((D_MODEL, C_END)), _const_spec((1, GROUP_W)), _const_spec((1, GROUP_W))]
    args = [x, shift, scale, g, cos, sin, w_all, ln_g, ln_b]
    outs = PROJ_OUT + (PROJ_OUT_B16 if feature_major else PROJ_OUT_ROWS)
    out_specs = [_row_spec(tm, w) for _, w, _ in outs]
    out_shape = [jax.ShapeDtypeStruct((bsz, t_len, w), dt) for _, w, dt in outs]
    names = [n for n, _, _ in outs]
    if feature_major:
        tab_t_spec = pl.BlockSpec((LANES, tm), lambda b, t: (0, t))
        in_specs += [_const_spec(t_side[0].shape), tab_t_spec, tab_t_spec]
        args += list(t_side)
        out_specs += [pl.BlockSpec((None, w, tm), lambda b, t: (b, 0, t)) for _, w in PROJ_OUT_T]
        out_shape += [jax.ShapeDtypeStruct((bsz, w, t_len), F32) for _, w in PROJ_OUT_T]
        names += [n for n, _ in PROJ_OUT_T]
    res = pl.pallas_call(
        functools.partial(_proj_kernel, feature_major=feature_major),
        grid=(bsz, t_len // tm),
        in_specs=in_specs, out_specs=out_specs, out_shape=out_shape,
        compiler_params=_params(("parallel", "parallel")),
        name="in_proj",
    )(*args)
    return dict(zip(names, res))


def _outproj_kernel(x_ref, oa_ref, ob_ref, oc_ref, od_ref, gt_ref, g_ref, w_ref, o_ref):
    y = _dot(oa_ref[...], w_ref[0:GROUP_W, :])
    y = y + _dot(ob_ref[...], w_ref[GROUP_W:2 * GROUP_W, :])
    y = y + _dot(oc_ref[...], w_ref[2 * GROUP_W:3 * GROUP_W, :])
    y = y + _dot(od_ref[...], w_ref[3 * GROUP_W:4 * GROUP_W, :])
    o_ref[...] = x_ref[...] + gt_ref[...] * _rms(y, g_ref[...])


def out_proj(x, o_a, o_b, o_c, o_d, gate, g, w_out, tm):
    bsz, t_len, _ = x.shape
    return pl.pallas_call(
        _outproj_kernel,
        grid=(bsz, t_len // tm),
        in_specs=[_row_spec(tm, D_MODEL)] + [_row_spec(tm, GROUP_W)] * 4 +
                 [_mod_spec(gate, tm), _const_spec((1, D_MODEL)), _const_spec((D_MODEL, D_MODEL))],
        out_specs=_row_spec(tm, D_MODEL),
        out_shape=jax.ShapeDtypeStruct(x.shape, F32),
        compiler_params=_params(("parallel", "parallel")),
        name="out_proj",
    )(x, o_a, o_b, o_c, o_d, gate, g, w_out)


def _compress_rows(row_ref, n_grp, pe_ref, w_ref, base=0):
    lo_parts, hi_parts = [], []
    for l in range(CMP_STRIDE):
        a = row_ref[pl.ds(base + l, n_grp, stride=CMP_STRIDE), :]
        lo_parts.append((a + pe_ref[l:l + 1, :]).astype(BF))
        hi_parts.append((a + pe_ref[CMP_STRIDE + l:CMP_STRIDE + l + 1, :]).astype(BF))
    lo = _dot(jnp.concatenate(lo_parts, axis=1), w_ref[0])
    hi = _dot(jnp.concatenate(hi_parts, axis=1), w_ref[1])
    return lo, hi


def _compress_kernel(row_ref, pe_ref, w_ref, o_ref, hi_sc):
    n_grp = o_ref.shape[0]
    lo, hi = _compress_rows(row_ref, n_grp, pe_ref, w_ref)
    hi_sc[0:n_grp, :] = hi
    hi_sc[n_grp:n_grp + 8, :] = jnp.zeros((8, LANES), F32)
    o_ref[...] = (lo + hi_sc[pl.ds(1, n_grp), :]).astype(BF)


def compress_prompt(kv_cmp, pe, w_cmp):
    bsz, t_len, _ = kv_cmp.shape
    n_grp = t_len // CMP_STRIDE
    return pl.pallas_call(
        _compress_kernel,
        grid=(bsz,),
        in_specs=[pl.BlockSpec((None, t_len, LANES), lambda b: (b, 0, 0)),
                  pl.BlockSpec((CMP_LEN, LANES), lambda b: (0, 0)),
                  pl.BlockSpec((2, CMP_STRIDE * LANES, LANES), lambda b: (0, 0, 0))],
        out_specs=pl.BlockSpec((None, n_grp, LANES), lambda b: (b, 0, 0)),
        out_shape=jax.ShapeDtypeStruct((bsz, n_grp, LANES), BF),
        scratch_shapes=[pltpu.VMEM((n_grp + 8, LANES), F32)],
        compiler_params=_params(("parallel",)),
        name="nsa_compress",
    )(kv_cmp, pe, w_cmp)


NSA_Q = 128
NSA_KC = 512


def _softmax_rows(s, allow):
    sm = jnp.where(allow, s, NEG)
    m = jnp.max(sm, axis=-1, keepdims=True)
    e = jnp.where(allow, jnp.exp(sm - m), 0.0)
    return e / jnp.maximum(jnp.sum(e, axis=-1, keepdims=True), 1e-30)


def _topk_select(score, valid, n_pick, axis=0):
    n_blk = score.shape[axis]
    j_io = lax.broadcasted_iota(jnp.int32, score.shape, axis)
    sel = jnp.zeros(score.shape, F32)
    sc = score
    for _ in range(n_pick):
        m = jnp.max(sc, axis=axis, keepdims=True)
        idx = jnp.min(jnp.where(sc == m, j_io, n_blk), axis=axis, keepdims=True)
        pick = j_io == idx
        sel = jnp.where(pick, 1.0, sel)
        sc = jnp.where(pick, NEG, sc)
    return jnp.where(valid, sel, 0.0)


def _nsa_prompt_kernel(qa_ref, g_ref, kcmp_ref, ksel_ref, kwin_ref, mt_ref, e_ref, o_ref,
                       m_sc, l_sc, acc_sc):
    i = pl.program_id(1)
    s0 = i * NSA_Q
    n_cmp = kcmp_ref.shape[0]
    n_sel = mt_ref.shape[0]
    t_len = ksel_ref.shape[0]
    qs = jnp.concatenate([qa_ref[:, LANES * h:LANES * (h + 1)] for h in range(N_HEADS)], axis=0)
    qpos = s0 + lax.broadcasted_iota(jnp.int32, (NSA_Q, 1), 0)
    qpos4 = jnp.concatenate([qpos] * N_HEADS, axis=0)

    kc = kcmp_ref[...]
    s = _dot_nt(qs, kc)
    blk_end = lax.broadcasted_iota(jnp.int32, (1, n_cmp), 1) * CMP_STRIDE + (CMP_LEN - 1)
    p = _softmax_rows(s, blk_end <= qpos4)
    o_cmp = _dot(p.astype(BF), kc)
    ps = p[0:NSA_Q] + p[NSA_Q:2 * NSA_Q] + p[2 * NSA_Q:3 * NSA_Q] + p[3 * NSA_Q:4 * NSA_Q]
    mt = mt_ref[...]
    p_slc_t = sum(_dot_nt(mt, part) for part in _split3(ps))
    j_io = lax.broadcasted_iota(jnp.int32, (n_sel, NSA_Q), 0)
    cur = (s0 + lax.broadcasted_iota(jnp.int32, (1, NSA_Q), 1)) // SEL_BLOCK
    forced = (j_io == 0) | (j_io == cur) | (j_io == cur - 1)
    valid = j_io <= cur
    score_t = jnp.where(valid, jnp.where(forced, FORCE_SCORE, p_slc_t), NEG)
    sel = _topk_select(score_t, valid, min(N_TOPK, n_sel)).T.astype(BF)

    m_sc[...] = jnp.full(m_sc.shape, NEG, F32)
    l_sc[...] = jnp.zeros(l_sc.shape, F32)
    acc_sc[...] = jnp.zeros(acc_sc.shape, F32)
    rep = NSA_KC // LANES

    def sel_chunk(c, carry):
        k0 = pl.multiple_of(c * NSA_KC, NSA_KC)
        kv = ksel_ref[pl.ds(k0, NSA_KC), :]
        sc = _dot_nt(qs, kv)
        selx = _dot(sel, e_ref[:, pl.ds(k0, NSA_KC)])
        tok = k0 + lax.broadcasted_iota(jnp.int32, (1, NSA_KC), 1)
        allow = (selx > 0.5) & (tok <= qpos)
        for h in range(N_HEADS):
            sm = jnp.where(allow, sc[h * NSA_Q:(h + 1) * NSA_Q], NEG)
            m_old = m_sc[h]
            m_new = jnp.maximum(m_old, jnp.max(sm, axis=-1, keepdims=True))
            alpha = jnp.exp(m_old - m_new)
            pe = jnp.where(allow, jnp.exp(sm - jnp.concatenate([m_new] * rep, axis=1)), 0.0)
            l_sc[h] = alpha * l_sc[h] + jnp.sum(pe, axis=-1, keepdims=True)
            acc_sc[h] = alpha * acc_sc[h] + _dot(pe.astype(BF), kv)
            m_sc[h] = m_new
        return carry

    lax.fori_loop(0, (s0 + NSA_Q + NSA_KC - 1) // NSA_KC, sel_chunk, 0)

    n_win = WINDOW + NSA_Q
    w0 = pl.multiple_of(jnp.maximum(s0 - WINDOW, 0), NSA_Q)
    kvw = kwin_ref[pl.ds(w0, n_win), :]
    sw = _dot_nt(qs, kvw)
    dist = qpos4 - (w0 + lax.broadcasted_iota(jnp.int32, (1, n_win), 1))
    pw = _softmax_rows(sw, (dist >= 0) & (dist <= WINDOW))
    o_win = _dot(pw.astype(BF), kvw)

    g = g_ref[...]
    outs = []
    for h in range(N_HEADS):
        rows = slice(h * NSA_Q, (h + 1) * NSA_Q)
        o_sel = acc_sc[h] / jnp.maximum(l_sc[h], 1e-30)
        o = (g[:, 3 * h:3 * h + 1] * o_cmp[rows] + g[:, 3 * h + 1:3 * h + 2] * o_sel
             + g[:, 3 * h + 2:3 * h + 3] * o_win[rows])
        outs.append(o[:, HEAD_DIM:])
    o_ref[...] = jnp.concatenate(outs, axis=1).astype(BF)


def nsa_prompt(qa, gate, kcmp, ksel_b, kwin_b, mt, e_mat):
    bsz, t_len, _ = qa.shape
    n_cmp = kcmp.shape[1]
    n_sel = mt.shape[0]
    return pl.pallas_call(
        _nsa_prompt_kernel,
        grid=(bsz, t_len // NSA_Q),
        in_specs=[_row_spec(NSA_Q, 2 * GROUP_W), _row_spec(NSA_Q, LANES),
                  pl.BlockSpec((None, n_cmp, LANES), lambda b, t: (b, 0, 0)),
                  pl.BlockSpec((None, t_len, LANES), lambda b, t: (b, 0, 0)),
                  pl.BlockSpec((None, t_len, LANES), lambda b, t: (b, 0, 0)),
                  _const_spec((n_sel, n_cmp)), _const_spec((n_sel, t_len))],
        out_specs=_row_spec(NSA_Q, GROUP_W),
        out_shape=jax.ShapeDtypeStruct((bsz, t_len, GROUP_W), BF),
        scratch_shapes=[pltpu.VMEM((N_HEADS, NSA_Q, LANES), F32),
                        pltpu.VMEM((N_HEADS, NSA_Q, LANES), F32),
                        pltpu.VMEM((N_HEADS, NSA_Q, LANES), F32)],
        compiler_params=_params(("parallel", "arbitrary")),
        name="nsa_prompt",
    )(qa, gate, kcmp, ksel_b, kwin_b, mt, e_mat)


SB_Q = 256
SB_K = 256


def _softplus(z):
    return jnp.maximum(z, 0.0) + jnp.log1p(jnp.exp(-jnp.abs(z)))


def _tri_ones(n):
    r = lax.broadcasted_iota(jnp.int32, (n, n + LANES), 0)
    c = lax.broadcasted_iota(jnp.int32, (n, n + LANES), 1)
    return jnp.where((r > c) | (c >= n), 1.0, 0.0).astype(BF)


def _sb_chunk(qh, k, v, tri, carry, mask):
    n_k = k.shape[0]
    z = _dot_nt(qh, k)
    sp = _softplus(z)
    lk = -sp if mask is None else jnp.where(mask, -sp, 0.0)
    hi = lk.astype(BF)
    lo = (lk - hi.astype(F32)).astype(BF)
    cs = _dot(hi, tri) + _dot(lo, tri)
    after = cs[:, :n_k] + jnp.concatenate([carry] * (n_k // LANES), axis=1)
    a = jnp.exp(z - sp + after)
    if mask is not None:
        a = jnp.where(mask, a, 0.0)
    return _dot(a.astype(BF), v), carry + cs[:, n_k:]


def _sb_prompt_kernel(qb_ref, kvb_ref, o_ref, acc_sc, car_sc):
    i = pl.program_id(1)
    q = qb_ref[...]
    lane = lax.broadcasted_iota(jnp.int32, (1, GROUP_W), 1)
    qh = [jnp.where(lane // HEAD_DIM == h, q, jnp.zeros_like(q)) for h in range(N_HEADS)]
    tri = _tri_ones(SB_K)
    r = lax.broadcasted_iota(jnp.int32, (SB_Q, SB_K), 0)
    c = lax.broadcasted_iota(jnp.int32, (SB_Q, SB_K), 1)
    diag_mask = c < r

    def chunk(k0, mask, first):
        k = kvb_ref[pl.ds(k0, SB_K), 0:GROUP_W]
        v = kvb_ref[pl.ds(k0, SB_K), GROUP_W:2 * GROUP_W]
        for h in range(N_HEADS):
            car = jnp.zeros((SB_Q, LANES), F32) if first else car_sc[h]
            pv, car = _sb_chunk(qh[h], k, v, tri, car, mask)
            acc_sc[h] = pv if first else acc_sc[h] + pv
            car_sc[h] = car

    chunk(pl.multiple_of(i * SB_Q, SB_Q), diag_mask, True)

    def body(j, carry):
        chunk(pl.multiple_of((i - 1 - j) * SB_K, SB_K), None, False)
        return carry

    lax.fori_loop(0, i, body, 0)
    o = jnp.zeros((SB_Q, GROUP_W), F32)
    for h in range(N_HEADS):
        o = jnp.where(lane // HEAD_DIM == h, acc_sc[h], o)
    o_ref[...] = o.astype(BF)


def sb_prompt(qb, kvb_b):
    bsz, t_len, _ = qb.shape
    return pl.pallas_call(
        _sb_prompt_kernel,
        grid=(bsz, t_len // SB_Q),
        in_specs=[_row_spec(SB_Q, GROUP_W),
                  pl.BlockSpec((None, t_len, 2 * GROUP_W), lambda b, t: (b, 0, 0))],
        out_specs=_row_spec(SB_Q, GROUP_W),
        out_shape=jax.ShapeDtypeStruct((bsz, t_len, GROUP_W), BF),
        scratch_shapes=[pltpu.VMEM((N_HEADS, SB_Q, GROUP_W), F32),
                        pltpu.VMEM((N_HEADS, SB_Q, LANES), F32)],
        compiler_params=_params(("parallel", "arbitrary")),
        name="sb_prompt",
    )(qb, kvb_b)


def _pool_mix(ext_ref, tm, tpos, pool_w_ref, pool_s_ref):
    def shifted(ref, k):
        return ref[pl.ds(POOL_MAX - k, tm), :]
    x = shifted(ext_ref, 0)
    lane = lax.broadcasted_iota(jnp.int32, (1, GROUP_W), 1)
    grp = lane // HEAD_DIM
    s2 = x + shifted(ext_ref, 1)
    s4 = s2 + shifted(ext_ref, 2) + shifted(ext_ref, 3)
    s8 = s4 + sum(shifted(ext_ref, k) for k in range(4, 8))
    s16 = s8 + sum(shifted(ext_ref, k) for k in range(8, 16))
    tot = jnp.where(grp == 0, s2, jnp.where(grp == 1, s4, jnp.where(grp == 2, s8, s16)))
    wlen = jnp.where(grp == 0, 2, jnp.where(grp == 1, 4, jnp.where(grp == 2, 8, 16)))
    cnt = jnp.minimum(wlen, tpos + 1).astype(F32)
    d = tot / cnt - x
    return _dot(d.astype(BF), pool_w_ref[...]) * pool_s_ref[...]


def _gmlp_pool_kernel(u_ref, v_ref, pin_ref, halo_ref, sgw_ref, sgb_ref, pw_ref, ps_ref,
                      oc_ref, od_ref, ext_sc):
    t = pl.program_id(1)
    tm = u_ref.shape[0]
    lane = lax.broadcasted_iota(jnp.int32, (1, GROUP_W), 1)
    for c in range(tm // CHUNK):
        rows = slice(c * CHUNK, (c + 1) * CHUNK)
        v = v_ref[rows, :]
        vz = jnp.zeros_like(v)
        vst = jnp.concatenate([jnp.where(lane // HEAD_DIM == h, v, vz) for h in range(N_HEADS)], axis=0)
        s = _dot(sgw_ref[...], vst.astype(BF)) + sgb_ref[...]
        oc_ref[rows, :] = (u_ref[rows, :] * s).astype(BF)
    halo = halo_ref[...]
    ext_sc[0:POOL_MAX, :] = jnp.where(t > 0, halo, jnp.zeros_like(halo))
    ext_sc[POOL_MAX:POOL_MAX + tm, :] = pin_ref[...]
    tpos = t * tm + lax.broadcasted_iota(jnp.int32, (tm, 1), 0)
    od_ref[...] = _pool_mix(ext_sc, tm, tpos, pw_ref, ps_ref).astype(BF)


def gmlp_pool_prompt(u, v, pin, sgw_cat, sgb_full, pool_w_bd, pool_scale, tm):
    bsz, t_len, _ = u.shape
    per = tm // POOL_MAX
    halo_spec = pl.BlockSpec((None, POOL_MAX, GROUP_W), lambda b, t: (b, jnp.maximum(t * per - 1, 0), 0))
    return pl.pallas_call(
        _gmlp_pool_kernel,
        grid=(bsz, t_len // tm),
        in_specs=[_row_spec(tm, GROUP_W), _row_spec(tm, GROUP_W), _row_spec(tm, GROUP_W), halo_spec,
                  _const_spec((CHUNK, N_HEADS * CHUNK)), _const_spec((CHUNK, GROUP_W)),
                  _const_spec((GROUP_W, GROUP_W)), _const_spec((1, GROUP_W))],
        out_specs=[_row_spec(tm, GROUP_W), _row_spec(tm, GROUP_W)],
        out_shape=[jax.ShapeDtypeStruct((bsz, t_len, GROUP_W), BF)] * 2,
        scratch_shapes=[pltpu.VMEM((POOL_MAX + tm, GROUP_W), F32)],
        compiler_params=_params(("parallel", "parallel")),
        name="gmlp_pool",
    )(u, v, pin, pin, sgw_cat, sgb_full, pool_w_bd, pool_scale)


def _swap_neg(w):
    half = HEAD_DIM // 2
    w = w.reshape(w.shape[0], -1, 2, half)
    return jnp.stack([-w[:, :, 1], w[:, :, 0]], axis=2).reshape(w.shape[0], -1)


def _pad_heads(w):
    w = w.reshape(w.shape[0], -1, HEAD_DIM)
    return jnp.concatenate([w, jnp.zeros_like(w)], axis=2).reshape(w.shape[0], -1)


def _proj_weights(w_in):
    a_q, a_kv, a_g, b_qkv, c_uv, d_in = jnp.split(
        w_in, np.cumsum([GROUP_W, 6 * HEAD_DIM, 3 * N_HEADS, 3 * GROUP_W, 2 * GROUP_W]).tolist(), axis=1)
    scale = HEAD_DIM ** -0.5
    a_q = a_q * scale
    kv = a_kv.reshape(-1, 3, 2, HEAD_DIM)
    kv_sw = jnp.concatenate([_swap_neg(kv[:, :, 0].reshape(-1, 3 * HEAD_DIM)).reshape(-1, 3, 1, HEAD_DIM),
                             jnp.zeros_like(kv[:, :, 1:2])], axis=2).reshape(-1, 6 * HEAD_DIM)
    g_pad = jnp.pad(a_g, ((0, 0), (0, LANES - 3 * N_HEADS)))
    b_q = b_qkv[:, :GROUP_W] * scale
    w_all = jnp.concatenate([_pad_heads(a_q), _pad_heads(_swap_neg(a_q)), a_kv, kv_sw, g_pad,
                             b_q, b_qkv[:, GROUP_W:], c_uv, d_in], axis=1)
    w_t = jnp.concatenate([a_kv, kv_sw, b_qkv[:, GROUP_W:]], axis=1).T
    return w_all.astype(BF), w_t.astype(BF)


def _rope_tables(pos):
    half = HEAD_DIM // 2
    inv = ROPE_THETA ** (-jnp.arange(half, dtype=F32) / half)
    ang = pos.astype(F32)[:, None] * inv[None, :]
    cos, sin = jnp.cos(ang), jnp.sin(ang)
    cos_t = jnp.concatenate([cos.T, cos.T, jnp.ones((HEAD_DIM, pos.shape[0]), F32)], axis=0)
    sin_t = jnp.concatenate([sin.T, sin.T, jnp.zeros((HEAD_DIM, pos.shape[0]), F32)], axis=0)
    return jnp.tile(cos, (1, 4)), jnp.tile(sin, (1, 4)), cos_t, sin_t


def _cmp_weights(cmp_w):
    w = jnp.zeros((CMP_LEN, 2, HEAD_DIM, 2, HEAD_DIM), F32)
    w = w.at[:, 0, :, 0, :].set(cmp_w[0]).at[:, 1, :, 1, :].set(cmp_w[1])
    return w.reshape(2, CMP_STRIDE * LANES, LANES).astype(BF)


def _sel_constants(n_cmp, n_sel, t_len):
    n = np.arange(n_cmp)[None, :]
    j = np.arange(n_sel)[:, None]
    mt = ((n >= 4 * j - 1) & (n <= 4 * j + 3)).astype(np.float32) + ((n >= 4 * j) & (n <= 4 * j + 2))
    e = (np.arange(t_len)[None, :] // SEL_BLOCK == j).astype(np.float32)
    return jnp.asarray(mt, BF), jnp.asarray(e, BF)


def _gmlp_weights(sg_w, sg_b):
    wm = sg_w * jnp.tril(jnp.ones((CHUNK, CHUNK), sg_w.dtype))
    w_cat = jnp.transpose(wm, (1, 0, 2)).reshape(CHUNK, N_HEADS * CHUNK).astype(BF)
    b_full = jnp.repeat(sg_b.T, HEAD_DIM, axis=1)
    return w_cat, b_full


def _pool_weights(pool_w):
    w = jnp.zeros((N_HEADS, HEAD_DIM, N_HEADS, HEAD_DIM), F32)
    for g in range(N_HEADS):
        w = w.at[g, :, g, :].set(pool_w[g])
    return w.reshape(GROUP_W, GROUP_W).astype(BF)


def _sample_sel_weights(n_cmp_pad, n_sel_pad, n_cmp, n_sel):
    n = np.arange(n_cmp_pad)[:, None]
    j = np.arange(n_sel_pad)[None, :]
    ms = ((n >= 4 * j - 1) & (n <= 4 * j + 3)).astype(np.float32) + ((n >= 4 * j) & (n <= 4 * j + 2))
    ms = ms * ((n < n_cmp) & (j < n_sel))
    return jnp.asarray(ms, BF)


def _layer_weights(l, w_ada, b_ada, norm_g, w_ffn_up, w_ffn_down, w_in, w_out,
                   cmp_pe, cmp_w, sg_ln_g, sg_ln_b, sg_w, sg_b, pool_w, pool_scale):
    sgw_cat, sgb_full = _gmlp_weights(sg_w[l], sg_b[l])
    w_proj, w_proj_t = _proj_weights(w_in[l])
    return dict(
        w_ada=w_ada[l].astype(BF), b_ada=b_ada[l][None, :], norm_g=norm_g[l][:, None, :],
        w_up=w_ffn_up[l].astype(BF), w_down=w_ffn_down[l].astype(BF),
        w_in=w_proj, w_in_t=w_proj_t, w_out=w_out[l].astype(BF),
        cmp_pe=cmp_pe[l].reshape(CMP_LEN, LANES), cmp_w=_cmp_weights(cmp_w[l]),
        ln_g=sg_ln_g[l][None, :], ln_b=sg_ln_b[l][None, :], sgw_cat=sgw_cat, sgb_full=sgb_full,
        sg_w00=jnp.repeat(sg_w[l][:, 0, 0], HEAD_DIM)[None, :], sg_b0=jnp.repeat(sg_b[l][:, 0], HEAD_DIM)[None, :],
        pool_w=_pool_weights(pool_w[l]), pool_scale=pool_scale[l][None, :])


def _mods(mod, shape):
    m = mod.reshape(mod.shape[0], N_MOD, D_MODEL)
    return [m[:, k].reshape(shape) for k in range(N_MOD)]


def prompt_layer(x, mod, lw, tables, consts, tm):
    bsz, t_len, _ = x.shape
    m = _mods(mod, (bsz, 1, D_MODEL))
    g = lw["norm_g"]
    cos, sin, cos_t, sin_t = tables
    x = ffn_half(x, m[0], m[1], m[2], g[0], g[1], lw["w_up"][0], lw["w_down"][0], tm)
    pr = in_proj(x, m[3], m[4], g[2], cos, sin, lw["w_in"], lw["ln_g"], lw["ln_b"], tm,
                 t_side=(lw["w_in_t"], cos_t, sin_t))
    kcmp = compress_prompt(pr["kv_cmp"], lw["cmp_pe"], lw["cmp_w"])
    o_a = nsa_prompt(pr["qa"], pr["gate"], kcmp, pr["kv_sel_b"], pr["kv_win_b"], *consts)
    o_b = sb_prompt(pr["qb"], pr["kvb_b"])
    o_c, o_d = gmlp_pool_prompt(pr["u"], pr["v"], pr["pin"], lw["sgw_cat"], lw["sgb_full"],
                                lw["pool_w"], lw["pool_scale"], tm)
    x = out_proj(x, o_a, o_b, o_c, o_d, m[5], g[3], lw["w_out"], tm)
    x = ffn_half(x, m[6], m[7], m[8], g[4], g[5], lw["w_up"][1], lw["w_down"][1], tm)
    n_win = min(WINDOW, t_len)
    state = (pr["kvt_cmp"], pr["kvt_sel"], pr["kvt_win"][:, :, t_len - n_win:], pr["kvbt"],
             pr["pin"][:, t_len - POOL_BUF:])
    return x, state


def _page_copies(cache_ref, layer, pt_ref, b, first_page, n_pages, buf_ref, slot, sem_ref):
    return [pltpu.make_async_copy(cache_ref.at[layer, pt_ref[b, first_page + p]], buf_ref.at[slot, p],
                                  sem_ref.at[slot]) for p in range(n_pages)]


SB_GROUP = 16


def _sb_sample_kernel(pt_ref, q_ref, cache_ref, o_ref, buf, sem, *, layer, n_pages):
    b = pl.program_id(0)
    n_seq = pl.num_programs(0)
    n_grp = n_pages // SB_GROUP

    def copies(seq, grp, slot):
        return _page_copies(cache_ref, layer, pt_ref, seq, (n_grp - 1 - grp) * SB_GROUP, SB_GROUP, buf, slot, sem)

    @pl.when(b == 0)
    def _():
        for c in copies(0, 0, 0):
            c.start()

    row = lax.broadcasted_iota(jnp.int32, (8, GROUP_W), 0)
    lane = lax.broadcasted_iota(jnp.int32, (8, GROUP_W), 1)
    head_lanes = lane // HEAD_DIM == row
    q = jnp.broadcast_to(q_ref[...].astype(F32), (8, GROUP_W))
    qm = jnp.where(head_lanes, q, 0.0).astype(BF)
    tri = _tri_ones(PAGE)

    def group(g, carry):
        acc, car = carry
        step = b * n_grp + g
        slot = step % 2
        last = g + 1 == n_grp

        @pl.when(step + 1 < n_seq * n_grp)
        def _():
            for c in copies(jnp.where(last, b + 1, b), jnp.where(last, 0, g + 1), 1 - slot):
                c.start()

        for c in copies(b, g, slot):
            c.wait()
        for p in reversed(range(SB_GROUP)):
            kt = buf[slot, p, 0:GROUP_W, :].astype(BF)
            vt = buf[slot, p, GROUP_W:2 * GROUP_W, :].astype(BF)
            z = _dot(qm, kt)
            sp = _softplus(z)
            lk = -sp
            hi = lk.astype(BF)
            lo = (lk - hi.astype(F32)).astype(BF)
            cs = _dot(hi, tri) + _dot(lo, tri)
            a = jnp.exp(z - sp + cs[:, :PAGE] + car)
            acc = acc + _dot_nt(a.astype(BF), vt)
            car = car + cs[:, PAGE:]
        return acc, car

    acc, _ = lax.fori_loop(0, n_grp, group, (jnp.zeros((8, GROUP_W), F32), jnp.zeros((8, LANES), F32)))
    o_ref[...] = jnp.sum(jnp.where(head_lanes, acc, 0.0), axis=0, keepdims=True)


def sb_sample(page_table, qb, cache_t, layer):
    n_seq = qb.shape[0]
    n_pages = page_table.shape[1]
    return pl.pallas_call(
        functools.partial(_sb_sample_kernel, layer=layer, n_pages=n_pages),
        grid_spec=pltpu.PrefetchScalarGridSpec(
            num_scalar_prefetch=1, grid=(n_seq,),
            in_specs=[pl.BlockSpec((None, 1, GROUP_W), lambda b, pt: (b, 0, 0)),
                      pl.BlockSpec(memory_space=pl.ANY)],
            out_specs=pl.BlockSpec((None, 1, GROUP_W), lambda b, pt: (b, 0, 0)),
            scratch_shapes=[pltpu.VMEM((2, SB_GROUP, 2 * GROUP_W, PAGE), F32), pltpu.SemaphoreType.DMA((2,))]),
        out_shape=jax.ShapeDtypeStruct((n_seq, 1, GROUP_W), F32),
        compiler_params=_params(("arbitrary",)),
        name="sb_sample",
    )(page_table, qb, cache_t)


CMP_CHUNK = 64
TAIL_ROWS = 128


def _nsa_cmp_sample_kernel(pt_ref, q_ref, new_ref, pe_ref, w_ref, ms_ref, cache_ref, ocmp_ref, sel_ref,
                           buf, sem, rows_sc, lo_sc, hi_sc, *, layer, n_pages):
    b = pl.program_id(0)
    slot = b % 2

    def copies(seq, sl):
        return _page_copies(cache_ref, layer, pt_ref, seq, 0, n_pages, buf, sl, sem)

    @pl.when(b == 0)
    def _():
        for c in copies(0, 0):
            c.start()

    @pl.when(b + 1 < pl.num_programs(0))
    def _():
        for c in copies(b + 1, 1 - slot):
            c.start()

    for c in copies(b, slot):
        c.wait()

    past = n_pages * PAGE
    n_pad = lo_sc.shape[0]

    def to_rows(p, carry):
        rows_sc[pl.ds(pl.multiple_of(p * PAGE, PAGE), PAGE), :] = buf[slot, p].T
        return carry

    lax.fori_loop(0, n_pages, to_rows, 0)
    r_io = lax.broadcasted_iota(jnp.int32, (TAIL_ROWS, LANES), 0)
    rows_sc[past:past + TAIL_ROWS, :] = jnp.where(r_io == 0, jnp.broadcast_to(new_ref[...], (TAIL_ROWS, LANES)), 0.0)

    def cmp_chunk(c, carry):
        g0 = pl.multiple_of(c * CMP_CHUNK, CMP_CHUNK)
        lo, hi = _compress_rows(rows_sc, CMP_CHUNK, pe_ref, w_ref, base=g0 * CMP_STRIDE)
        lo_sc[pl.ds(g0, CMP_CHUNK), :] = lo
        hi_sc[pl.ds(g0, CMP_CHUNK), :] = hi
        return carry

    n_full = past // (CMP_STRIDE * CMP_CHUNK)
    lax.fori_loop(0, n_full, cmp_chunk, 0)
    n_tail = TAIL_ROWS // CMP_STRIDE
    g_tail = past // CMP_STRIDE
    lo, hi = _compress_rows(rows_sc, n_tail, pe_ref, w_ref, base=past)
    lo_sc[g_tail:g_tail + n_tail, :] = lo
    hi_sc[g_tail:g_tail + n_tail, :] = hi
    lo_sc[g_tail + n_tail:n_pad, :] = jnp.zeros((n_pad - g_tail - n_tail, LANES), F32)
    hi_sc[g_tail + n_tail:n_pad + 8, :] = jnp.zeros((n_pad + 8 - g_tail - n_tail, LANES), F32)
    kc = (lo_sc[...] + hi_sc[pl.ds(1, n_pad), :]).astype(BF)

    q8 = q_ref[...]
    qpos = past
    s = _dot_nt(q8, kc)
    blk_end = lax.broadcasted_iota(jnp.int32, (1, n_pad), 1) * CMP_STRIDE + (CMP_LEN - 1)
    p = _softmax_rows(s, blk_end <= qpos)
    ocmp_ref[...] = _dot(p.astype(BF), kc)
    ps = jnp.broadcast_to(jnp.sum(p[0:N_HEADS], axis=0, keepdims=True), (8, n_pad))
    ms = ms_ref[...]
    p_slc = sum(_dot(part, ms) for part in _split3(ps))
    j_io = lax.broadcasted_iota(jnp.int32, p_slc.shape, 1)
    cur = qpos // SEL_BLOCK
    forced = (j_io == 0) | (j_io == cur) | (j_io == cur - 1)
    valid = j_io <= cur
    score = jnp.where(valid, jnp.where(forced, FORCE_SCORE, p_slc), NEG)
    sel_ref[...] = _topk_select(score, valid, N_TOPK, axis=1)


def nsa_cmp_sample(page_table, q8, new_cmp, pe, w_cmp, ms, cache_t, layer):
    n_seq = q8.shape[0]
    n_pages = page_table.shape[1]
    n_pad, n_sel_pad = ms.shape
    seq_spec = lambda w: pl.BlockSpec((None, 8, w), lambda b, pt: (b, 0, 0))
    const2 = lambda shape: pl.BlockSpec(shape, lambda b, pt: (0,) * len(shape))
    return pl.pallas_call(
        functools.partial(_nsa_cmp_sample_kernel, layer=layer, n_pages=n_pages),
        grid_spec=pltpu.PrefetchScalarGridSpec(
            num_scalar_prefetch=1, grid=(n_seq,),
            in_specs=[seq_spec(LANES), pl.BlockSpec((None, 1, LANES), lambda b, pt: (b, 0, 0)),
                      const2((CMP_LEN, LANES)), const2((2, CMP_STRIDE * LANES, LANES)), const2((n_pad, n_sel_pad)),
                      pl.BlockSpec(memory_space=pl.ANY)],
            out_specs=[seq_spec(LANES), seq_spec(n_sel_pad)],
            scratch_shapes=[pltpu.VMEM((2, n_pages, LANES, PAGE), F32), pltpu.SemaphoreType.DMA((2,)),
                            pltpu.VMEM((n_pages * PAGE + TAIL_ROWS, LANES), F32),
                            pltpu.VMEM((n_pad, LANES), F32), pltpu.VMEM((n_pad + 8, LANES), F32)]),
        out_shape=[jax.ShapeDtypeStruct((n_seq, 8, LANES), F32), jax.ShapeDtypeStruct((n_seq, 8, n_sel_pad), F32)],
        compiler_params=_params(("arbitrary",)),
        name="nsa_cmp_sample",
    )(page_table, q8, new_cmp, pe, w_cmp, ms, cache_t)


def _nsa_sel_sample_kernel(pt_ref, q_ref, sel_ref, ocmp_ref, gate_ref, news_ref, neww_ref, win_ref, cache_ref,
                           o_ref, buf, sem, mask_sc, s_sc, *, layer, n_pages):
    b = pl.program_id(0)
    slot = b % 2

    def copies(seq, sl):
        return _page_copies(cache_ref, layer, pt_ref, seq, 0, n_pages, buf, sl, sem)

    @pl.when(b == 0)
    def _():
        for c in copies(0, 0):
            c.start()

    @pl.when(b + 1 < pl.num_programs(0))
    def _():
        for c in copies(b + 1, 1 - slot):
            c.start()

    for c in copies(b, slot):
        c.wait()

    past = n_pages * PAGE
    qpos = past
    q8 = q_ref[...]
    q8f = q8.astype(F32)
    sel = sel_ref[...]
    n_sel_pad = sel.shape[1]
    pg = lax.broadcasted_iota(jnp.int32, (n_pages, n_sel_pad), 0)
    jj = lax.broadcasted_iota(jnp.int32, (n_pages, n_sel_pad), 1)
    per_page = jnp.where(jj // 2 == pg, jnp.broadcast_to(sel[0:1, :], (n_pages, n_sel_pad)), 0.0).astype(BF)
    j2 = lax.broadcasted_iota(jnp.int32, (n_sel_pad, PAGE), 0)
    t2 = lax.broadcasted_iota(jnp.int32, (n_sel_pad, PAGE), 1)
    half = jnp.where(t2 // SEL_BLOCK == j2 % 2, 1.0, 0.0).astype(BF)
    mask_sc[...] = _dot(per_page, half)
    lane = lax.broadcasted_iota(jnp.int32, (1, PAGE), 1)

    def new_token(new_ref, allowed):
        kn = new_ref[...].astype(BF).astype(F32)
        s_new = jnp.sum(q8f * kn, axis=-1, keepdims=True)
        return kn, jnp.where(allowed, s_new, NEG)

    def score(p, m_run):
        allow = (mask_sc[pl.ds(p, 1), :] > 0.5) & (p * PAGE + lane <= qpos)
        sm = jnp.where(allow, _dot(q8, buf[slot, p].astype(BF)), NEG)
        s_sc[p] = sm
        return jnp.maximum(m_run, sm)

    kn_s, s_new = new_token(news_ref, (sel[:, qpos // SEL_BLOCK:qpos // SEL_BLOCK + 1] > 0.5))
    m_run = lax.fori_loop(0, n_pages, score, jnp.full((8, PAGE), NEG, F32))
    m = jnp.maximum(jnp.max(m_run, axis=-1, keepdims=True), s_new)

    def weigh(p, carry):
        l_run, acc = carry
        sm = s_sc[p]
        e = jnp.where(sm > 0.5 * NEG, jnp.exp(sm - m), 0.0)
        return l_run + e, acc + _dot_nt(e.astype(BF), buf[slot, p].astype(BF))

    l_run, acc = lax.fori_loop(0, n_pages, weigh, (jnp.zeros((8, PAGE), F32), jnp.zeros((8, LANES), F32)))
    e_new = jnp.where(s_new > 0.5 * NEG, jnp.exp(s_new - m), 0.0)
    l_sel = jnp.sum(l_run, axis=-1, keepdims=True) + e_new
    o_sel = (acc + e_new * kn_s) / jnp.maximum(l_sel, 1e-30)

    n_buf = win_ref.shape[1]
    wb = win_ref[...].astype(BF)
    kwpos = past - n_buf + lax.broadcasted_iota(jnp.int32, (1, n_buf), 1)
    dist = qpos - kwpos
    allow_w = (dist >= 0) & (dist <= WINDOW) & (kwpos >= 0)
    sw = jnp.where(allow_w, _dot(q8, wb), NEG)
    kn_w, sw_new = new_token(neww_ref, True)
    mw = jnp.maximum(jnp.max(sw, axis=-1, keepdims=True), sw_new)
    ew = jnp.where(allow_w, jnp.exp(sw - mw), 0.0)
    ew_new = jnp.exp(sw_new - mw)
    l_w = jnp.sum(ew, axis=-1, keepdims=True) + ew_new
    o_win = (_dot_nt(ew.astype(BF), wb) + ew_new * kn_w) / jnp.maximum(l_w, 1e-30)

    g = jnp.broadcast_to(gate_ref[...], (8, LANES))
    g_row = lax.broadcasted_iota(jnp.int32, (8, LANES), 0)
    g_lane = lax.broadcasted_iota(jnp.int32, (8, LANES), 1)
    gk = [jnp.sum(jnp.where(g_lane == 3 * g_row + k, g, 0.0), axis=-1, keepdims=True) for k in range(3)]
    o_ref[...] = gk[0] * ocmp_ref[...] + gk[1] * o_sel + gk[2] * o_win


def nsa_sel_sample(page_table, q8, sel, o_cmp, gate, new_sel, new_win, win_t, cache_t, layer):
    n_seq = q8.shape[0]
    n_pages = page_table.shape[1]
    n_buf = win_t.shape[3]
    seq_spec = lambda r, w: pl.BlockSpec((None, r, w), lambda b, pt: (b, 0, 0))
    return pl.pallas_call(
        functools.partial(_nsa_sel_sample_kernel, layer=layer, n_pages=n_pages),
        grid_spec=pltpu.PrefetchScalarGridSpec(
            num_scalar_prefetch=1, grid=(n_seq,),
            in_specs=[seq_spec(8, LANES), seq_spec(8, sel.shape[2]), seq_spec(8, LANES), seq_spec(1, LANES),
                      seq_spec(1, LANES), seq_spec(1, LANES),
                      pl.BlockSpec((None, None, LANES, n_buf), lambda b, pt: (layer, b, 0, 0)),
                      pl.BlockSpec(memory_space=pl.ANY)],
            out_specs=seq_spec(8, LANES),
            scratch_shapes=[pltpu.VMEM((2, n_pages, LANES, PAGE), F32), pltpu.SemaphoreType.DMA((2,)),
                            pltpu.VMEM((n_pages, PAGE), F32), pltpu.VMEM((n_pages, 8, PAGE), F32)]),
        out_shape=jax.ShapeDtypeStruct((n_seq, 8, LANES), F32),
        compiler_params=_params(("arbitrary",)),
        name="nsa_sel_sample",
    )(page_table, q8, sel, o_cmp, gate, new_sel, new_win, win_t, cache_t)


def _gmlp_pool_sample_kernel(u_ref, v_ref, pin_ref, hist_ref, w00_ref, b0_ref, pw_ref, ps_ref, oc_ref, od_ref,
                             *, past_len):
    oc_ref[...] = u_ref[...] * (w00_ref[...] * v_ref[...] + b0_ref[...])
    x = pin_ref[...]
    lane = lax.broadcasted_iota(jnp.int32, (1, GROUP_W), 1)
    grp = lane // HEAD_DIM
    sums = []
    tot = x
    k = 1
    for wlen in POOL_WINDOWS:
        while k < wlen:
            tot = tot + hist_ref[POOL_BUF - k]
            k += 1
        sums.append(tot)
    tot = jnp.where(grp == 0, sums[0], jnp.where(grp == 1, sums[1], jnp.where(grp == 2, sums[2], sums[3])))
    wlen = jnp.where(grp == 0, POOL_WINDOWS[0], jnp.where(grp == 1, POOL_WINDOWS[1],
                                                          jnp.where(grp == 2, POOL_WINDOWS[2], POOL_WINDOWS[3])))
    cnt = jnp.minimum(wlen, past_len + 1).astype(F32)
    d = tot / cnt - x
    od_ref[...] = _dot(d.astype(BF), pw_ref[...]) * ps_ref[...]


def gmlp_pool_sample(u, v, pin, hist_t, layer, w00, b0, pool_w_bd, pool_scale, past_len):
    n_seq = u.shape[0]
    full = lambda shape: pl.BlockSpec(shape, lambda i: (0,) * len(shape))
    return pl.pallas_call(
        functools.partial(_gmlp_pool_sample_kernel, past_len=past_len),
        grid=(1,),
        in_specs=[full((n_seq, GROUP_W))] * 3 +
                 [pl.BlockSpec((None, POOL_BUF, n_seq, GROUP_W), lambda i: (layer, 0, 0, 0)),
                  full((1, GROUP_W)), full((1, GROUP_W)), full((GROUP_W, GROUP_W)), full((1, GROUP_W))],
        out_specs=[full((n_seq, GROUP_W))] * 2,
        out_shape=[jax.ShapeDtypeStruct((n_seq, GROUP_W), F32)] * 2,
        compiler_params=_params(("arbitrary",)),
        name="gmlp_pool_sample",
    )(u, v, pin, hist_t, w00, b0, pool_w_bd, pool_scale)


def sample_layer(x, mod, lw, tables, ms, caches_t, layer, page_table):
    n_seq = x.shape[1]
    past_len = page_table.shape[1] * PAGE
    cmp_t, sel_t, win_t, sb_t, pool_t = caches_t
    m = _mods(mod, (1, n_seq, D_MODEL))
    g = lw["norm_g"]
    x = ffn_half(x, m[0], m[1], m[2], g[0], g[1], lw["w_up"][0], lw["w_down"][0], n_seq)
    pr = in_proj(x, m[3], m[4], g[2], tables[0], tables[1], lw["w_in"], lw["ln_g"], lw["ln_b"], n_seq)
    rows = {k: v[0] for k, v in pr.items()}
    q8 = jnp.pad(rows["qa"].reshape(n_seq, N_HEADS, LANES), ((0, 0), (0, 8 - N_HEADS), (0, 0)))
    o_cmp, sel = nsa_cmp_sample(page_table, q8, rows["kv_cmp"][:, None, :], lw["cmp_pe"], lw["cmp_w"], ms,
                                cmp_t, layer)
    o8 = nsa_sel_sample(page_table, q8, sel, o_cmp, rows["gate"][:, None, :], rows["kv_sel"][:, None, :],
                        rows["kv_win"][:, None, :], win_t, sel_t, layer)
    o_a = o8[:, :N_HEADS, HEAD_DIM:].reshape(1, n_seq, GROUP_W).astype(BF)
    o_b = sb_sample(page_table, rows["qb"][:, None, :], sb_t, layer).reshape(1, n_seq, GROUP_W).astype(BF)
    o_c, o_d = gmlp_pool_sample(rows["u"], rows["v"], rows["pin"], pool_t, layer, lw["sg_w00"], lw["sg_b0"],
                                lw["pool_w"], lw["pool_scale"], past_len)
    x = out_proj(x, o_a, o_b, o_c[None].astype(BF), o_d[None].astype(BF), m[5], g[3], lw["w_out"], n_seq)
    x = ffn_half(x, m[6], m[7], m[8], g[4], g[5], lw["w_up"][1], lw["w_down"][1], n_seq)
    state = (rows["kv_cmp"], rows["kv_sel"], rows["kv_win"], rows["kvb"], rows["pin"], rows["v"])
    return x, state


PROMPT_TM = 512


def kernel(x_prompt, x_sample, cache_nsa_cmp, cache_nsa_sel, cache_nsa_win, cache_sb, state_pool, page_table,
           c_prompt, c_sample, w_ada, b_ada, norm_g, w_ffn_up, w_ffn_down, w_in, w_out,
           cmp_pe, cmp_w, sg_ln_g, sg_ln_b, sg_w, sg_b, pool_w, pool_scale):
    n_p, t_len, _ = x_prompt.shape
    n_s, t_dec, _ = x_sample.shape
    assert t_dec == 1, "the sample step advances one token per sequence"
    depth = w_ada.shape[0]
    n_phys = cache_sb.shape[1]
    past_len = page_table.shape[1] * PAGE
    c_all = jnp.concatenate([c_prompt, c_sample], axis=0)
    c_all = jnp.pad(c_all, ((0, (-c_all.shape[0]) % 8), (0, 0)))
    tables_p = _rope_tables(jnp.arange(t_len))
    tables_s = _rope_tables(jnp.full((n_s,), past_len))
    consts = _sel_constants(t_len // CMP_STRIDE, t_len // SEL_BLOCK, t_len)
    n_rows = -(-(past_len + t_dec) // SEL_BLOCK) * SEL_BLOCK
    n_cmp = n_rows // CMP_STRIDE - 1
    n_sel = (n_cmp + 1) // (SEL_BLOCK // CMP_STRIDE)
    ms = _sample_sel_weights(-(-((past_len + TAIL_ROWS) // CMP_STRIDE) // LANES) * LANES,
                             -(-n_sel // LANES) * LANES, n_cmp, n_sel)
    caches_t = (jnp.transpose(cache_nsa_cmp, (0, 1, 3, 4, 2)).reshape(depth, n_phys, LANES, PAGE),
                jnp.transpose(cache_nsa_sel, (0, 1, 3, 4, 2)).reshape(depth, n_phys, LANES, PAGE),
                jnp.transpose(cache_nsa_win, (0, 1, 3, 4, 2)).reshape(depth, n_s, LANES, -1),
                jnp.transpose(cache_sb, (0, 1, 3, 4, 5, 2)).reshape(depth, n_phys, 2 * GROUP_W, PAGE),
                jnp.transpose(state_pool, (0, 2, 1, 3)))
    y_p, y_s = x_prompt, x_sample.reshape(1, n_s, D_MODEL)
    st_p, st_s = [], []
    for l in range(depth):
        lw = _layer_weights(l, w_ada, b_ada, norm_g, w_ffn_up, w_ffn_down, w_in, w_out,
                            cmp_pe, cmp_w, sg_ln_g, sg_ln_b, sg_w, sg_b, pool_w, pool_scale)
        mod = ada_mod(c_all, lw["w_ada"], lw["b_ada"])
        y_p, s = prompt_layer(y_p, mod[:n_p], lw, tables_p, consts, PROMPT_TM)
        st_p.append(s)
        y_s, s = sample_layer(y_s, mod[n_p:n_p + n_s], lw, tables_s, ms, caches_t, l, page_table)
        st_s.append(s)

    def stack(states, i):
        return jnp.stack([s[i] for s in states], axis=0)

    def kv_rows(x_t):
        d, bsz, _, t = x_t.shape
        return jnp.transpose(x_t.reshape(d, bsz, 2, HEAD_DIM, t), (0, 1, 4, 2, 3))

    sb_p = stack(st_p, 3)
    sb_p = jnp.transpose(sb_p.reshape(depth, n_p, 2, N_HEADS, HEAD_DIM, t_len), (0, 1, 5, 2, 3, 4))
    win_new = stack(st_s, 2).reshape(depth, n_s, 1, 2, HEAD_DIM)
    pool_new = stack(st_s, 4)[:, :, None, :]
    return (y_p, y_s.reshape(n_s, 1, D_MODEL),
            kv_rows(stack(st_p, 0)), stack(st_s, 0).reshape(depth, n_s, 1, 2, HEAD_DIM),
            kv_rows(stack(st_p, 1)), stack(st_s, 1).reshape(depth, n_s, 1, 2, HEAD_DIM),
            kv_rows(stack(st_p, 2)), jnp.concatenate([cache_nsa_win[:, :, 1:], win_new], axis=2),
            sb_p, stack(st_s, 3).reshape(depth, n_s, 1, 2, N_HEADS, HEAD_DIM),
            stack(st_p, 4), jnp.concatenate([state_pool[:, :, 1:], pool_new], axis=2),
            stack(st_s, 5)[:, :, None, :])
```

```python
import functools
import math

import numpy as np
import jax
import jax.numpy as jnp
from jax import lax
from jax.experimental import pallas as pl
from jax.experimental.pallas import tpu as pltpu

F32 = jnp.float32
BF = jnp.bfloat16

D_MODEL = 1024
HEAD_DIM = 64
N_HEADS = 4
GROUP_W = N_HEADS * HEAD_DIM
D_FF = 2816
N_MOD = 9
PAGE = 128
CMP_STRIDE = 16
CMP_LEN = 32
SEL_BLOCK = 64
N_TOPK = 16
WINDOW = 512
CHUNK = 128
POOL_WINDOWS = (2, 4, 8, 16)
POOL_MAX = 16
POOL_BUF = POOL_MAX - 1
ROPE_THETA = 10000.0
EPS = 1e-6
FORCE_SCORE = 1e9
NEG = -3.0e38
MASK_BIAS = 1.0e30
LANES = 128
VMEM_LIMIT = 56 * 1024 * 1024

C_QA, C_QAS, C_KV, C_KVS, C_G, C_QB, C_KVB, C_UV, C_DIN, C_END = (
    0, 512, 1024, 1408, 1792, 1920, 2176, 2688, 3200, 3456)


def _params(sem, vmem=VMEM_LIMIT):
    return pltpu.CompilerParams(dimension_semantics=sem, vmem_limit_bytes=vmem)


def _dot(a, b):
    return jnp.dot(a, b, preferred_element_type=F32)


def _dot_nt(a, b):
    return lax.dot_general(a, b, (((1,), (1,)), ((), ())), preferred_element_type=F32)


def _rms(x, g):
    return x * lax.rsqrt(jnp.mean(x * x, axis=-1, keepdims=True) + EPS) * g


def _split3(x):
    hi = x.astype(BF)
    r = x - hi.astype(F32)
    mid = r.astype(BF)
    lo = (r - mid.astype(F32)).astype(BF)
    return hi, mid, lo


def _ada_kernel(c_ref, w_ref, b_ref, o_ref):
    c = c_ref[...]
    s = (c * jax.nn.sigmoid(c)).astype(BF)
    o_ref[...] = _dot(s, w_ref[...]) + b_ref[...]


def ada_mod(c_all, w_ada, b_ada):
    r = c_all.shape[0]
    n = w_ada.shape[1]
    tn = 2304
    return pl.pallas_call(
        _ada_kernel,
        grid=(n // tn,),
        in_specs=[pl.BlockSpec((r, D_MODEL), lambda j: (0, 0)),
                  pl.BlockSpec((D_MODEL, tn), lambda j: (0, j)),
                  pl.BlockSpec((1, tn), lambda j: (0, j))],
        out_specs=pl.BlockSpec((r, tn), lambda j: (0, j)),
        out_shape=jax.ShapeDtypeStruct((r, n), F32),
        compiler_params=_params(("parallel",)),
        name="ada_mod",
    )(c_all, w_ada, b_ada)


FF_CHUNK = 256


def _ffn_kernel(x_ref, sh_ref, sc_ref, gt_ref, g1_ref, g2_ref, wu_ref, wd_ref, o_ref):
    x = x_ref[...]
    h = _rms(x, g1_ref[...]) * (1.0 + sc_ref[...]) + sh_ref[...]
    hb = h.astype(BF)
    acc = jnp.zeros(x.shape, F32)
    for c in range(D_FF // FF_CHUNK):
        lo = c * FF_CHUNK
        gate = _dot(hb, wu_ref[:, lo:lo + FF_CHUNK])
        up = _dot(hb, wu_ref[:, D_FF + lo:D_FF + lo + FF_CHUNK])
        a = (gate * jax.nn.sigmoid(gate) * up).astype(BF)
        acc = acc + _dot(a, wd_ref[lo:lo + FF_CHUNK, :])
    o_ref[...] = x + 0.5 * gt_ref[...] * _rms(acc, g2_ref[...])


def _mod_spec(mod, tm):
    if mod.shape[1] == 1:
        return pl.BlockSpec((None, 1, D_MODEL), lambda b, t: (b, 0, 0))
    return pl.BlockSpec((None, tm, D_MODEL), lambda b, t: (b, t, 0))


def _row_spec(tm, w):
    return pl.BlockSpec((None, tm, w), lambda b, t: (b, t, 0))


def _const_spec(shape):
    nd = len(shape)
    return pl.BlockSpec(shape, lambda b, t: (0,) * nd)


def ffn_half(x, shift, scale, gate, g1, g2, w_up, w_down, tm):
    bsz, t_len, _ = x.shape
    return pl.pallas_call(
        _ffn_kernel,
        grid=(bsz, t_len // tm),
        in_specs=[_row_spec(tm, D_MODEL), _mod_spec(shift, tm), _mod_spec(scale, tm), _mod_spec(gate, tm),
                  _const_spec((1, D_MODEL)), _const_spec((1, D_MODEL)),
                  _const_spec((D_MODEL, 2 * D_FF)), _const_spec((D_FF, D_MODEL))],
        out_specs=_row_spec(tm, D_MODEL),
        out_shape=jax.ShapeDtypeStruct(x.shape, F32),
        compiler_params=_params(("parallel", "parallel")),
        name="ffn_half",
    )(x, shift, scale, gate, g1, g2, w_up, w_down)


def _gelu_tanh(x):
    return 0.5 * x * (1.0 + jnp.tanh(np.sqrt(2.0 / np.pi).astype(np.float32) * (x + 0.044715 * (x * x * x))))


PROJ_IN = ("x", "shift", "scale", "g", "cos", "sin", "w", "ln_g", "ln_b")
PROJ_IN_T = ("w_t", "cos_t", "sin_t")
PROJ_OUT = (("qa", 2 * GROUP_W, BF), ("kv_cmp", LANES, F32), ("gate", LANES, F32), ("qb", GROUP_W, BF),
            ("u", GROUP_W, F32), ("v", GROUP_W, F32), ("pin", GROUP_W, F32))
PROJ_OUT_ROWS = (("kv_sel", LANES, F32), ("kv_win", LANES, F32), ("kvb", 2 * GROUP_W, F32))
PROJ_OUT_B16 = (("kv_sel_b", LANES, BF), ("kv_win_b", LANES, BF), ("kvb_b", 2 * GROUP_W, BF))
PROJ_OUT_T = (("kvt_cmp", LANES), ("kvt_sel", LANES), ("kvt_win", LANES), ("kvbt", 2 * GROUP_W))


def _proj_kernel(*refs, feature_major):
    names = PROJ_IN + (PROJ_IN_T if feature_major else ())
    names += tuple(n for n, _, _ in PROJ_OUT + (PROJ_OUT_B16 if feature_major else PROJ_OUT_ROWS))
    names += tuple(n for n, _ in PROJ_OUT_T) if feature_major else ()
    r = dict(zip(names, refs))
    x = r["x"][...]
    h = _rms(x, r["g"][...]) * (1.0 + r["scale"][...]) + r["shift"][...]
    hb = h.astype(BF)
    cos = r["cos"][...]
    sin = r["sin"][...]
    w_ref = r["w"]

    def mm(lo, hi):
        return _dot(hb, w_ref[:, lo:hi])

    for j in range(N_HEADS):
        p = mm(C_QA + LANES * j, C_QA + LANES * (j + 1))
        ps = mm(C_QAS + LANES * j, C_QAS + LANES * (j + 1))
        r["qa"][:, LANES * j:LANES * (j + 1)] = (p * cos + ps * sin).astype(BF)
    lane = lax.broadcasted_iota(jnp.int32, cos.shape, 1)
    ckv = jnp.where(lane < HEAD_DIM, cos, 1.0)
    for j, nm in enumerate(("kv_cmp", "kv_sel", "kv_win")):
        p = mm(C_KV + LANES * j, C_KV + LANES * (j + 1))
        ps = mm(C_KVS + LANES * j, C_KVS + LANES * (j + 1))
        kv = p * ckv + ps * sin
        if nm in r:
            r[nm][...] = kv
        if nm + "_b" in r:
            r[nm + "_b"][...] = kv.astype(BF)
    r["gate"][...] = jax.nn.sigmoid(mm(C_G, C_QB))
    r["qb"][...] = mm(C_QB, C_KVB).astype(BF)
    kvb = mm(C_KVB, C_UV)
    if feature_major:
        r["kvb_b"][...] = kvb.astype(BF)
    else:
        r["kvb"][...] = kvb
    r["u"][...] = _gelu_tanh(mm(C_UV, C_UV + GROUP_W))
    v = _gelu_tanh(mm(C_UV + GROUP_W, C_DIN))
    vc = v - jnp.mean(v, axis=-1, keepdims=True)
    vn = vc * lax.rsqrt(jnp.mean(vc * vc, axis=-1, keepdims=True) + EPS)
    r["v"][...] = vn * r["ln_g"][...] + r["ln_b"][...]
    r["pin"][...] = mm(C_DIN, C_END)
    if feature_major:
        wt_ref = r["w_t"]
        cos_t = r["cos_t"][...]
        sin_t = r["sin_t"][...]
        n_kv = 3 * LANES
        for j, nm in enumerate(("kvt_cmp", "kvt_sel", "kvt_win")):
            p = _dot_nt(wt_ref[LANES * j:LANES * (j + 1), :], hb)
            ps = _dot_nt(wt_ref[n_kv + LANES * j:n_kv + LANES * (j + 1), :], hb)
            r[nm][...] = p * cos_t + ps * sin_t
        r["kvbt"][...] = _dot_nt(wt_ref[2 * n_kv:2 * n_kv + 2 * GROUP_W, :], hb)


def in_proj(x, shift, scale, g, cos, sin, w_all, ln_g, ln_b, tm, t_side=None):
    bsz, t_len, _ = x.shape
    feature_major = t_side is not None
    tab_spec = pl.BlockSpec((tm, LANES), lambda b, t: (t, 0))
    in_specs = [_row_spec(tm, D_MODEL), _mod_spec(shift, tm), _mod_spec(scale, tm),
                _const_spec((1, D_MODEL)), tab_spec, tab_spec,
                _const_spec((D_MODEL, C_END)), _const_spec((1, GROUP_W)), _const_spec((1, GROUP_W))]
    args = [x, shift, scale, g, cos, sin, w_all, ln_g, ln_b]
    outs = PROJ_OUT + (PROJ_OUT_B16 if feature_major else PROJ_OUT_ROWS)
    out_specs = [_row_spec(tm, w) for _, w, _ in outs]
    out_shape = [jax.ShapeDtypeStruct((bsz, t_len, w), dt) for _, w, dt in outs]
    names = [n for n, _, _ in outs]
    if feature_major:
        tab_t_spec = pl.BlockSpec((LANES, tm), lambda b, t: (0, t))
        in_specs += [_const_spec(t_side[0].shape), tab_t_spec, tab_t_spec]
        args += list(t_side)
        out_specs += [pl.BlockSpec((None, w, tm), lambda b, t: (b, 0, t)) for _, w in PROJ_OUT_T]
        out_shape += [jax.ShapeDtypeStruct((bsz, w, t_len), F32) for _, w in PROJ_OUT_T]
        names += [n for n, _ in PROJ_OUT_T]
    res = pl.pallas_call(
        functools.partial(_proj_kernel, feature_major=feature_major),
        grid=(bsz, t_len // tm),
        in_specs=in_specs, out_specs=out_specs, out_shape=out_shape,
        compiler_params=_params(("parallel", "parallel")),
        name="in_proj",
    )(*args)
    return dict(zip(names, res))


def _outproj_kernel(x_ref, oa_ref, ob_ref, oc_ref, od_ref, gt_ref, g_ref, w_ref, o_ref):
    y = _dot(oa_ref[...], w_ref[0:GROUP_W, :])
    y = y + _dot(ob_ref[...], w_ref[GROUP_W:2 * GROUP_W, :])
    y = y + _dot(oc_ref[...], w_ref[2 * GROUP_W:3 * GROUP_W, :])
    y = y + _dot(od_ref[...], w_ref[3 * GROUP_W:4 * GROUP_W, :])
    o_ref[...] = x_ref[...] + gt_ref[...] * _rms(y, g_ref[...])


def out_proj(x, o_a, o_b, o_c, o_d, gate, g, w_out, tm):
    bsz, t_len, _ = x.shape
    return pl.pallas_call(
        _outproj_kernel,
        grid=(bsz, t_len // tm),
        in_specs=[_row_spec(tm, D_MODEL)] + [_row_spec(tm, GROUP_W)] * 4 +
                 [_mod_spec(gate, tm), _const_spec((1, D_MODEL)), _const_spec((D_MODEL, D_MODEL))],
        out_specs=_row_spec(tm, D_MODEL),
        out_shape=jax.ShapeDtypeStruct(x.shape, F32),
        compiler_params=_params(("parallel", "parallel")),
        name="out_proj",
    )(x, o_a, o_b, o_c, o_d, gate, g, w_out)


def _compress_rows(row_ref, n_grp, pe_ref, w_ref, base=0):
    lo_parts, hi_parts = [], []
    for l in range(CMP_STRIDE):
        a = row_ref[pl.ds(base + l, n_grp, stride=CMP_STRIDE), :]
        lo_parts.append((a + pe_ref[l:l + 1, :]).astype(BF))
        hi_parts.append((a + pe_ref[CMP_STRIDE + l:CMP_STRIDE + l + 1, :]).astype(BF))
    lo = _dot(jnp.concatenate(lo_parts, axis=1), w_ref[0])
    hi = _dot(jnp.concatenate(hi_parts, axis=1), w_ref[1])
    return lo, hi


def _compress_kernel(row_ref, pe_ref, w_ref, o_ref, hi_sc):
    n_grp = o_ref.shape[0]
    lo, hi = _compress_rows(row_ref, n_grp, pe_ref, w_ref)
    hi_sc[0:n_grp, :] = hi
    hi_sc[n_grp:n_grp + 8, :] = jnp.zeros((8, LANES), F32)
    o_ref[...] = (lo + hi_sc[pl.ds(1, n_grp), :]).astype(BF)


def compress_prompt(kv_cmp, pe, w_cmp):
    bsz, t_len, _ = kv_cmp.shape
    n_grp = t_len // CMP_STRIDE
    return pl.pallas_call(
        _compress_kernel,
        grid=(bsz,),
        in_specs=[pl.BlockSpec((None, t_len, LANES), lambda b: (b, 0, 0)),
                  pl.BlockSpec((CMP_LEN, LANES), lambda b: (0, 0)),
                  pl.BlockSpec((2, CMP_STRIDE * LANES, LANES), lambda b: (0, 0, 0))],
        out_specs=pl.BlockSpec((None, n_grp, LANES), lambda b: (b, 0, 0)),
        out_shape=jax.ShapeDtypeStruct((bsz, n_grp, LANES), BF),
        scratch_shapes=[pltpu.VMEM((n_grp + 8, LANES), F32)],
        compiler_params=_params(("parallel",)),
        name="nsa_compress",
    )(kv_cmp, pe, w_cmp)


NSA_Q = 128
NSA_KC = 512


def _softmax_rows(s, allow):
    sm = jnp.where(allow, s, NEG)
    m = jnp.max(sm, axis=-1, keepdims=True)
    e = jnp.where(allow, jnp.exp(sm - m), 0.0)
    return e / jnp.maximum(jnp.sum(e, axis=-1, keepdims=True), 1e-30)


def _topk_select(score, valid, n_pick, axis=0):
    n_blk = score.shape[axis]
    j_io = lax.broadcasted_iota(jnp.int32, score.shape, axis)
    sel = jnp.zeros(score.shape, F32)
    sc = score
    for _ in range(n_pick):
        m = jnp.max(sc, axis=axis, keepdims=True)
        idx = jnp.min(jnp.where(sc == m, j_io, n_blk), axis=axis, keepdims=True)
        pick = j_io == idx
        sel = jnp.where(pick, 1.0, sel)
        sc = jnp.where(pick, NEG, sc)
    return jnp.where(valid, sel, 0.0)


def _nsa_prompt_kernel(qa_ref, g_ref, kcmp_ref, ksel_ref, kwin_ref, mt_ref, e_ref, o_ref,
                       m_sc, acc_sc):
    i = pl.program_id(1)
    s0 = i * NSA_Q
    n_cmp = kcmp_ref.shape[0]
    n_sel = mt_ref.shape[0]
    t_len = ksel_ref.shape[0]
    qs = jnp.concatenate([qa_ref[:, LANES * h:LANES * (h + 1)] for h in range(N_HEADS)], axis=0)
    qpos = s0 + lax.broadcasted_iota(jnp.int32, (NSA_Q, 1), 0)
    qpos4 = jnp.concatenate([qpos] * N_HEADS, axis=0)

    kc = kcmp_ref[...]
    s = _dot_nt(qs, kc)
    blk_end = lax.broadcasted_iota(jnp.int32, (1, n_cmp), 1) * CMP_STRIDE + (CMP_LEN - 1)
    p = _softmax_rows(s, blk_end <= qpos4)
    o_cmp = _dot(p.astype(BF), kc)
    ps = p[0:NSA_Q] + p[NSA_Q:2 * NSA_Q] + p[2 * NSA_Q:3 * NSA_Q] + p[3 * NSA_Q:4 * NSA_Q]
    mt = mt_ref[...]
    p_slc_t = sum(_dot_nt(mt, part) for part in _split3(ps))
    j_io = lax.broadcasted_iota(jnp.int32, (n_sel, NSA_Q), 0)
    cur = (s0 + lax.broadcasted_iota(jnp.int32, (1, NSA_Q), 1)) // SEL_BLOCK
    forced = (j_io == 0) | (j_io == cur) | (j_io == cur - 1)
    valid = j_io <= cur
    score_t = jnp.where(valid, jnp.where(forced, FORCE_SCORE, p_slc_t), NEG)
    sel = _topk_select(score_t, valid, min(N_TOPK, n_sel)).T.astype(BF)

    m_sc[...] = jnp.full(m_sc.shape, NEG, F32)
    acc_sc[...] = jnp.zeros(acc_sc.shape, F32)
    rep = NSA_KC // LANES
    key_lane = lax.broadcasted_iota(jnp.int32, (1, LANES), 1) < HEAD_DIM

    def sel_chunk(c, carry):
        k0 = pl.multiple_of(c * NSA_KC, NSA_KC)
        kv = ksel_ref[pl.ds(k0, NSA_KC), :]
        kv1 = jnp.where(key_lane, jnp.ones_like(kv), kv)
        sc = _dot_nt(qs, kv)
        selx = _dot(sel, e_ref[:, pl.ds(k0, NSA_KC)])
        tok = k0 + lax.broadcasted_iota(jnp.int32, (1, NSA_KC), 1)
        bias = jnp.where((selx > 0.5) & (tok <= qpos), 0.0, -MASK_BIAS)
        for h in range(N_HEADS):
            sm = sc[h * NSA_Q:(h + 1) * NSA_Q] + bias
            m_old = m_sc[h]
            m_new = jnp.maximum(m_old, jnp.max(sm, axis=-1, keepdims=True))
            alpha = jnp.exp(m_old - m_new)
            pe = jnp.exp(sm - jnp.concatenate([m_new] * rep, axis=1))
            acc_sc[h] = alpha * acc_sc[h] + _dot(pe.astype(BF), kv1)
            m_sc[h] = m_new
        return carry

    lax.fori_loop(0, (s0 + NSA_Q + NSA_KC - 1) // NSA_KC, sel_chunk, 0)

    n_win = WINDOW + NSA_Q
    w0 = pl.multiple_of(jnp.maximum(s0 - WINDOW, 0), NSA_Q)
    kvw = kwin_ref[pl.ds(w0, n_win), :]
    sw = _dot_nt(qs, kvw)
    dist = qpos4 - (w0 + lax.broadcasted_iota(jnp.int32, (1, n_win), 1))
    pw = _softmax_rows(sw, (dist >= 0) & (dist <= WINDOW))
    o_win = _dot(pw.astype(BF), kvw)

    g = g_ref[...]
    outs = []
    for h in range(N_HEADS):
        rows = slice(h * NSA_Q, (h + 1) * NSA_Q)
        acc = acc_sc[h]
        o_sel = acc / jnp.maximum(acc[:, 0:1], 1e-30)
        o = (g[:, 3 * h:3 * h + 1] * o_cmp[rows] + g[:, 3 * h + 1:3 * h + 2] * o_sel
             + g[:, 3 * h + 2:3 * h + 3] * o_win[rows])
        outs.append(o[:, HEAD_DIM:])
    o_ref[...] = jnp.concatenate(outs, axis=1).astype(BF)


def nsa_prompt(qa, gate, kcmp, ksel_b, kwin_b, mt, e_mat):
    bsz, t_len, _ = qa.shape
    n_cmp = kcmp.shape[1]
    n_sel = mt.shape[0]
    return pl.pallas_call(
        _nsa_prompt_kernel,
        grid=(bsz, t_len // NSA_Q),
        in_specs=[_row_spec(NSA_Q, 2 * GROUP_W), _row_spec(NSA_Q, LANES),
                  pl.BlockSpec((None, n_cmp, LANES), lambda b, t: (b, 0, 0)),
                  pl.BlockSpec((None, t_len, LANES), lambda b, t: (b, 0, 0)),
                  pl.BlockSpec((None, t_len, LANES), lambda b, t: (b, 0, 0)),
                  _const_spec((n_sel, n_cmp)), _const_spec((n_sel, t_len))],
        out_specs=_row_spec(NSA_Q, GROUP_W),
        out_shape=jax.ShapeDtypeStruct((bsz, t_len, GROUP_W), BF),
        scratch_shapes=[pltpu.VMEM((N_HEADS, NSA_Q, LANES), F32),
                        pltpu.VMEM((N_HEADS, NSA_Q, LANES), F32)],
        compiler_params=_params(("parallel", "arbitrary")),
        name="nsa_prompt",
    )(qa, gate, kcmp, ksel_b, kwin_b, mt, e_mat)


SB_Q = 256
SB_K = 256


SB_STOP = -110.0


def _softplus(z):
    return jnp.maximum(z, 0.0) + jnp.log(1.0 + jnp.exp(-jnp.abs(z)))


def _tri_neg(n):
    r = lax.broadcasted_iota(jnp.int32, (2 * n, n + LANES), 0)
    c = lax.broadcasted_iota(jnp.int32, (2 * n, n + LANES), 1)
    r = jnp.where(r >= n, r - n, r)
    return jnp.where((r > c) | (c >= n), -1.0, 0.0).astype(BF)


def _sb_weights(z, tri, carry, mask):
    n_k = z.shape[1]
    sp = _softplus(z)
    spm = sp if mask is None else jnp.where(mask, sp, 0.0)
    hi = spm.astype(BF)
    lo = (spm - hi.astype(F32)).astype(BF)
    cs = _dot(jnp.concatenate([hi, lo], axis=1), tri)
    after = cs[:, :n_k] + jnp.concatenate([carry] * (n_k // LANES), axis=1)
    a = jnp.exp(z - sp + after)
    if mask is not None:
        a = jnp.where(mask, a, 0.0)
    return a.astype(BF), carry + cs[:, n_k:]


def _sb_prompt_kernel(qb_ref, kvb_ref, o_ref, acc_sc, car_sc):
    i = pl.program_id(1)
    q = qb_ref[...]
    lane = lax.broadcasted_iota(jnp.int32, (1, GROUP_W), 1)
    qh = [jnp.where(lane // HEAD_DIM == h, q, jnp.zeros_like(q)) for h in range(N_HEADS)]
    tri = _tri_neg(SB_K)
    r = lax.broadcasted_iota(jnp.int32, (SB_Q, SB_K), 0)
    c = lax.broadcasted_iota(jnp.int32, (SB_Q, SB_K), 1)
    diag_mask = c < r

    def chunk(k0, mask, first):
        k = kvb_ref[pl.ds(k0, SB_K), 0:GROUP_W]
        v = kvb_ref[pl.ds(k0, SB_K), GROUP_W:2 * GROUP_W]
        car_max = None
        for h in range(N_HEADS):
            car = jnp.zeros((SB_Q, LANES), F32) if first else car_sc[h]
            a, car = _sb_weights(_dot_nt(qh[h], k), tri, car, mask)
            pv = _dot(a, v)
            acc_sc[h] = pv if first else acc_sc[h] + pv
            car_sc[h] = car
            car_max = car if car_max is None else jnp.maximum(car_max, car)
        return jnp.max(car_max)

    car_max = chunk(pl.multiple_of(i * SB_Q, SB_Q), diag_mask, True)

    def more(st):
        j, car_max = st
        return (j < i) & (car_max > SB_STOP)

    def body(st):
        j, _ = st
        return j + 1, chunk(pl.multiple_of((i - 1 - j) * SB_K, SB_K), None, False)

    lax.while_loop(more, body, (0, car_max))
    o = jnp.zeros((SB_Q, GROUP_W), F32)
    for h in range(N_HEADS):
        o = jnp.where(lane // HEAD_DIM == h, acc_sc[h], o)
    o_ref[...] = o.astype(BF)


def sb_prompt(qb, kvb_b):
    bsz, t_len, _ = qb.shape
    return pl.pallas_call(
        _sb_prompt_kernel,
        grid=(bsz, t_len // SB_Q),
        in_specs=[_row_spec(SB_Q, GROUP_W),
                  pl.BlockSpec((None, t_len, 2 * GROUP_W), lambda b, t: (b, 0, 0))],
        out_specs=_row_spec(SB_Q, GROUP_W),
        out_shape=jax.ShapeDtypeStruct((bsz, t_len, GROUP_W), BF),
        scratch_shapes=[pltpu.VMEM((N_HEADS, SB_Q, GROUP_W), F32),
                        pltpu.VMEM((N_HEADS, SB_Q, LANES), F32)],
        compiler_params=_params(("parallel", "arbitrary")),
        name="sb_prompt",
    )(qb, kvb_b)


def _pool_mix(ext_ref, tm, tpos, pool_w_ref, pool_s_ref):
    def shifted(ref, k):
        return ref[pl.ds(POOL_MAX - k, tm), :]
    x = shifted(ext_ref, 0)
    lane = lax.broadcasted_iota(jnp.int32, (1, GROUP_W), 1)
    grp = lane // HEAD_DIM
    s2 = x + shifted(ext_ref, 1)
    s4 = s2 + shifted(ext_ref, 2) + shifted(ext_ref, 3)
    s8 = s4 + sum(shifted(ext_ref, k) for k in range(4, 8))
    s16 = s8 + sum(shifted(ext_ref, k) for k in range(8, 16))
    tot = jnp.where(grp == 0, s2, jnp.where(grp == 1, s4, jnp.where(grp == 2, s8, s16)))
    wlen = jnp.where(grp == 0, 2, jnp.where(grp == 1, 4, jnp.where(grp == 2, 8, 16)))
    cnt = jnp.minimum(wlen, tpos + 1).astype(F32)
    d = tot / cnt - x
    return _dot(d.astype(BF), pool_w_ref[...]) * pool_s_ref[...]


def _gmlp_pool_kernel(u_ref, v_ref, pin_ref, halo_ref, sgw_ref, sgb_ref, pw_ref, ps_ref,
                      oc_ref, od_ref, ext_sc):
    t = pl.program_id(1)
    tm = u_ref.shape[0]
    lane = lax.broadcasted_iota(jnp.int32, (1, GROUP_W), 1)
    for c in range(tm // CHUNK):
        rows = slice(c * CHUNK, (c + 1) * CHUNK)
        v = v_ref[rows, :]
        vz = jnp.zeros_like(v)
        vst = jnp.concatenate([jnp.where(lane // HEAD_DIM == h, v, vz) for h in range(N_HEADS)], axis=0)
        s = _dot(sgw_ref[...], vst.astype(BF)) + sgb_ref[...]
        oc_ref[rows, :] = (u_ref[rows, :] * s).astype(BF)
    halo = halo_ref[...]
    ext_sc[0:POOL_MAX, :] = jnp.where(t > 0, halo, jnp.zeros_like(halo))
    ext_sc[POOL_MAX:POOL_MAX + tm, :] = pin_ref[...]
    tpos = t * tm + lax.broadcasted_iota(jnp.int32, (tm, 1), 0)
    od_ref[...] = _pool_mix(ext_sc, tm, tpos, pw_ref, ps_ref).astype(BF)


def gmlp_pool_prompt(u, v, pin, sgw_cat, sgb_full, pool_w_bd, pool_scale, tm):
    bsz, t_len, _ = u.shape
    per = tm // POOL_MAX
    halo_spec = pl.BlockSpec((None, POOL_MAX, GROUP_W), lambda b, t: (b, jnp.maximum(t * per - 1, 0), 0))
    return pl.pallas_call(
        _gmlp_pool_kernel,
        grid=(bsz, t_len // tm),
        in_specs=[_row_spec(tm, GROUP_W), _row_spec(tm, GROUP_W), _row_spec(tm, GROUP_W), halo_spec,
                  _const_spec((CHUNK, N_HEADS * CHUNK)), _const_spec((CHUNK, GROUP_W)),
                  _const_spec((GROUP_W, GROUP_W)), _const_spec((1, GROUP_W))],
        out_specs=[_row_spec(tm, GROUP_W), _row_spec(tm, GROUP_W)],
        out_shape=[jax.ShapeDtypeStruct((bsz, t_len, GROUP_W), BF)] * 2,
        scratch_shapes=[pltpu.VMEM((POOL_MAX + tm, GROUP_W), F32)],
        compiler_params=_params(("parallel", "parallel")),
        name="gmlp_pool",
    )(u, v, pin, pin, sgw_cat, sgb_full, pool_w_bd, pool_scale)


def _swap_neg(w):
    half = HEAD_DIM // 2
    w = w.reshape(w.shape[0], -1, 2, half)
    return jnp.stack([-w[:, :, 1], w[:, :, 0]], axis=2).reshape(w.shape[0], -1)


def _pad_heads(w):
    w = w.reshape(w.shape[0], -1, HEAD_DIM)
    return jnp.concatenate([w, jnp.zeros_like(w)], axis=2).reshape(w.shape[0], -1)


def _proj_weights(w_in):
    a_q, a_kv, a_g, b_qkv, c_uv, d_in = jnp.split(
        w_in, np.cumsum([GROUP_W, 6 * HEAD_DIM, 3 * N_HEADS, 3 * GROUP_W, 2 * GROUP_W]).tolist(), axis=1)
    scale = HEAD_DIM ** -0.5
    a_q = a_q * scale
    kv = a_kv.reshape(-1, 3, 2, HEAD_DIM)
    kv_sw = jnp.concatenate([_swap_neg(kv[:, :, 0].reshape(-1, 3 * HEAD_DIM)).reshape(-1, 3, 1, HEAD_DIM),
                             jnp.zeros_like(kv[:, :, 1:2])], axis=2).reshape(-1, 6 * HEAD_DIM)
    g_pad = jnp.pad(a_g, ((0, 0), (0, LANES - 3 * N_HEADS)))
    b_q = b_qkv[:, :GROUP_W] * scale
    w_all = jnp.concatenate([_pad_heads(a_q), _pad_heads(_swap_neg(a_q)), a_kv, kv_sw, g_pad,
                             b_q, b_qkv[:, GROUP_W:], c_uv, d_in], axis=1)
    w_t = jnp.concatenate([a_kv, kv_sw, b_qkv[:, GROUP_W:]], axis=1).T
    return w_all.astype(BF), w_t.astype(BF)


def _rope_tables(pos):
    half = HEAD_DIM // 2
    inv = ROPE_THETA ** (-jnp.arange(half, dtype=F32) / half)
    ang = pos.astype(F32)[:, None] * inv[None, :]
    cos, sin = jnp.cos(ang), jnp.sin(ang)
    cos_t = jnp.concatenate([cos.T, cos.T, jnp.ones((HEAD_DIM, pos.shape[0]), F32)], axis=0)
    sin_t = jnp.concatenate([sin.T, sin.T, jnp.zeros((HEAD_DIM, pos.shape[0]), F32)], axis=0)
    return jnp.tile(cos, (1, 4)), jnp.tile(sin, (1, 4)), cos_t, sin_t


def _cmp_weights(cmp_w):
    w = jnp.zeros((CMP_LEN, 2, HEAD_DIM, 2, HEAD_DIM), F32)
    w = w.at[:, 0, :, 0, :].set(cmp_w[0]).at[:, 1, :, 1, :].set(cmp_w[1])
    return w.reshape(2, CMP_STRIDE * LANES, LANES).astype(BF)


def _sel_constants(n_cmp, n_sel, t_len):
    n = np.arange(n_cmp)[None, :]
    j = np.arange(n_sel)[:, None]
    mt = ((n >= 4 * j - 1) & (n <= 4 * j + 3)).astype(np.float32) + ((n >= 4 * j) & (n <= 4 * j + 2))
    e = (np.arange(t_len)[None, :] // SEL_BLOCK == j).astype(np.float32)
    return jnp.asarray(mt, BF), jnp.asarray(e, BF)


def _gmlp_weights(sg_w, sg_b):
    wm = sg_w * jnp.tril(jnp.ones((CHUNK, CHUNK), sg_w.dtype))
    w_cat = jnp.transpose(wm, (1, 0, 2)).reshape(CHUNK, N_HEADS * CHUNK).astype(BF)
    b_full = jnp.repeat(sg_b.T, HEAD_DIM, axis=1)
    return w_cat, b_full


def _pool_weights(pool_w):
    w = jnp.zeros((N_HEADS, HEAD_DIM, N_HEADS, HEAD_DIM), F32)
    for g in range(N_HEADS):
        w = w.at[g, :, g, :].set(pool_w[g])
    return w.reshape(GROUP_W, GROUP_W).astype(BF)


def _sample_sel_weights(n_cmp_pad, n_sel_pad, n_cmp, n_sel):
    n = np.arange(n_cmp_pad)[:, None]
    j = np.arange(n_sel_pad)[None, :]
    ms = ((n >= 4 * j - 1) & (n <= 4 * j + 3)).astype(np.float32) + ((n >= 4 * j) & (n <= 4 * j + 2))
    ms = ms * ((n < n_cmp) & (j < n_sel))
    return jnp.asarray(ms, BF)


def _layer_weights(l, w_ada, b_ada, norm_g, w_ffn_up, w_ffn_down, w_in, w_out,
                   cmp_pe, cmp_w, sg_ln_g, sg_ln_b, sg_w, sg_b, pool_w, pool_scale):
    sgw_cat, sgb_full = _gmlp_weights(sg_w[l], sg_b[l])
    w_proj, w_proj_t = _proj_weights(w_in[l])
    return dict(
        w_ada=w_ada[l].astype(BF), b_ada=b_ada[l][None, :], norm_g=norm_g[l][:, None, :],
        w_up=w_ffn_up[l].astype(BF), w_down=w_ffn_down[l].astype(BF),
        w_in=w_proj, w_in_t=w_proj_t, w_out=w_out[l].astype(BF),
        cmp_pe=cmp_pe[l].reshape(CMP_LEN, LANES), cmp_w=_cmp_weights(cmp_w[l]),
        ln_g=sg_ln_g[l][None, :], ln_b=sg_ln_b[l][None, :], sgw_cat=sgw_cat, sgb_full=sgb_full,
        sg_w00=jnp.repeat(sg_w[l][:, 0, 0], HEAD_DIM)[None, :], sg_b0=jnp.repeat(sg_b[l][:, 0], HEAD_DIM)[None, :],
        pool_w=_pool_weights(pool_w[l]), pool_scale=pool_scale[l][None, :])


def _mods(mod, shape):
    m = mod.reshape(mod.shape[0], N_MOD, D_MODEL)
    return [m[:, k].reshape(shape) for k in range(N_MOD)]


def prompt_layer(x, mod, lw, tables, consts, tm):
    bsz, t_len, _ = x.shape
    m = _mods(mod, (bsz, 1, D_MODEL))
    g = lw["norm_g"]
    cos, sin, cos_t, sin_t = tables
    x = ffn_half(x, m[0], m[1], m[2], g[0], g[1], lw["w_up"][0], lw["w_down"][0], tm)
    pr = in_proj(x, m[3], m[4], g[2], cos, sin, lw["w_in"], lw["ln_g"], lw["ln_b"], tm,
                 t_side=(lw["w_in_t"], cos_t, sin_t))
    kcmp = compress_prompt(pr["kv_cmp"], lw["cmp_pe"], lw["cmp_w"])
    o_a = nsa_prompt(pr["qa"], pr["gate"], kcmp, pr["kv_sel_b"], pr["kv_win_b"], *consts)
    o_b = sb_prompt(pr["qb"], pr["kvb_b"])
    o_c, o_d = gmlp_pool_prompt(pr["u"], pr["v"], pr["pin"], lw["sgw_cat"], lw["sgb_full"],
                                lw["pool_w"], lw["pool_scale"], tm)
    x = out_proj(x, o_a, o_b, o_c, o_d, m[5], g[3], lw["w_out"], tm)
    x = ffn_half(x, m[6], m[7], m[8], g[4], g[5], lw["w_up"][1], lw["w_down"][1], tm)
    n_win = min(WINDOW, t_len)
    state = (pr["kvt_cmp"], pr["kvt_sel"], pr["kvt_win"][:, :, t_len - n_win:], pr["kvbt"],
             pr["pin"][:, t_len - POOL_BUF:])
    return x, state


def _page_copies(cache_ref, layer, pt_ref, b, first_page, n_pages, buf_ref, slot, sem_ref):
    return [pltpu.make_async_copy(cache_ref.at[layer, pt_ref[b, first_page + p]], buf_ref.at[slot, p],
                                  sem_ref.at[slot]) for p in range(n_pages)]


SB_GROUP = 4


def _sb_sample_kernel(pt_ref, q_ref, cache_ref, o_ref, buf0, buf, sem0, sem, *, layer, n_pages):
    b = pl.program_id(0)
    n_grp = n_pages // SB_GROUP

    def copies(seq, grp, dst, slot, sm):
        return _page_copies(cache_ref, layer, pt_ref, seq, (n_grp - 1 - grp) * SB_GROUP, SB_GROUP, dst, slot, sm)

    @pl.when(b == 0)
    def _():
        for c in copies(0, 0, buf0, 0, sem0):
            c.start()

    @pl.when(b + 1 < pl.num_programs(0))
    def _():
        for c in copies(b + 1, 0, buf0, (b + 1) % 2, sem0):
            c.start()

    if n_grp > 1:
        for c in copies(b, 1, buf, 1, sem):
            c.start()

    row = lax.broadcasted_iota(jnp.int32, (8, GROUP_W), 0)
    lane = lax.broadcasted_iota(jnp.int32, (8, GROUP_W), 1)
    head_lanes = lane // HEAD_DIM == row
    q = jnp.broadcast_to(q_ref[...].astype(F32), (8, GROUP_W))
    qm = jnp.where(head_lanes, q, 0.0).astype(BF)
    tri = _tri_neg(PAGE)

    def sweep(src, slot, acc, car):
        for p in reversed(range(SB_GROUP)):
            kt = src[slot, p, 0:GROUP_W, :].astype(BF)
            vt = src[slot, p, GROUP_W:2 * GROUP_W, :].astype(BF)
            a, car = _sb_weights(_dot(qm, kt), tri, car, None)
            acc = acc + _dot_nt(a, vt)
        return acc, car

    for c in copies(b, 0, buf0, b % 2, sem0):
        c.wait()
    acc, car = sweep(buf0, b % 2, jnp.zeros((8, GROUP_W), F32), jnp.zeros((8, LANES), F32))

    def more(st):
        g, car_max, _, _ = st
        return (g < n_grp) & (car_max > SB_STOP)

    def body(st):
        g, _, acc, car = st
        slot = g % 2

        @pl.when(g + 1 < n_grp)
        def _():
            for c in copies(b, g + 1, buf, 1 - slot, sem):
                c.start()

        for c in copies(b, g, buf, slot, sem):
            c.wait()
        acc, car = sweep(buf, slot, acc, car)
        return g + 1, jnp.max(car), acc, car

    g, _, acc, _ = lax.while_loop(more, body, (1, jnp.max(car), acc, car))

    @pl.when(g < n_grp)
    def _():
        for c in copies(b, g, buf, g % 2, sem):
            c.wait()

    o_ref[...] = jnp.sum(jnp.where(head_lanes, acc, 0.0), axis=0, keepdims=True)


def sb_sample(page_table, qb, cache_t, layer):
    n_seq = qb.shape[0]
    n_pages = page_table.shape[1]
    grp_buf = pltpu.VMEM((2, SB_GROUP, 2 * GROUP_W, PAGE), F32)
    return pl.pallas_call(
        functools.partial(_sb_sample_kernel, layer=layer, n_pages=n_pages),
        grid_spec=pltpu.PrefetchScalarGridSpec(
            num_scalar_prefetch=1, grid=(n_seq,),
            in_specs=[pl.BlockSpec((None, 1, GROUP_W), lambda b, pt: (b, 0, 0)),
                      pl.BlockSpec(memory_space=pl.ANY)],
            out_specs=pl.BlockSpec((None, 1, GROUP_W), lambda b, pt: (b, 0, 0)),
            scratch_shapes=[grp_buf, grp_buf, pltpu.SemaphoreType.DMA((2,)), pltpu.SemaphoreType.DMA((2,))]),
        out_shape=jax.ShapeDtypeStruct((n_seq, 1, GROUP_W), F32),
        compiler_params=_params(("arbitrary",)),
        name="sb_sample",
    )(page_table, qb, cache_t)


def _slab_copies(cache_ref, layer, pt_ref, b, n_pages, buf_ref, slot, sem_ref):
    return [pltpu.make_async_copy(cache_ref.at[layer, pt_ref[b, p]],
                                  buf_ref.at[slot, :, pl.ds(p * PAGE, PAGE)], sem_ref.at[slot])
            for p in range(n_pages)]


CMP_CHUNK = 256
TAIL_ROWS = 128
XPOSE_TOKENS = 1024
SLAB_CAST = 2048


def _nsa_cmp_sample_kernel(pt_ref, q_ref, new_ref, pe_ref, w_ref, ms_ref, cache_ref, ocmp_ref, sel_ref,
                           buf, sem, rows_sc, lo_sc, hi_sc, *, layer, n_pages):
    b = pl.program_id(0)
    slot = b % 2

    def copies(seq, sl):
        return _slab_copies(cache_ref, layer, pt_ref, seq, n_pages, buf, sl, sem)

    @pl.when(b == 0)
    def _():
        for c in copies(0, 0):
            c.start()

    @pl.when(b + 1 < pl.num_programs(0))
    def _():
        for c in copies(b + 1, 1 - slot):
            c.start()

    for c in copies(b, slot):
        c.wait()

    past = n_pages * PAGE
    n_pad = lo_sc.shape[0]
    for c in range(past // XPOSE_TOKENS):
        span = slice(c * XPOSE_TOKENS, (c + 1) * XPOSE_TOKENS)
        rows_sc[span, :] = buf[slot, :, span].T
    r_io = lax.broadcasted_iota(jnp.int32, (TAIL_ROWS, LANES), 0)
    rows_sc[past:past + TAIL_ROWS, :] = jnp.where(r_io == 0, jnp.broadcast_to(new_ref[...], (TAIL_ROWS, LANES)), 0.0)

    chunk = math.gcd(CMP_CHUNK, past // CMP_STRIDE)

    def cmp_chunk(c, carry):
        g0 = pl.multiple_of(c * chunk, chunk)
        lo, hi = _compress_rows(rows_sc, chunk, pe_ref, w_ref, base=g0 * CMP_STRIDE)
        lo_sc[pl.ds(g0, chunk), :] = lo
        hi_sc[pl.ds(g0, chunk), :] = hi
        return carry

    lax.fori_loop(0, past // (CMP_STRIDE * chunk), cmp_chunk, 0)
    n_tail = TAIL_ROWS // CMP_STRIDE
    g_tail = past // CMP_STRIDE
    lo, hi = _compress_rows(rows_sc, n_tail, pe_ref, w_ref, base=past)
    lo_sc[g_tail:g_tail + n_tail, :] = lo
    hi_sc[g_tail:g_tail + n_tail, :] = hi
    lo_sc[g_tail + n_tail:n_pad, :] = jnp.zeros((n_pad - g_tail - n_tail, LANES), F32)
    hi_sc[g_tail + n_tail:n_pad + 8, :] = jnp.zeros((n_pad + 8 - g_tail - n_tail, LANES), F32)
    kc = (lo_sc[...] + hi_sc[pl.ds(1, n_pad), :]).astype(BF)

    q8 = q_ref[...]
    qpos = past
    s = _dot_nt(q8, kc)
    blk_end = lax.broadcasted_iota(jnp.int32, (1, n_pad), 1) * CMP_STRIDE + (CMP_LEN - 1)
    p = _softmax_rows(s, blk_end <= qpos)
    ocmp_ref[...] = _dot(p.astype(BF), kc)
    ps = jnp.broadcast_to(jnp.sum(p[0:N_HEADS], axis=0, keepdims=True), (8, n_pad))
    ms = ms_ref[...]
    p_slc = sum(_dot(part, ms) for part in _split3(ps))
    j_io = lax.broadcasted_iota(jnp.int32, p_slc.shape, 1)
    cur = qpos // SEL_BLOCK
    forced = (j_io == 0) | (j_io == cur) | (j_io == cur - 1)
    valid = j_io <= cur
    score = jnp.where(valid, jnp.where(forced, FORCE_SCORE, p_slc), NEG)
    sel_ref[...] = _topk_select(score, valid, N_TOPK, axis=1)


def nsa_cmp_sample(page_table, q8, new_cmp, pe, w_cmp, ms, cache_t, layer):
    n_seq = q8.shape[0]
    n_pages = page_table.shape[1]
    n_pad, n_sel_pad = ms.shape
    seq_spec = lambda w: pl.BlockSpec((None, 8, w), lambda b, pt: (b, 0, 0))
    const2 = lambda shape: pl.BlockSpec(shape, lambda b, pt: (0,) * len(shape))
    return pl.pallas_call(
        functools.partial(_nsa_cmp_sample_kernel, layer=layer, n_pages=n_pages),
        grid_spec=pltpu.PrefetchScalarGridSpec(
            num_scalar_prefetch=1, grid=(n_seq,),
            in_specs=[seq_spec(LANES), pl.BlockSpec((None, 1, LANES), lambda b, pt: (b, 0, 0)),
                      const2((CMP_LEN, LANES)), const2((2, CMP_STRIDE * LANES, LANES)), const2((n_pad, n_sel_pad)),
                      pl.BlockSpec(memory_space=pl.ANY)],
            out_specs=[seq_spec(LANES), seq_spec(n_sel_pad)],
            scratch_shapes=[pltpu.VMEM((2, LANES, n_pages * PAGE), F32), pltpu.SemaphoreType.DMA((2,)),
                            pltpu.VMEM((n_pages * PAGE + TAIL_ROWS, LANES), F32),
                            pltpu.VMEM((n_pad, LANES), F32), pltpu.VMEM((n_pad + 8, LANES), F32)]),
        out_shape=[jax.ShapeDtypeStruct((n_seq, 8, LANES), F32), jax.ShapeDtypeStruct((n_seq, 8, n_sel_pad), F32)],
        compiler_params=_params(("arbitrary",)),
        name="nsa_cmp_sample",
    )(page_table, q8, new_cmp, pe, w_cmp, ms, cache_t)


def _nsa_sel_sample_kernel(pt_ref, q_ref, sel_ref, ocmp_ref, gate_ref, news_ref, neww_ref, e_ref, win_ref,
                           cache_ref, o_ref, buf, sem, kv_sc, *, layer, n_pages):
    b = pl.program_id(0)
    slot = b % 2

    def copies(seq, sl):
        return _slab_copies(cache_ref, layer, pt_ref, seq, n_pages, buf, sl, sem)

    @pl.when(b == 0)
    def _():
        for c in copies(0, 0):
            c.start()

    @pl.when(b + 1 < pl.num_programs(0))
    def _():
        for c in copies(b + 1, 1 - slot):
            c.start()

    for c in copies(b, slot):
        c.wait()

    past = n_pages * PAGE
    qpos = past
    q8 = q_ref[...]
    q8f = q8.astype(F32)
    sel = sel_ref[...]
    cast = math.gcd(SLAB_CAST, past)
    for c in range(past // cast):
        span = slice(c * cast, (c + 1) * cast)
        kv_sc[:, span] = buf[slot, :, span].astype(BF)
    kvt = kv_sc[...]

    def new_token(new_ref, allowed):
        kn = new_ref[...].astype(BF).astype(F32)
        s_new = jnp.sum(q8f * kn, axis=-1, keepdims=True)
        return kn, jnp.where(allowed, s_new, NEG)

    blk = e_ref.shape[0]
    sel_b = sel.astype(BF)
    selx = jnp.concatenate([_dot(sel_b[:, c * blk:(c + 1) * blk], e_ref[...])
                            for c in range(past // (blk * SEL_BLOCK))], axis=1)
    allow = (selx > 0.5) & (lax.broadcasted_iota(jnp.int32, (1, past), 1) <= qpos)
    sm = jnp.where(allow, _dot(q8, kvt), NEG)
    kn_s, s_new = new_token(news_ref, (sel[:, qpos // SEL_BLOCK:qpos // SEL_BLOCK + 1] > 0.5))
    m = jnp.maximum(jnp.max(sm, axis=-1, keepdims=True), s_new)
    e = jnp.where(allow, jnp.exp(sm - m), 0.0)
    e_new = jnp.where(s_new > 0.5 * NEG, jnp.exp(s_new - m), 0.0)
    l_sel = jnp.sum(e, axis=-1, keepdims=True) + e_new
    o_sel = (_dot_nt(e.astype(BF), kvt) + e_new * kn_s) / jnp.maximum(l_sel, 1e-30)

    n_buf = win_ref.shape[1]
    wb = win_ref[...].astype(BF)
    kwpos = past - n_buf + lax.broadcasted_iota(jnp.int32, (1, n_buf), 1)
    dist = qpos - kwpos
    allow_w = (dist >= 0) & (dist <= WINDOW) & (kwpos >= 0)
    sw = jnp.where(allow_w, _dot(q8, wb), NEG)
    kn_w, sw_new = new_token(neww_ref, True)
    mw = jnp.maximum(jnp.max(sw, axis=-1, keepdims=True), sw_new)
    ew = jnp.where(allow_w, jnp.exp(sw - mw), 0.0)
    ew_new = jnp.exp(sw_new - mw)
    l_w = jnp.sum(ew, axis=-1, keepdims=True) + ew_new
    o_win = (_dot_nt(ew.astype(BF), wb) + ew_new * kn_w) / jnp.maximum(l_w, 1e-30)

    g = jnp.broadcast_to(gate_ref[...], (8, LANES))
    g_row = lax.broadcasted_iota(jnp.int32, (8, LANES), 0)
    g_lane = lax.broadcasted_iota(jnp.int32, (8, LANES), 1)
    gk = [jnp.sum(jnp.where(g_lane == 3 * g_row + k, g, 0.0), axis=-1, keepdims=True) for k in range(3)]
    o_ref[...] = gk[0] * ocmp_ref[...] + gk[1] * o_sel + gk[2] * o_win


def nsa_sel_sample(page_table, q8, sel, o_cmp, gate, new_sel, new_win, e_mat, win_t, cache_t, layer):
    n_seq = q8.shape[0]
    n_pages = page_table.shape[1]
    n_buf = win_t.shape[3]
    seq_spec = lambda r, w: pl.BlockSpec((None, r, w), lambda b, pt: (b, 0, 0))
    return pl.pallas_call(
        functools.partial(_nsa_sel_sample_kernel, layer=layer, n_pages=n_pages),
        grid_spec=pltpu.PrefetchScalarGridSpec(
            num_scalar_prefetch=1, grid=(n_seq,),
            in_specs=[seq_spec(8, LANES), seq_spec(8, sel.shape[2]), seq_spec(8, LANES), seq_spec(1, LANES),
                      seq_spec(1, LANES), seq_spec(1, LANES),
                      pl.BlockSpec(e_mat.shape, lambda b, pt: (0, 0)),
                      pl.BlockSpec((None, None, LANES, n_buf), lambda b, pt: (layer, b, 0, 0)),
                      pl.BlockSpec(memory_space=pl.ANY)],
            out_specs=seq_spec(8, LANES),
            scratch_shapes=[pltpu.VMEM((2, LANES, n_pages * PAGE), F32), pltpu.SemaphoreType.DMA((2,)),
                            pltpu.VMEM((LANES, n_pages * PAGE), BF)]),
        out_shape=jax.ShapeDtypeStruct((n_seq, 8, LANES), F32),
        compiler_params=_params(("arbitrary",)),
        name="nsa_sel_sample",
    )(page_table, q8, sel, o_cmp, gate, new_sel, new_win, e_mat, win_t, cache_t)


def _gmlp_pool_sample_kernel(u_ref, v_ref, pin_ref, hist_ref, w00_ref, b0_ref, pw_ref, ps_ref, oc_ref, od_ref,
                             *, past_len):
    oc_ref[...] = u_ref[...] * (w00_ref[...] * v_ref[...] + b0_ref[...])
    x = pin_ref[...]
    lane = lax.broadcasted_iota(jnp.int32, (1, GROUP_W), 1)
    grp = lane // HEAD_DIM
    sums = []
    tot = x
    k = 1
    for wlen in POOL_WINDOWS:
        while k < wlen:
            tot = tot + hist_ref[POOL_BUF - k]
            k += 1
        sums.append(tot)
    tot = jnp.where(grp == 0, sums[0], jnp.where(grp == 1, sums[1], jnp.where(grp == 2, sums[2], sums[3])))
    wlen = jnp.where(grp == 0, POOL_WINDOWS[0], jnp.where(grp == 1, POOL_WINDOWS[1],
                                                          jnp.where(grp == 2, POOL_WINDOWS[2], POOL_WINDOWS[3])))
    cnt = jnp.minimum(wlen, past_len + 1).astype(F32)
    d = tot / cnt - x
    od_ref[...] = _dot(d.astype(BF), pw_ref[...]) * ps_ref[...]


def gmlp_pool_sample(u, v, pin, hist_t, layer, w00, b0, pool_w_bd, pool_scale, past_len):
    n_seq = u.shape[0]
    full = lambda shape: pl.BlockSpec(shape, lambda i: (0,) * len(shape))
    return pl.pallas_call(
        functools.partial(_gmlp_pool_sample_kernel, past_len=past_len),
        grid=(1,),
        in_specs=[full((n_seq, GROUP_W))] * 3 +
                 [pl.BlockSpec((None, POOL_BUF, n_seq, GROUP_W), lambda i: (layer, 0, 0, 0)),
                  full((1, GROUP_W)), full((1, GROUP_W)), full((GROUP_W, GROUP_W)), full((1, GROUP_W))],
        out_specs=[full((n_seq, GROUP_W))] * 2,
        out_shape=[jax.ShapeDtypeStruct((n_seq, GROUP_W), F32)] * 2,
        compiler_params=_params(("arbitrary",)),
        name="gmlp_pool_sample",
    )(u, v, pin, hist_t, w00, b0, pool_w_bd, pool_scale)


def sample_layer(x, mod, lw, tables, sel_consts, caches_t, layer, page_table):
    n_seq = x.shape[1]
    past_len = page_table.shape[1] * PAGE
    cmp_t, sel_t, win_t, sb_t, pool_t = caches_t
    ms, e_mat = sel_consts
    m = _mods(mod, (1, n_seq, D_MODEL))
    g = lw["norm_g"]
    x = ffn_half(x, m[0], m[1], m[2], g[0], g[1], lw["w_up"][0], lw["w_down"][0], n_seq)
    pr = in_proj(x, m[3], m[4], g[2], tables[0], tables[1], lw["w_in"], lw["ln_g"], lw["ln_b"], n_seq)
    rows = {k: v[0] for k, v in pr.items()}
    q8 = jnp.pad(rows["qa"].reshape(n_seq, N_HEADS, LANES), ((0, 0), (0, 8 - N_HEADS), (0, 0)))
    o_cmp, sel = nsa_cmp_sample(page_table, q8, rows["kv_cmp"][:, None, :], lw["cmp_pe"], lw["cmp_w"], ms,
                                cmp_t, layer)
    o8 = nsa_sel_sample(page_table, q8, sel, o_cmp, rows["gate"][:, None, :], rows["kv_sel"][:, None, :],
                        rows["kv_win"][:, None, :], e_mat, win_t, sel_t, layer)
    o_a = o8[:, :N_HEADS, HEAD_DIM:].reshape(1, n_seq, GROUP_W).astype(BF)
    o_b = sb_sample(page_table, rows["qb"][:, None, :], sb_t, layer).reshape(1, n_seq, GROUP_W).astype(BF)
    o_c, o_d = gmlp_pool_sample(rows["u"], rows["v"], rows["pin"], pool_t, layer, lw["sg_w00"], lw["sg_b0"],
                                lw["pool_w"], lw["pool_scale"], past_len)
    x = out_proj(x, o_a, o_b, o_c[None].astype(BF), o_d[None].astype(BF), m[5], g[3], lw["w_out"], n_seq)
    x = ffn_half(x, m[6], m[7], m[8], g[4], g[5], lw["w_up"][1], lw["w_down"][1], n_seq)
    state = (rows["kv_cmp"], rows["kv_sel"], rows["kv_win"], rows["kvb"], rows["pin"], rows["v"])
    return x, state


PROMPT_TM = 512


def kernel(x_prompt, x_sample, cache_nsa_cmp, cache_nsa_sel, cache_nsa_win, cache_sb, state_pool, page_table,
           c_prompt, c_sample, w_ada, b_ada, norm_g, w_ffn_up, w_ffn_down, w_in, w_out,
           cmp_pe, cmp_w, sg_ln_g, sg_ln_b, sg_w, sg_b, pool_w, pool_scale):
    n_p, t_len, _ = x_prompt.shape
    n_s, t_dec, _ = x_sample.shape
    assert t_dec == 1, "the sample step advances one token per sequence"
    depth = w_ada.shape[0]
    n_phys = cache_sb.shape[1]
    past_len = page_table.shape[1] * PAGE
    c_all = jnp.concatenate([c_prompt, c_sample], axis=0)
    c_all = jnp.pad(c_all, ((0, (-c_all.shape[0]) % 8), (0, 0)))
    tables_p = _rope_tables(jnp.arange(t_len))
    tables_s = _rope_tables(jnp.full((n_s,), past_len))
    consts = _sel_constants(t_len // CMP_STRIDE, t_len // SEL_BLOCK, t_len)
    n_rows = -(-(past_len + t_dec) // SEL_BLOCK) * SEL_BLOCK
    n_cmp = n_rows // CMP_STRIDE - 1
    n_sel = (n_cmp + 1) // (SEL_BLOCK // CMP_STRIDE)
    ms = _sample_sel_weights(-(-((past_len + TAIL_ROWS) // CMP_STRIDE) // LANES) * LANES,
                             -(-n_sel // LANES) * LANES, n_cmp, n_sel)
    blk_s = math.gcd(LANES, past_len // SEL_BLOCK)
    sel_consts_s = (ms, _sel_constants(1, blk_s, blk_s * SEL_BLOCK)[1])
    caches_t = (jnp.transpose(cache_nsa_cmp, (0, 1, 3, 4, 2)).reshape(depth, n_phys, LANES, PAGE),
                jnp.transpose(cache_nsa_sel, (0, 1, 3, 4, 2)).reshape(depth, n_phys, LANES, PAGE),
                jnp.transpose(cache_nsa_win, (0, 1, 3, 4, 2)).reshape(depth, n_s, LANES, -1),
                jnp.transpose(cache_sb, (0, 1, 3, 4, 5, 2)).reshape(depth, n_phys, 2 * GROUP_W, PAGE),
                jnp.transpose(state_pool, (0, 2, 1, 3)))
    y_p, y_s = x_prompt, x_sample.reshape(1, n_s, D_MODEL)
    st_p, st_s = [], []
    for l in range(depth):
        lw = _layer_weights(l, w_ada, b_ada, norm_g, w_ffn_up, w_ffn_down, w_in, w_out,
                            cmp_pe, cmp_w, sg_ln_g, sg_ln_b, sg_w, sg_b, pool_w, pool_scale)
        mod = ada_mod(c_all, lw["w_ada"], lw["b_ada"])
        y_p, s = prompt_layer(y_p, mod[:n_p], lw, tables_p, consts, PROMPT_TM)
        st_p.append(s)
        y_s, s = sample_layer(y_s, mod[n_p:n_p + n_s], lw, tables_s, sel_consts_s, caches_t, l, page_table)
        st_s.append(s)

    def stack(states, i):
        return jnp.stack([s[i] for s in states], axis=0)

    def kv_rows(x_t):
        d, bsz, _, t = x_t.shape
        return jnp.transpose(x_t.reshape(d, bsz, 2, HEAD_DIM, t), (0, 1, 4, 2, 3))

    sb_p = stack(st_p, 3)
    sb_p = jnp.transpose(sb_p.reshape(depth, n_p, 2, N_HEADS, HEAD_DIM, t_len), (0, 1, 5, 2, 3, 4))
    win_new = stack(st_s, 2).reshape(depth, n_s, 1, 2, HEAD_DIM)
    pool_new = stack(st_s, 4)[:, :, None, :]
    return (y_p, y_s.reshape(n_s, 1, D_MODEL),
            kv_rows(stack(st_p, 0)), stack(st_s, 0).reshape(depth, n_s, 1, 2, HEAD_DIM),
            kv_rows(stack(st_p, 1)), stack(st_s, 1).reshape(depth, n_s, 1, 2, HEAD_DIM),
            kv_rows(stack(st_p, 2)), jnp.concatenate([cache_nsa_win[:, :, 1:], win_new], axis=2),
            sb_p, stack(st_s, 3).reshape(depth, n_s, 1, 2, N_HEADS, HEAD_DIM),
            stack(st_p, 4), jnp.concatenate([state_pool[:, :, 1:], pool_new], axis=2),
            stack(st_s, 5)[:, :, None, :])
```

```python
import functools
import math

import numpy as np
import jax
import jax.numpy as jnp
from jax import lax
from jax.experimental import pallas as pl
from jax.experimental.pallas import tpu as pltpu

F32 = jnp.float32
BF = jnp.bfloat16

D_MODEL = 1024
HEAD_DIM = 64
N_HEADS = 4
GROUP_W = N_HEADS * HEAD_DIM
D_FF = 2816
N_MOD = 9
PAGE = 128
CMP_STRIDE = 16
CMP_LEN = 32
SEL_BLOCK = 64
N_TOPK = 16
WINDOW = 512
CHUNK = 128
POOL_WINDOWS = (2, 4, 8, 16)
POOL_MAX = 16
POOL_BUF = POOL_MAX - 1
ROPE_THETA = 10000.0
EPS = 1e-6
FORCE_SCORE = 1e9
NEG = -3.0e38
MASK_BIAS = 1.0e30
LANES = 128
VMEM_LIMIT = 56 * 1024 * 1024

C_QA, C_QAS, C_KV, C_KVS, C_G, C_QB, C_KVB, C_UV, C_DIN, C_END = (
    0, 512, 1024, 1408, 1792, 1920, 2176, 2688, 3200, 3456)


def _params(sem, vmem=VMEM_LIMIT):
    return pltpu.CompilerParams(dimension_semantics=sem, vmem_limit_bytes=vmem)


def _dot(a, b):
    return jnp.dot(a, b, preferred_element_type=F32)


def _dot_nt(a, b):
    return lax.dot_general(a, b, (((1,), (1,)), ((), ())), preferred_element_type=F32)


def _rms(x, g):
    return x * lax.rsqrt(jnp.mean(x * x, axis=-1, keepdims=True) + EPS) * g


def _split3(x):
    hi = x.astype(BF)
    r = x - hi.astype(F32)
    mid = r.astype(BF)
    lo = (r - mid.astype(F32)).astype(BF)
    return hi, mid, lo


def _ada_kernel(c_ref, w_ref, b_ref, o_ref):
    c = c_ref[...]
    s = (c * jax.nn.sigmoid(c)).astype(BF)
    o_ref[...] = _dot(s, w_ref[...]) + b_ref[...]


def _stacked_spec(tail, idx):
    return pl.BlockSpec((None,) * len(idx) + tail, lambda *_: idx + (0,) * len(tail))


def ada_mod(c_all, w_ada, b_ada, layer):
    r = c_all.shape[0]
    n = w_ada.shape[2]
    tn = 2304
    return pl.pallas_call(
        _ada_kernel,
        grid=(n // tn,),
        in_specs=[pl.BlockSpec((r, D_MODEL), lambda j: (0, 0)),
                  pl.BlockSpec((None, D_MODEL, tn), lambda j: (layer, 0, j)),
                  pl.BlockSpec((1, tn), lambda j: (0, j))],
        out_specs=pl.BlockSpec((r, tn), lambda j: (0, j)),
        out_shape=jax.ShapeDtypeStruct((r, n), F32),
        compiler_params=_params(("parallel",)),
        name="ada_mod",
    )(c_all, w_ada, b_ada)


FF_CHUNK = 256


def _ffn_kernel(x_ref, sh_ref, sc_ref, gt_ref, g1_ref, g2_ref, wu_ref, wd_ref, o_ref):
    x = x_ref[...]
    h = _rms(x, g1_ref[...]) * (1.0 + sc_ref[...]) + sh_ref[...]
    hb = h.astype(BF)
    acc = jnp.zeros(x.shape, F32)
    for c in range(D_FF // FF_CHUNK):
        lo = c * FF_CHUNK
        gate = _dot(hb, wu_ref[:, lo:lo + FF_CHUNK])
        up = _dot(hb, wu_ref[:, D_FF + lo:D_FF + lo + FF_CHUNK])
        a = (gate * jax.nn.sigmoid(gate) * up).astype(BF)
        acc = acc + _dot(a, wd_ref[lo:lo + FF_CHUNK, :])
    o_ref[...] = x + 0.5 * gt_ref[...] * _rms(acc, g2_ref[...])


def _mod_spec(mod, tm):
    if mod.shape[1] == 1:
        return pl.BlockSpec((None, 1, D_MODEL), lambda b, t: (b, 0, 0))
    return pl.BlockSpec((None, tm, D_MODEL), lambda b, t: (b, t, 0))


def _row_spec(tm, w):
    return pl.BlockSpec((None, tm, w), lambda b, t: (b, t, 0))


def _const_spec(shape):
    nd = len(shape)
    return pl.BlockSpec(shape, lambda b, t: (0,) * nd)


def ffn_half(x, shift, scale, gate, g1, g2, w_up, w_down, widx, tm):
    bsz, t_len, _ = x.shape
    return pl.pallas_call(
        _ffn_kernel,
        grid=(bsz, t_len // tm),
        in_specs=[_row_spec(tm, D_MODEL), _mod_spec(shift, tm), _mod_spec(scale, tm), _mod_spec(gate, tm),
                  _const_spec((1, D_MODEL)), _const_spec((1, D_MODEL)),
                  _stacked_spec((D_MODEL, 2 * D_FF), widx), _stacked_spec((D_FF, D_MODEL), widx)],
        out_specs=_row_spec(tm, D_MODEL),
        out_shape=jax.ShapeDtypeStruct(x.shape, F32),
        compiler_params=_params(("parallel", "parallel")),
        name="ffn_half",
    )(x, shift, scale, gate, g1, g2, w_up, w_down)


def _gelu_tanh(x):
    return 0.5 * x * (1.0 + jnp.tanh(np.sqrt(2.0 / np.pi).astype(np.float32) * (x + 0.044715 * (x * x * x))))


PROJ_IN = ("x", "shift", "scale", "g", "cos", "sin", "w", "ln_g", "ln_b")
PROJ_IN_T = ("w_t", "cos_t", "sin_t")
PROJ_OUT = (("qa", 2 * GROUP_W, BF), ("kv_cmp", LANES, F32), ("gate", LANES, F32), ("qb", GROUP_W, BF),
            ("u", GROUP_W, F32), ("v", GROUP_W, F32), ("pin", GROUP_W, F32))
PROJ_OUT_ROWS = (("kv_sel", LANES, F32), ("kv_win", LANES, F32), ("kvb", 2 * GROUP_W, F32))
PROJ_OUT_B16 = (("kv_sel_b", LANES, BF), ("kv_win_b", LANES, BF), ("kvb_b", 2 * GROUP_W, BF))
PROJ_OUT_T = (("kvt_cmp", LANES), ("kvt_sel", LANES), ("kvt_win", LANES), ("kvbt", 2 * GROUP_W))


def _proj_kernel(*refs, feature_major):
    names = PROJ_IN + (PROJ_IN_T if feature_major else ())
    names += tuple(n for n, _, _ in PROJ_OUT + (PROJ_OUT_B16 if feature_major else PROJ_OUT_ROWS))
    names += tuple(n for n, _ in PROJ_OUT_T) if feature_major else ()
    r = dict(zip(names, refs))
    x = r["x"][...]
    h = _rms(x, r["g"][...]) * (1.0 + r["scale"][...]) + r["shift"][...]
    hb = h.astype(BF)
    cos = r["cos"][...]
    sin = r["sin"][...]
    w_ref = r["w"]

    def mm(lo, hi):
        return _dot(hb, w_ref[:, lo:hi])

    for j in range(N_HEADS):
        p = mm(C_QA + LANES * j, C_QA + LANES * (j + 1))
        ps = mm(C_QAS + LANES * j, C_QAS + LANES * (j + 1))
        r["qa"][:, LANES * j:LANES * (j + 1)] = (p * cos + ps * sin).astype(BF)
    lane = lax.broadcasted_iota(jnp.int32, cos.shape, 1)
    ckv = jnp.where(lane < HEAD_DIM, cos, 1.0)
    for j, nm in enumerate(("kv_cmp", "kv_sel", "kv_win")):
        p = mm(C_KV + LANES * j, C_KV + LANES * (j + 1))
        ps = mm(C_KVS + LANES * j, C_KVS + LANES * (j + 1))
        kv = p * ckv + ps * sin
        if nm in r:
            r[nm][...] = kv
        if nm + "_b" in r:
            r[nm + "_b"][...] = kv.astype(BF)
    r["gate"][...] = jax.nn.sigmoid(mm(C_G, C_QB))
    r["qb"][...] = mm(C_QB, C_KVB).astype(BF)
    kvb = mm(C_KVB, C_UV)
    if feature_major:
        r["kvb_b"][...] = kvb.astype(BF)
    else:
        r["kvb"][...] = kvb
    r["u"][...] = _gelu_tanh(mm(C_UV, C_UV + GROUP_W))
    v = _gelu_tanh(mm(C_UV + GROUP_W, C_DIN))
    vc = v - jnp.mean(v, axis=-1, keepdims=True)
    vn = vc * lax.rsqrt(jnp.mean(vc * vc, axis=-1, keepdims=True) + EPS)
    r["v"][...] = vn * r["ln_g"][...] + r["ln_b"][...]
    r["pin"][...] = mm(C_DIN, C_END)
    if feature_major:
        wt_ref = r["w_t"]
        cos_t = r["cos_t"][...]
        sin_t = r["sin_t"][...]
        n_kv = 3 * LANES
        for j, nm in enumerate(("kvt_cmp", "kvt_sel", "kvt_win")):
            p = _dot_nt(wt_ref[LANES * j:LANES * (j + 1), :], hb)
            ps = _dot_nt(wt_ref[n_kv + LANES * j:n_kv + LANES * (j + 1), :], hb)
            r[nm][...] = p * cos_t + ps * sin_t
        r["kvbt"][...] = _dot_nt(wt_ref[2 * n_kv:2 * n_kv + 2 * GROUP_W, :], hb)


def in_proj(x, shift, scale, g, cos, sin, w_all, ln_g, ln_b, tm, t_side=None):
    bsz, t_len, _ = x.shape
    feature_major = t_side is not None
    tab_spec = pl.BlockSpec((tm, LANES), lambda b, t: (t, 0))
    in_specs = [_row_spec(tm, D_MODEL), _mod_spec(shift, tm), _mod_spec(scale, tm),
                _const_spec((1, D_MODEL)), tab_spec, tab_spec,
                _const_spec((D_MODEL, C_END)), _const_spec((1, GROUP_W)), _const_spec((1, GROUP_W))]
    args = [x, shift, scale, g, cos, sin, w_all, ln_g, ln_b]
    outs = PROJ_OUT + (PROJ_OUT_B16 if feature_major else PROJ_OUT_ROWS)
    out_specs = [_row_spec(tm, w) for _, w, _ in outs]
    out_shape = [jax.ShapeDtypeStruct((bsz, t_len, w), dt) for _, w, dt in outs]
    names = [n for n, _, _ in outs]
    if feature_major:
        tab_t_spec = pl.BlockSpec((LANES, tm), lambda b, t: (0, t))
        in_specs += [_const_spec(t_side[0].shape), tab_t_spec, tab_t_spec]
        args += list(t_side)
        out_specs += [pl.BlockSpec((None, w, tm), lambda b, t: (b, 0, t)) for _, w in PROJ_OUT_T]
        out_shape += [jax.ShapeDtypeStruct((bsz, w, t_len), F32) for _, w in PROJ_OUT_T]
        names += [n for n, _ in PROJ_OUT_T]
    res = pl.pallas_call(
        functools.partial(_proj_kernel, feature_major=feature_major),
        grid=(bsz, t_len // tm),
        in_specs=in_specs, out_specs=out_specs, out_shape=out_shape,
        compiler_params=_params(("parallel", "parallel")),
        name="in_proj",
    )(*args)
    return dict(zip(names, res))


def _outproj_kernel(x_ref, oa_ref, ob_ref, oc_ref, od_ref, gt_ref, g_ref, w_ref, o_ref):
    y = _dot(oa_ref[...], w_ref[0:GROUP_W, :])
    y = y + _dot(ob_ref[...], w_ref[GROUP_W:2 * GROUP_W, :])
    y = y + _dot(oc_ref[...], w_ref[2 * GROUP_W:3 * GROUP_W, :])
    y = y + _dot(od_ref[...], w_ref[3 * GROUP_W:4 * GROUP_W, :])
    o_ref[...] = x_ref[...] + gt_ref[...] * _rms(y, g_ref[...])


def out_proj(x, o_a, o_b, o_c, o_d, gate, g, w_out, layer, tm):
    bsz, t_len, _ = x.shape
    return pl.pallas_call(
        _outproj_kernel,
        grid=(bsz, t_len // tm),
        in_specs=[_row_spec(tm, D_MODEL)] + [_row_spec(tm, GROUP_W)] * 4 +
                 [_mod_spec(gate, tm), _const_spec((1, D_MODEL)), _stacked_spec((D_MODEL, D_MODEL), (layer,))],
        out_specs=_row_spec(tm, D_MODEL),
        out_shape=jax.ShapeDtypeStruct(x.shape, F32),
        compiler_params=_params(("parallel", "parallel")),
        name="out_proj",
    )(x, o_a, o_b, o_c, o_d, gate, g, w_out)


def _compress_rows(row_ref, n_grp, pe_ref, w_ref, base=0):
    lo_parts, hi_parts = [], []
    for l in range(CMP_STRIDE):
        a = row_ref[pl.ds(base + l, n_grp, stride=CMP_STRIDE), :]
        lo_parts.append((a + pe_ref[l:l + 1, :]).astype(BF))
        hi_parts.append((a + pe_ref[CMP_STRIDE + l:CMP_STRIDE + l + 1, :]).astype(BF))
    lo = _dot(jnp.concatenate(lo_parts, axis=1), w_ref[0])
    hi = _dot(jnp.concatenate(hi_parts, axis=1), w_ref[1])
    return lo, hi


def _compress_kernel(row_ref, pe_ref, w_ref, o_ref, hi_sc):
    n_grp = o_ref.shape[0]
    lo, hi = _compress_rows(row_ref, n_grp, pe_ref, w_ref)
    hi_sc[0:n_grp, :] = hi
    hi_sc[n_grp:n_grp + 8, :] = jnp.zeros((8, LANES), F32)
    o_ref[...] = (lo + hi_sc[pl.ds(1, n_grp), :]).astype(BF)


def compress_prompt(kv_cmp, pe, w_cmp):
    bsz, t_len, _ = kv_cmp.shape
    n_grp = t_len // CMP_STRIDE
    return pl.pallas_call(
        _compress_kernel,
        grid=(bsz,),
        in_specs=[pl.BlockSpec((None, t_len, LANES), lambda b: (b, 0, 0)),
                  pl.BlockSpec((CMP_LEN, LANES), lambda b: (0, 0)),
                  pl.BlockSpec((2, CMP_STRIDE * LANES, LANES), lambda b: (0, 0, 0))],
        out_specs=pl.BlockSpec((None, n_grp, LANES), lambda b: (b, 0, 0)),
        out_shape=jax.ShapeDtypeStruct((bsz, n_grp, LANES), BF),
        scratch_shapes=[pltpu.VMEM((n_grp + 8, LANES), F32)],
        compiler_params=_params(("parallel",)),
        name="nsa_compress",
    )(kv_cmp, pe, w_cmp)


NSA_Q = 256
NSA_KC = 512


def _softmax_bias(s, bias):
    sm = s + bias
    m = jnp.maximum(jnp.max(sm, axis=-1, keepdims=True), -0.5 * MASK_BIAS)
    return jnp.exp(sm - m)


def _softmax_rows(s, allow):
    sm = jnp.where(allow, s, NEG)
    m = jnp.max(sm, axis=-1, keepdims=True)
    e = jnp.where(allow, jnp.exp(sm - m), 0.0)
    return e / jnp.maximum(jnp.sum(e, axis=-1, keepdims=True), 1e-30)


def _topk_select(score, valid, n_pick, axis=0):
    n_blk = score.shape[axis]
    j_io = lax.broadcasted_iota(jnp.int32, score.shape, axis)
    sel = jnp.zeros(score.shape, F32)
    sc = score
    for _ in range(n_pick):
        m = jnp.max(sc, axis=axis, keepdims=True)
        idx = jnp.min(jnp.where(sc == m, j_io, n_blk), axis=axis, keepdims=True)
        pick = j_io == idx
        sel = jnp.where(pick, 1.0, sel)
        sc = jnp.where(pick, NEG, sc)
    return jnp.where(valid, sel, 0.0)


def _nsa_prompt_kernel(qa_ref, g_ref, kcmp_ref, ksel_ref, kwin_ref, mt_ref, e_ref, o_ref,
                       m_sc, acc_sc, sa_sc, sb_sc):
    i = pl.program_id(1)
    s0 = i * NSA_Q
    n_cmp = kcmp_ref.shape[0]
    n_sel = mt_ref.shape[0]
    qs = jnp.concatenate([qa_ref[:, LANES * h:LANES * (h + 1)] for h in range(N_HEADS)], axis=0)
    qpos = s0 + lax.broadcasted_iota(jnp.int32, (NSA_Q, 1), 0)

    def heads(x):
        return jnp.concatenate([x] * N_HEADS, axis=0)

    kc = kcmp_ref[...]
    blk_end = lax.broadcasted_iota(jnp.int32, (1, n_cmp), 1) * CMP_STRIDE + (CMP_LEN - 1)
    e = _softmax_bias(_dot_nt(qs, kc), heads(jnp.where(blk_end <= qpos, 0.0, -MASK_BIAS)))
    p = e * (1.0 / jnp.maximum(jnp.sum(e, axis=-1, keepdims=True), 1e-30))
    o_cmp = _dot(p.astype(BF), kc)
    ps = p[0:NSA_Q] + p[NSA_Q:2 * NSA_Q] + p[2 * NSA_Q:3 * NSA_Q] + p[3 * NSA_Q:4 * NSA_Q]
    mt = mt_ref[...]
    p_slc_t = sum(_dot_nt(mt, part) for part in _split3(ps))
    j_io = lax.broadcasted_iota(jnp.int32, (n_sel, NSA_Q), 0)
    cur = (s0 + lax.broadcasted_iota(jnp.int32, (1, NSA_Q), 1)) // SEL_BLOCK
    forced = (j_io == 0) | (j_io == cur) | (j_io == cur - 1)
    valid = j_io <= cur
    score_t = jnp.where(valid, jnp.where(forced, FORCE_SCORE, p_slc_t), NEG)
    sel = _topk_select(score_t, valid, min(N_TOPK, n_sel)).T.astype(BF)

    m_sc[...] = jnp.full(m_sc.shape, NEG, F32)
    acc_sc[...] = jnp.zeros(acc_sc.shape, F32)
    rep = NSA_KC // LANES
    key_lane = lax.broadcasted_iota(jnp.int32, (1, LANES), 1) < HEAD_DIM
    last = (s0 + NSA_Q - 1) // NSA_KC

    def chunk_start(c):
        return pl.multiple_of(jnp.minimum(c, last) * NSA_KC, NSA_KC)

    def scores(c, dst):
        dst[...] = _dot_nt(qs, ksel_ref[pl.ds(chunk_start(c), NSA_KC), :])

    def update(c, src):
        k0 = chunk_start(c)
        kv = ksel_ref[pl.ds(k0, NSA_KC), :]
        kv1 = jnp.where(key_lane, jnp.ones_like(kv), kv)
        selx = _dot(sel, e_ref[:, pl.ds(k0, NSA_KC)])
        tok = k0 + lax.broadcasted_iota(jnp.int32, (1, NSA_KC), 1)
        live = jnp.where(c <= last, 0.0, -MASK_BIAS)
        sm = src[...] + heads(jnp.where((selx > 0.5) & (tok <= qpos), live, -MASK_BIAS))
        m_old = m_sc[...]
        m_new = jnp.maximum(m_old, jnp.max(sm, axis=-1, keepdims=True))
        pe = jnp.exp(sm - jnp.concatenate([m_new] * rep, axis=1))
        acc_sc[...] = jnp.exp(m_old - m_new) * acc_sc[...] + _dot(pe.astype(BF), kv1)
        m_sc[...] = m_new

    scores(0, sa_sc)

    def chunk_pair(j, carry):
        scores(2 * j + 1, sb_sc)
        update(2 * j, sa_sc)
        scores(2 * j + 2, sa_sc)
        update(2 * j + 1, sb_sc)
        return carry

    lax.fori_loop(0, last // 2 + 1, chunk_pair, 0)

    n_win = WINDOW + NSA_Q
    w0 = pl.multiple_of(jnp.maximum(s0 - WINDOW, 0), NSA_Q)
    kvw = kwin_ref[pl.ds(w0, n_win), :]
    dist = qpos - (w0 + lax.broadcasted_iota(jnp.int32, (1, n_win), 1))
    ew = _softmax_bias(_dot_nt(qs, kvw), heads(jnp.where((dist >= 0) & (dist <= WINDOW), 0.0, -MASK_BIAS)))
    o_win = _dot(ew.astype(BF), jnp.where(key_lane, jnp.ones_like(kvw), kvw))

    g = g_ref[...]
    outs = []
    for h in range(N_HEADS):
        rows = slice(h * NSA_Q, (h + 1) * NSA_Q)
        acc = acc_sc[rows, :]
        win = o_win[rows]
        o = (g[:, 3 * h:3 * h + 1] * o_cmp[rows]
             + g[:, 3 * h + 1:3 * h + 2] * (acc / jnp.maximum(acc[:, 0:1], 1e-30))
             + g[:, 3 * h + 2:3 * h + 3] * (win / jnp.maximum(win[:, 0:1], 1e-30)))
        outs.append(o[:, HEAD_DIM:])
    o_ref[...] = jnp.concatenate(outs, axis=1).astype(BF)


def nsa_prompt(qa, gate, kcmp, ksel_b, kwin_b, mt, e_mat):
    bsz, t_len, _ = qa.shape
    n_cmp = kcmp.shape[1]
    n_sel = mt.shape[0]
    return pl.pallas_call(
        _nsa_prompt_kernel,
        grid=(bsz, t_len // NSA_Q),
        in_specs=[_row_spec(NSA_Q, 2 * GROUP_W), _row_spec(NSA_Q, LANES),
                  pl.BlockSpec((None, n_cmp, LANES), lambda b, t: (b, 0, 0)),
                  pl.BlockSpec((None, t_len, LANES), lambda b, t: (b, 0, 0)),
                  pl.BlockSpec((None, t_len, LANES), lambda b, t: (b, 0, 0)),
                  _const_spec((n_sel, n_cmp)), _const_spec((n_sel, t_len))],
        out_specs=_row_spec(NSA_Q, GROUP_W),
        out_shape=jax.ShapeDtypeStruct((bsz, t_len, GROUP_W), BF),
        scratch_shapes=[pltpu.VMEM((N_HEADS * NSA_Q, LANES), F32), pltpu.VMEM((N_HEADS * NSA_Q, LANES), F32),
                        pltpu.VMEM((N_HEADS * NSA_Q, NSA_KC), F32), pltpu.VMEM((N_HEADS * NSA_Q, NSA_KC), F32)],
        compiler_params=_params(("parallel", "arbitrary")),
        name="nsa_prompt",
    )(qa, gate, kcmp, ksel_b, kwin_b, mt, e_mat)


SB_Q = 256
SB_K = 256


SB_STOP = -110.0


def _softplus(z):
    return jnp.maximum(z, 0.0) + jnp.log(1.0 + jnp.exp(-jnp.abs(z)))


def _tri_neg(n):
    r = lax.broadcasted_iota(jnp.int32, (2 * n, n + LANES), 0)
    c = lax.broadcasted_iota(jnp.int32, (2 * n, n + LANES), 1)
    r = jnp.where(r >= n, r - n, r)
    return jnp.where((r > c) | (c >= n), -1.0, 0.0).astype(BF)


def _sb_weights(z, tri, carry, mask):
    n_k = z.shape[1]
    sp = _softplus(z)
    spm = sp if mask is None else jnp.where(mask, sp, 0.0)
    hi = spm.astype(BF)
    lo = (spm - hi.astype(F32)).astype(BF)
    cs = _dot(jnp.concatenate([hi, lo], axis=1), tri)
    after = cs[:, :n_k] + jnp.concatenate([carry] * (n_k // LANES), axis=1)
    a = jnp.exp(z - sp + after)
    if mask is not None:
        a = jnp.where(mask, a, 0.0)
    return a.astype(BF), carry + cs[:, n_k:]


def _sb_prompt_kernel(qb_ref, kvb_ref, o_ref, acc_sc, car_sc):
    i = pl.program_id(1)
    q = qb_ref[...]
    lane = lax.broadcasted_iota(jnp.int32, (1, GROUP_W), 1)
    qh = [jnp.where(lane // HEAD_DIM == h, q, jnp.zeros_like(q)) for h in range(N_HEADS)]
    tri = _tri_neg(SB_K)
    r = lax.broadcasted_iota(jnp.int32, (SB_Q, SB_K), 0)
    c = lax.broadcasted_iota(jnp.int32, (SB_Q, SB_K), 1)
    diag_mask = c < r

    def chunk(k0, mask, first):
        k = kvb_ref[pl.ds(k0, SB_K), 0:GROUP_W]
        v = kvb_ref[pl.ds(k0, SB_K), GROUP_W:2 * GROUP_W]
        car_max = None
        for h in range(N_HEADS):
            car = jnp.zeros((SB_Q, LANES), F32) if first else car_sc[h]
            a, car = _sb_weights(_dot_nt(qh[h], k), tri, car, mask)
            pv = _dot(a, v)
            acc_sc[h] = pv if first else acc_sc[h] + pv
            car_sc[h] = car
            car_max = car if car_max is None else jnp.maximum(car_max, car)
        return jnp.max(car_max)

    car_max = chunk(pl.multiple_of(i * SB_Q, SB_Q), diag_mask, True)

    def more(st):
        j, car_max = st
        return (j < i) & (car_max > SB_STOP)

    def body(st):
        j, _ = st
        return j + 1, chunk(pl.multiple_of((i - 1 - j) * SB_K, SB_K), None, False)

    lax.while_loop(more, body, (0, car_max))
    o = jnp.zeros((SB_Q, GROUP_W), F32)
    for h in range(N_HEADS):
        o = jnp.where(lane // HEAD_DIM == h, acc_sc[h], o)
    o_ref[...] = o.astype(BF)


def sb_prompt(qb, kvb_b):
    bsz, t_len, _ = qb.shape
    return pl.pallas_call(
        _sb_prompt_kernel,
        grid=(bsz, t_len // SB_Q),
        in_specs=[_row_spec(SB_Q, GROUP_W),
                  pl.BlockSpec((None, t_len, 2 * GROUP_W), lambda b, t: (b, 0, 0))],
        out_specs=_row_spec(SB_Q, GROUP_W),
        out_shape=jax.ShapeDtypeStruct((bsz, t_len, GROUP_W), BF),
        scratch_shapes=[pltpu.VMEM((N_HEADS, SB_Q, GROUP_W), F32),
                        pltpu.VMEM((N_HEADS, SB_Q, LANES), F32)],
        compiler_params=_params(("parallel", "arbitrary")),
        name="sb_prompt",
    )(qb, kvb_b)


def _pool_mix(ext_ref, tm, tpos, pool_w_ref, pool_s_ref):
    def shifted(ref, k):
        return ref[pl.ds(POOL_MAX - k, tm), :]
    x = shifted(ext_ref, 0)
    lane = lax.broadcasted_iota(jnp.int32, (1, GROUP_W), 1)
    grp = lane // HEAD_DIM
    s2 = x + shifted(ext_ref, 1)
    s4 = s2 + shifted(ext_ref, 2) + shifted(ext_ref, 3)
    s8 = s4 + sum(shifted(ext_ref, k) for k in range(4, 8))
    s16 = s8 + sum(shifted(ext_ref, k) for k in range(8, 16))
    tot = jnp.where(grp == 0, s2, jnp.where(grp == 1, s4, jnp.where(grp == 2, s8, s16)))
    wlen = jnp.where(grp == 0, 2, jnp.where(grp == 1, 4, jnp.where(grp == 2, 8, 16)))
    cnt = jnp.minimum(wlen, tpos + 1).astype(F32)
    d = tot / cnt - x
    return _dot(d.astype(BF), pool_w_ref[...]) * pool_s_ref[...]


def _gmlp_pool_kernel(u_ref, v_ref, pin_ref, halo_ref, sgw_ref, sgb_ref, pw_ref, ps_ref,
                      oc_ref, od_ref, ext_sc):
    t = pl.program_id(1)
    tm = u_ref.shape[0]
    lane = lax.broadcasted_iota(jnp.int32, (1, GROUP_W), 1)
    for c in range(tm // CHUNK):
        rows = slice(c * CHUNK, (c + 1) * CHUNK)
        v = v_ref[rows, :]
        vz = jnp.zeros_like(v)
        vst = jnp.concatenate([jnp.where(lane // HEAD_DIM == h, v, vz) for h in range(N_HEADS)], axis=0)
        s = _dot(sgw_ref[...], vst.astype(BF)) + sgb_ref[...]
        oc_ref[rows, :] = (u_ref[rows, :] * s).astype(BF)
    halo = halo_ref[...]
    ext_sc[0:POOL_MAX, :] = jnp.where(t > 0, halo, jnp.zeros_like(halo))
    ext_sc[POOL_MAX:POOL_MAX + tm, :] = pin_ref[...]
    tpos = t * tm + lax.broadcasted_iota(jnp.int32, (tm, 1), 0)
    od_ref[...] = _pool_mix(ext_sc, tm, tpos, pw_ref, ps_ref).astype(BF)


def gmlp_pool_prompt(u, v, pin, sgw_cat, sgb_full, pool_w_bd, pool_scale, tm):
    bsz, t_len, _ = u.shape
    per = tm // POOL_MAX
    halo_spec = pl.BlockSpec((None, POOL_MAX, GROUP_W), lambda b, t: (b, jnp.maximum(t * per - 1, 0), 0))
    return pl.pallas_call(
        _gmlp_pool_kernel,
        grid=(bsz, t_len // tm),
        in_specs=[_row_spec(tm, GROUP_W), _row_spec(tm, GROUP_W), _row_spec(tm, GROUP_W), halo_spec,
                  _const_spec((CHUNK, N_HEADS * CHUNK)), _const_spec((CHUNK, GROUP_W)),
                  _const_spec((GROUP_W, GROUP_W)), _const_spec((1, GROUP_W))],
        out_specs=[_row_spec(tm, GROUP_W), _row_spec(tm, GROUP_W)],
        out_shape=[jax.ShapeDtypeStruct((bsz, t_len, GROUP_W), BF)] * 2,
        scratch_shapes=[pltpu.VMEM((POOL_MAX + tm, GROUP_W), F32)],
        compiler_params=_params(("parallel", "parallel")),
        name="gmlp_pool",
    )(u, v, pin, pin, sgw_cat, sgb_full, pool_w_bd, pool_scale)


def _swap_neg(w):
    half = HEAD_DIM // 2
    w = w.reshape(w.shape[0], -1, 2, half)
    return jnp.stack([-w[:, :, 1], w[:, :, 0]], axis=2).reshape(w.shape[0], -1)


def _pad_heads(w):
    w = w.reshape(w.shape[0], -1, HEAD_DIM)
    return jnp.concatenate([w, jnp.zeros_like(w)], axis=2).reshape(w.shape[0], -1)


def _proj_weights(w_in):
    a_q, a_kv, a_g, b_qkv, c_uv, d_in = jnp.split(
        w_in, np.cumsum([GROUP_W, 6 * HEAD_DIM, 3 * N_HEADS, 3 * GROUP_W, 2 * GROUP_W]).tolist(), axis=1)
    scale = HEAD_DIM ** -0.5
    a_q = a_q * scale
    kv = a_kv.reshape(-1, 3, 2, HEAD_DIM)
    kv_sw = jnp.concatenate([_swap_neg(kv[:, :, 0].reshape(-1, 3 * HEAD_DIM)).reshape(-1, 3, 1, HEAD_DIM),
                             jnp.zeros_like(kv[:, :, 1:2])], axis=2).reshape(-1, 6 * HEAD_DIM)
    g_pad = jnp.pad(a_g, ((0, 0), (0, LANES - 3 * N_HEADS)))
    b_q = b_qkv[:, :GROUP_W] * scale
    w_all = jnp.concatenate([_pad_heads(a_q), _pad_heads(_swap_neg(a_q)), a_kv, kv_sw, g_pad,
                             b_q, b_qkv[:, GROUP_W:], c_uv, d_in], axis=1)
    w_t = jnp.concatenate([a_kv, kv_sw, b_qkv[:, GROUP_W:]], axis=1).T
    return w_all.astype(BF), w_t.astype(BF)


def _rope_tables(pos):
    half = HEAD_DIM // 2
    inv = ROPE_THETA ** (-jnp.arange(half, dtype=F32) / half)
    ang = pos.astype(F32)[:, None] * inv[None, :]
    cos, sin = jnp.cos(ang), jnp.sin(ang)
    cos_t = jnp.concatenate([cos.T, cos.T, jnp.ones((HEAD_DIM, pos.shape[0]), F32)], axis=0)
    sin_t = jnp.concatenate([sin.T, sin.T, jnp.zeros((HEAD_DIM, pos.shape[0]), F32)], axis=0)
    return jnp.tile(cos, (1, 4)), jnp.tile(sin, (1, 4)), cos_t, sin_t


def _cmp_weights(cmp_w):
    w = jnp.zeros((CMP_LEN, 2, HEAD_DIM, 2, HEAD_DIM), F32)
    w = w.at[:, 0, :, 0, :].set(cmp_w[0]).at[:, 1, :, 1, :].set(cmp_w[1])
    return w.reshape(2, CMP_STRIDE * LANES, LANES).astype(BF)


def _sel_constants(n_cmp, n_sel, t_len):
    n = np.arange(n_cmp)[None, :]
    j = np.arange(n_sel)[:, None]
    mt = ((n >= 4 * j - 1) & (n <= 4 * j + 3)).astype(np.float32) + ((n >= 4 * j) & (n <= 4 * j + 2))
    e = (np.arange(t_len)[None, :] // SEL_BLOCK == j).astype(np.float32)
    return jnp.asarray(mt, BF), jnp.asarray(e, BF)


def _gmlp_weights(sg_w, sg_b):
    wm = sg_w * jnp.tril(jnp.ones((CHUNK, CHUNK), sg_w.dtype))
    w_cat = jnp.transpose(wm, (1, 0, 2)).reshape(CHUNK, N_HEADS * CHUNK).astype(BF)
    b_full = jnp.repeat(sg_b.T, HEAD_DIM, axis=1)
    return w_cat, b_full


def _pool_weights(pool_w):
    w = jnp.zeros((N_HEADS, HEAD_DIM, N_HEADS, HEAD_DIM), F32)
    for g in range(N_HEADS):
        w = w.at[g, :, g, :].set(pool_w[g])
    return w.reshape(GROUP_W, GROUP_W).astype(BF)


def _sample_sel_weights(n_cmp_pad, n_sel_pad, n_cmp, n_sel):
    n = np.arange(n_cmp_pad)[:, None]
    j = np.arange(n_sel_pad)[None, :]
    ms = ((n >= 4 * j - 1) & (n <= 4 * j + 3)).astype(np.float32) + ((n >= 4 * j) & (n <= 4 * j + 2))
    ms = ms * ((n < n_cmp) & (j < n_sel))
    return jnp.asarray(ms, BF)


def _layer_weights(l, w_ada, b_ada, norm_g, w_ffn_up, w_ffn_down, w_in, w_out,
                   cmp_pe, cmp_w, sg_ln_g, sg_ln_b, sg_w, sg_b, pool_w, pool_scale):
    sgw_cat, sgb_full = _gmlp_weights(sg_w[l], sg_b[l])
    w_proj, w_proj_t = _proj_weights(w_in[l])
    return dict(
        layer=l, w_ada=w_ada, b_ada=b_ada[l][None, :], norm_g=norm_g[l][:, None, :],
        w_up=w_ffn_up, w_down=w_ffn_down, w_in=w_proj, w_in_t=w_proj_t, w_out=w_out,
        cmp_pe=cmp_pe[l].reshape(CMP_LEN, LANES), cmp_w=_cmp_weights(cmp_w[l]),
        ln_g=sg_ln_g[l][None, :], ln_b=sg_ln_b[l][None, :], sgw_cat=sgw_cat, sgb_full=sgb_full,
        sg_w00=jnp.repeat(sg_w[l][:, 0, 0], HEAD_DIM)[None, :], sg_b0=jnp.repeat(sg_b[l][:, 0], HEAD_DIM)[None, :],
        pool_w=_pool_weights(pool_w[l]), pool_scale=pool_scale[l][None, :])


def _mods(mod, shape):
    m = mod.reshape(mod.shape[0], N_MOD, D_MODEL)
    return [m[:, k].reshape(shape) for k in range(N_MOD)]


def prompt_layer(x, mod, lw, tables, consts, tm):
    bsz, t_len, _ = x.shape
    m = _mods(mod, (bsz, 1, D_MODEL))
    g = lw["norm_g"]
    cos, sin, cos_t, sin_t = tables
    l = lw["layer"]
    x = ffn_half(x, m[0], m[1], m[2], g[0], g[1], lw["w_up"], lw["w_down"], (l, 0), tm)
    pr = in_proj(x, m[3], m[4], g[2], cos, sin, lw["w_in"], lw["ln_g"], lw["ln_b"], tm,
                 t_side=(lw["w_in_t"], cos_t, sin_t))
    kcmp = compress_prompt(pr["kv_cmp"], lw["cmp_pe"], lw["cmp_w"])
    o_a = nsa_prompt(pr["qa"], pr["gate"], kcmp, pr["kv_sel_b"], pr["kv_win_b"], *consts)
    o_b = sb_prompt(pr["qb"], pr["kvb_b"])
    o_c, o_d = gmlp_pool_prompt(pr["u"], pr["v"], pr["pin"], lw["sgw_cat"], lw["sgb_full"],
                                lw["pool_w"], lw["pool_scale"], tm)
    x = out_proj(x, o_a, o_b, o_c, o_d, m[5], g[3], lw["w_out"], l, tm)
    x = ffn_half(x, m[6], m[7], m[8], g[4], g[5], lw["w_up"], lw["w_down"], (l, 1), tm)
    n_win = min(WINDOW, t_len)
    state = (pr["kvt_cmp"], pr["kvt_sel"], pr["kvt_win"][:, :, t_len - n_win:], pr["kvbt"],
             pr["pin"][:, t_len - POOL_BUF:])
    return x, state


def _page_copies(cache_ref, layer, pt_ref, b, first_page, n_pages, buf_ref, slot, sem_ref):
    return [pltpu.make_async_copy(cache_ref.at[layer, pt_ref[b, first_page + p]], buf_ref.at[slot, p],
                                  sem_ref.at[slot]) for p in range(n_pages)]


SB_GROUP = 4


def _sb_sample_kernel(pt_ref, q_ref, cache_ref, o_ref, buf0, buf, sem0, sem, *, layer, n_pages):
    b = pl.program_id(0)
    n_grp = n_pages // SB_GROUP

    def copies(seq, grp, dst, slot, sm):
        return _page_copies(cache_ref, layer, pt_ref, seq, (n_grp - 1 - grp) * SB_GROUP, SB_GROUP, dst, slot, sm)

    @pl.when(b == 0)
    def _():
        for c in copies(0, 0, buf0, 0, sem0):
            c.start()

    @pl.when(b + 1 < pl.num_programs(0))
    def _():
        for c in copies(b + 1, 0, buf0, (b + 1) % 2, sem0):
            c.start()

    if n_grp > 1:
        for c in copies(b, 1, buf, 1, sem):
            c.start()

    row = lax.broadcasted_iota(jnp.int32, (8, GROUP_W), 0)
    lane = lax.broadcasted_iota(jnp.int32, (8, GROUP_W), 1)
    head_lanes = lane // HEAD_DIM == row
    q = jnp.broadcast_to(q_ref[...].astype(F32), (8, GROUP_W))
    qm = jnp.where(head_lanes, q, 0.0).astype(BF)
    tri = _tri_neg(PAGE)

    def sweep(src, slot, acc, car):
        for p in reversed(range(SB_GROUP)):
            kt = src[slot, p, 0:GROUP_W, :].astype(BF)
            vt = src[slot, p, GROUP_W:2 * GROUP_W, :].astype(BF)
            a, car = _sb_weights(_dot(qm, kt), tri, car, None)
            acc = acc + _dot_nt(a, vt)
        return acc, car

    for c in copies(b, 0, buf0, b % 2, sem0):
        c.wait()
    acc, car = sweep(buf0, b % 2, jnp.zeros((8, GROUP_W), F32), jnp.zeros((8, LANES), F32))

    def more(st):
        g, car_max, _, _ = st
        return (g < n_grp) & (car_max > SB_STOP)

    def body(st):
        g, _, acc, car = st
        slot = g % 2

        @pl.when(g + 1 < n_grp)
        def _():
            for c in copies(b, g + 1, buf, 1 - slot, sem):
                c.start()

        for c in copies(b, g, buf, slot, sem):
            c.wait()
        acc, car = sweep(buf, slot, acc, car)
        return g + 1, jnp.max(car), acc, car

    g, _, acc, _ = lax.while_loop(more, body, (1, jnp.max(car), acc, car))

    @pl.when(g < n_grp)
    def _():
        for c in copies(b, g, buf, g % 2, sem):
            c.wait()

    o_ref[...] = jnp.sum(jnp.where(head_lanes, acc, 0.0), axis=0, keepdims=True)


def sb_sample(page_table, qb, cache_t, layer):
    n_seq = qb.shape[0]
    n_pages = page_table.shape[1]
    grp_buf = pltpu.VMEM((2, SB_GROUP, 2 * GROUP_W, PAGE), F32)
    return pl.pallas_call(
        functools.partial(_sb_sample_kernel, layer=layer, n_pages=n_pages),
        grid_spec=pltpu.PrefetchScalarGridSpec(
            num_scalar_prefetch=1, grid=(n_seq,),
            in_specs=[pl.BlockSpec((None, 1, GROUP_W), lambda b, pt: (b, 0, 0)),
                      pl.BlockSpec(memory_space=pl.ANY)],
            out_specs=pl.BlockSpec((None, 1, GROUP_W), lambda b, pt: (b, 0, 0)),
            scratch_shapes=[grp_buf, grp_buf, pltpu.SemaphoreType.DMA((2,)), pltpu.SemaphoreType.DMA((2,))]),
        out_shape=jax.ShapeDtypeStruct((n_seq, 1, GROUP_W), F32),
        compiler_params=_params(("arbitrary",)),
        name="sb_sample",
    )(page_table, qb, cache_t)


def _slab_copies(cache_ref, layer, pt_ref, b, n_pages, buf_ref, slot, sem_ref):
    return [pltpu.make_async_copy(cache_ref.at[layer, pt_ref[b, p]],
                                  buf_ref.at[slot, :, pl.ds(p * PAGE, PAGE)], sem_ref.at[slot])
            for p in range(n_pages)]


CMP_CHUNK = 256
TAIL_ROWS = 128
XPOSE_TOKENS = 1024
SLAB_CAST = 2048


def _nsa_cmp_sample_kernel(pt_ref, q_ref, new_ref, pe_ref, w_ref, ms_ref, cache_ref, ocmp_ref, sel_ref,
                           buf, sem, rows_sc, lo_sc, hi_sc, *, layer, n_pages):
    b = pl.program_id(0)
    slot = b % 2

    def copies(seq, sl):
        return _slab_copies(cache_ref, layer, pt_ref, seq, n_pages, buf, sl, sem)

    @pl.when(b == 0)
    def _():
        for c in copies(0, 0):
            c.start()

    @pl.when(b + 1 < pl.num_programs(0))
    def _():
        for c in copies(b + 1, 1 - slot):
            c.start()

    for c in copies(b, slot):
        c.wait()

    past = n_pages * PAGE
    n_pad = lo_sc.shape[0]
    for c in range(past // XPOSE_TOKENS):
        span = slice(c * XPOSE_TOKENS, (c + 1) * XPOSE_TOKENS)
        rows_sc[span, :] = buf[slot, :, span].T
    r_io = lax.broadcasted_iota(jnp.int32, (TAIL_ROWS, LANES), 0)
    rows_sc[past:past + TAIL_ROWS, :] = jnp.where(r_io == 0, jnp.broadcast_to(new_ref[...], (TAIL_ROWS, LANES)), 0.0)

    chunk = math.gcd(CMP_CHUNK, past // CMP_STRIDE)

    def cmp_chunk(c, carry):
        g0 = pl.multiple_of(c * chunk, chunk)
        lo, hi = _compress_rows(rows_sc, chunk, pe_ref, w_ref, base=g0 * CMP_STRIDE)
        lo_sc[pl.ds(g0, chunk), :] = lo
        hi_sc[pl.ds(g0, chunk), :] = hi
        return carry

    lax.fori_loop(0, past // (CMP_STRIDE * chunk), cmp_chunk, 0)
    n_tail = TAIL_ROWS // CMP_STRIDE
    g_tail = past // CMP_STRIDE
    lo, hi = _compress_rows(rows_sc, n_tail, pe_ref, w_ref, base=past)
    lo_sc[g_tail:g_tail + n_tail, :] = lo
    hi_sc[g_tail:g_tail + n_tail, :] = hi
    lo_sc[g_tail + n_tail:n_pad, :] = jnp.zeros((n_pad - g_tail - n_tail, LANES), F32)
    hi_sc[g_tail + n_tail:n_pad + 8, :] = jnp.zeros((n_pad + 8 - g_tail - n_tail, LANES), F32)
    kc = (lo_sc[...] + hi_sc[pl.ds(1, n_pad), :]).astype(BF)

    q8 = q_ref[...]
    qpos = past
    s = _dot_nt(q8, kc)
    blk_end = lax.broadcasted_iota(jnp.int32, (1, n_pad), 1) * CMP_STRIDE + (CMP_LEN - 1)
    p = _softmax_rows(s, blk_end <= qpos)
    ocmp_ref[...] = _dot(p.astype(BF), kc)
    ps = jnp.broadcast_to(jnp.sum(p[0:N_HEADS], axis=0, keepdims=True), (8, n_pad))
    ms = ms_ref[...]
    p_slc = sum(_dot(part, ms) for part in _split3(ps))
    j_io = lax.broadcasted_iota(jnp.int32, p_slc.shape, 1)
    cur = qpos // SEL_BLOCK
    forced = (j_io == 0) | (j_io == cur) | (j_io == cur - 1)
    valid = j_io <= cur
    score = jnp.where(valid, jnp.where(forced, FORCE_SCORE, p_slc), NEG)
    sel_ref[...] = _topk_select(score, valid, N_TOPK, axis=1)


def nsa_cmp_sample(page_table, q8, new_cmp, pe, w_cmp, ms, cache_t, layer):
    n_seq = q8.shape[0]
    n_pages = page_table.shape[1]
    n_pad, n_sel_pad = ms.shape
    seq_spec = lambda w: pl.BlockSpec((None, 8, w), lambda b, pt: (b, 0, 0))
    const2 = lambda shape: pl.BlockSpec(shape, lambda b, pt: (0,) * len(shape))
    return pl.pallas_call(
        functools.partial(_nsa_cmp_sample_kernel, layer=layer, n_pages=n_pages),
        grid_spec=pltpu.PrefetchScalarGridSpec(
            num_scalar_prefetch=1, grid=(n_seq,),
            in_specs=[seq_spec(LANES), pl.BlockSpec((None, 1, LANES), lambda b, pt: (b, 0, 0)),
                      const2((CMP_LEN, LANES)), const2((2, CMP_STRIDE * LANES, LANES)), const2((n_pad, n_sel_pad)),
                      pl.BlockSpec(memory_space=pl.ANY)],
            out_specs=[seq_spec(LANES), seq_spec(n_sel_pad)],
            scratch_shapes=[pltpu.VMEM((2, LANES, n_pages * PAGE), F32), pltpu.SemaphoreType.DMA((2,)),
                            pltpu.VMEM((n_pages * PAGE + TAIL_ROWS, LANES), F32),
                            pltpu.VMEM((n_pad, LANES), F32), pltpu.VMEM((n_pad + 8, LANES), F32)]),
        out_shape=[jax.ShapeDtypeStruct((n_seq, 8, LANES), F32), jax.ShapeDtypeStruct((n_seq, 8, n_sel_pad), F32)],
        compiler_params=_params(("arbitrary",)),
        name="nsa_cmp_sample",
    )(page_table, q8, new_cmp, pe, w_cmp, ms, cache_t)


def _nsa_sel_sample_kernel(pt_ref, q_ref, sel_ref, ocmp_ref, gate_ref, news_ref, neww_ref, e_ref, win_ref,
                           cache_ref, o_ref, buf, sem, kv_sc, *, layer, n_pages):
    b = pl.program_id(0)
    slot = b % 2

    def copies(seq, sl):
        return _slab_copies(cache_ref, layer, pt_ref, seq, n_pages, buf, sl, sem)

    @pl.when(b == 0)
    def _():
        for c in copies(0, 0):
            c.start()

    @pl.when(b + 1 < pl.num_programs(0))
    def _():
        for c in copies(b + 1, 1 - slot):
            c.start()

    for c in copies(b, slot):
        c.wait()

    past = n_pages * PAGE
    qpos = past
    q8 = q_ref[...]
    q8f = q8.astype(F32)
    sel = sel_ref[...]
    cast = math.gcd(SLAB_CAST, past)
    for c in range(past // cast):
        span = slice(c * cast, (c + 1) * cast)
        kv_sc[:, span] = buf[slot, :, span].astype(BF)
    kvt = kv_sc[...]

    def new_token(new_ref, allowed):
        kn = new_ref[...].astype(BF).astype(F32)
        s_new = jnp.sum(q8f * kn, axis=-1, keepdims=True)
        return kn, jnp.where(allowed, s_new, NEG)

    blk = e_ref.shape[0]
    sel_b = sel.astype(BF)
    selx = jnp.concatenate([_dot(sel_b[:, c * blk:(c + 1) * blk], e_ref[...])
                            for c in range(past // (blk * SEL_BLOCK))], axis=1)
    allow = (selx > 0.5) & (lax.broadcasted_iota(jnp.int32, (1, past), 1) <= qpos)
    sm = jnp.where(allow, _dot(q8, kvt), NEG)
    kn_s, s_new = new_token(news_ref, (sel[:, qpos // SEL_BLOCK:qpos // SEL_BLOCK + 1] > 0.5))
    m = jnp.maximum(jnp.max(sm, axis=-1, keepdims=True), s_new)
    e = jnp.where(allow, jnp.exp(sm - m), 0.0)
    e_new = jnp.where(s_new > 0.5 * NEG, jnp.exp(s_new - m), 0.0)
    l_sel = jnp.sum(e, axis=-1, keepdims=True) + e_new
    o_sel = (_dot_nt(e.astype(BF), kvt) + e_new * kn_s) / jnp.maximum(l_sel, 1e-30)

    n_buf = win_ref.shape[1]
    wb = win_ref[...].astype(BF)
    kwpos = past - n_buf + lax.broadcasted_iota(jnp.int32, (1, n_buf), 1)
    dist = qpos - kwpos
    allow_w = (dist >= 0) & (dist <= WINDOW) & (kwpos >= 0)
    sw = jnp.where(allow_w, _dot(q8, wb), NEG)
    kn_w, sw_new = new_token(neww_ref, True)
    mw = jnp.maximum(jnp.max(sw, axis=-1, keepdims=True), sw_new)
    ew = jnp.where(allow_w, jnp.exp(sw - mw), 0.0)
    ew_new = jnp.exp(sw_new - mw)
    l_w = jnp.sum(ew, axis=-1, keepdims=True) + ew_new
    o_win = (_dot_nt(ew.astype(BF), wb) + ew_new * kn_w) / jnp.maximum(l_w, 1e-30)

    g = jnp.broadcast_to(gate_ref[...], (8, LANES))
    g_row = lax.broadcasted_iota(jnp.int32, (8, LANES), 0)
    g_lane = lax.broadcasted_iota(jnp.int32, (8, LANES), 1)
    gk = [jnp.sum(jnp.where(g_lane == 3 * g_row + k, g, 0.0), axis=-1, keepdims=True) for k in range(3)]
    o_ref[...] = gk[0] * ocmp_ref[...] + gk[1] * o_sel + gk[2] * o_win


def nsa_sel_sample(page_table, q8, sel, o_cmp, gate, new_sel, new_win, e_mat, win_t, cache_t, layer):
    n_seq = q8.shape[0]
    n_pages = page_table.shape[1]
    n_buf = win_t.shape[3]
    seq_spec = lambda r, w: pl.BlockSpec((None, r, w), lambda b, pt: (b, 0, 0))
    return pl.pallas_call(
        functools.partial(_nsa_sel_sample_kernel, layer=layer, n_pages=n_pages),
        grid_spec=pltpu.PrefetchScalarGridSpec(
            num_scalar_prefetch=1, grid=(n_seq,),
            in_specs=[seq_spec(8, LANES), seq_spec(8, sel.shape[2]), seq_spec(8, LANES), seq_spec(1, LANES),
                      seq_spec(1, LANES), seq_spec(1, LANES),
                      pl.BlockSpec(e_mat.shape, lambda b, pt: (0, 0)),
                      pl.BlockSpec((None, None, LANES, n_buf), lambda b, pt: (layer, b, 0, 0)),
                      pl.BlockSpec(memory_space=pl.ANY)],
            out_specs=seq_spec(8, LANES),
            scratch_shapes=[pltpu.VMEM((2, LANES, n_pages * PAGE), F32), pltpu.SemaphoreType.DMA((2,)),
                            pltpu.VMEM((LANES, n_pages * PAGE), BF)]),
        out_shape=jax.ShapeDtypeStruct((n_seq, 8, LANES), F32),
        compiler_params=_params(("arbitrary",)),
        name="nsa_sel_sample",
    )(page_table, q8, sel, o_cmp, gate, new_sel, new_win, e_mat, win_t, cache_t)


def _gmlp_pool_sample_kernel(u_ref, v_ref, pin_ref, hist_ref, w00_ref, b0_ref, pw_ref, ps_ref, oc_ref, od_ref,
                             *, past_len):
    oc_ref[...] = u_ref[...] * (w00_ref[...] * v_ref[...] + b0_ref[...])
    x = pin_ref[...]
    lane = lax.broadcasted_iota(jnp.int32, (1, GROUP_W), 1)
    grp = lane // HEAD_DIM
    sums = []
    tot = x
    k = 1
    for wlen in POOL_WINDOWS:
        while k < wlen:
            tot = tot + hist_ref[POOL_BUF - k]
            k += 1
        sums.append(tot)
    tot = jnp.where(grp == 0, sums[0], jnp.where(grp == 1, sums[1], jnp.where(grp == 2, sums[2], sums[3])))
    wlen = jnp.where(grp == 0, POOL_WINDOWS[0], jnp.where(grp == 1, POOL_WINDOWS[1],
                                                          jnp.where(grp == 2, POOL_WINDOWS[2], POOL_WINDOWS[3])))
    cnt = jnp.minimum(wlen, past_len + 1).astype(F32)
    d = tot / cnt - x
    od_ref[...] = _dot(d.astype(BF), pw_ref[...]) * ps_ref[...]


def gmlp_pool_sample(u, v, pin, hist_t, layer, w00, b0, pool_w_bd, pool_scale, past_len):
    n_seq = u.shape[0]
    full = lambda shape: pl.BlockSpec(shape, lambda i: (0,) * len(shape))
    return pl.pallas_call(
        functools.partial(_gmlp_pool_sample_kernel, past_len=past_len),
        grid=(1,),
        in_specs=[full((n_seq, GROUP_W))] * 3 +
                 [pl.BlockSpec((None, POOL_BUF, n_seq, GROUP_W), lambda i: (layer, 0, 0, 0)),
                  full((1, GROUP_W)), full((1, GROUP_W)), full((GROUP_W, GROUP_W)), full((1, GROUP_W))],
        out_specs=[full((n_seq, GROUP_W))] * 2,
        out_shape=[jax.ShapeDtypeStruct((n_seq, GROUP_W), F32)] * 2,
        compiler_params=_params(("arbitrary",)),
        name="gmlp_pool_sample",
    )(u, v, pin, hist_t, w00, b0, pool_w_bd, pool_scale)


def sample_layer(x, mod, lw, tables, sel_consts, caches_t, layer, page_table):
    n_seq = x.shape[1]
    past_len = page_table.shape[1] * PAGE
    cmp_t, sel_t, win_t, sb_t, pool_t = caches_t
    ms, e_mat = sel_consts
    m = _mods(mod, (1, n_seq, D_MODEL))
    g = lw["norm_g"]
    x = ffn_half(x, m[0], m[1], m[2], g[0], g[1], lw["w_up"], lw["w_down"], (layer, 0), n_seq)
    pr = in_proj(x, m[3], m[4], g[2], tables[0], tables[1], lw["w_in"], lw["ln_g"], lw["ln_b"], n_seq)
    rows = {k: v[0] for k, v in pr.items()}
    q8 = jnp.pad(rows["qa"].reshape(n_seq, N_HEADS, LANES), ((0, 0), (0, 8 - N_HEADS), (0, 0)))
    o_cmp, sel = nsa_cmp_sample(page_table, q8, rows["kv_cmp"][:, None, :], lw["cmp_pe"], lw["cmp_w"], ms,
                                cmp_t, layer)
    o8 = nsa_sel_sample(page_table, q8, sel, o_cmp, rows["gate"][:, None, :], rows["kv_sel"][:, None, :],
                        rows["kv_win"][:, None, :], e_mat, win_t, sel_t, layer)
    o_a = o8[:, :N_HEADS, HEAD_DIM:].reshape(1, n_seq, GROUP_W).astype(BF)
    o_b = sb_sample(page_table, rows["qb"][:, None, :], sb_t, layer).reshape(1, n_seq, GROUP_W).astype(BF)
    o_c, o_d = gmlp_pool_sample(rows["u"], rows["v"], rows["pin"], pool_t, layer, lw["sg_w00"], lw["sg_b0"],
                                lw["pool_w"], lw["pool_scale"], past_len)
    x = out_proj(x, o_a, o_b, o_c[None].astype(BF), o_d[None].astype(BF), m[5], g[3], lw["w_out"], layer, n_seq)
    x = ffn_half(x, m[6], m[7], m[8], g[4], g[5], lw["w_up"], lw["w_down"], (layer, 1), n_seq)
    state = (rows["kv_cmp"], rows["kv_sel"], rows["kv_win"], rows["kvb"], rows["pin"], rows["v"])
    return x, state


PROMPT_TM = 512


def kernel(x_prompt, x_sample, cache_nsa_cmp, cache_nsa_sel, cache_nsa_win, cache_sb, state_pool, page_table,
           c_prompt, c_sample, w_ada, b_ada, norm_g, w_ffn_up, w_ffn_down, w_in, w_out,
           cmp_pe, cmp_w, sg_ln_g, sg_ln_b, sg_w, sg_b, pool_w, pool_scale):
    n_p, t_len, _ = x_prompt.shape
    n_s, t_dec, _ = x_sample.shape
    assert t_dec == 1, "the sample step advances one token per sequence"
    depth = w_ada.shape[0]
    n_phys = cache_sb.shape[1]
    past_len = page_table.shape[1] * PAGE
    c_all = jnp.concatenate([c_prompt, c_sample], axis=0)
    c_all = jnp.pad(c_all, ((0, (-c_all.shape[0]) % 8), (0, 0)))
    tables_p = _rope_tables(jnp.arange(t_len))
    tables_s = _rope_tables(jnp.full((n_s,), past_len))
    consts = _sel_constants(t_len // CMP_STRIDE, t_len // SEL_BLOCK, t_len)
    n_rows = -(-(past_len + t_dec) // SEL_BLOCK) * SEL_BLOCK
    n_cmp = n_rows // CMP_STRIDE - 1
    n_sel = (n_cmp + 1) // (SEL_BLOCK // CMP_STRIDE)
    ms = _sample_sel_weights(-(-((past_len + TAIL_ROWS) // CMP_STRIDE) // LANES) * LANES,
                             -(-n_sel // LANES) * LANES, n_cmp, n_sel)
    blk_s = math.gcd(LANES, past_len // SEL_BLOCK)
    sel_consts_s = (ms, _sel_constants(1, blk_s, blk_s * SEL_BLOCK)[1])
    caches_t = (jnp.transpose(cache_nsa_cmp, (0, 1, 3, 4, 2)).reshape(depth, n_phys, LANES, PAGE),
                jnp.transpose(cache_nsa_sel, (0, 1, 3, 4, 2)).reshape(depth, n_phys, LANES, PAGE),
                jnp.transpose(cache_nsa_win, (0, 1, 3, 4, 2)).reshape(depth, n_s, LANES, -1),
                jnp.transpose(cache_sb, (0, 1, 3, 4, 5, 2)).reshape(depth, n_phys, 2 * GROUP_W, PAGE),
                jnp.transpose(state_pool, (0, 2, 1, 3)))
    y_p, y_s = x_prompt, x_sample.reshape(1, n_s, D_MODEL)
    st_p, st_s = [], []
    w_ada, w_ffn_up, w_ffn_down, w_out = (w.astype(BF) for w in (w_ada, w_ffn_up, w_ffn_down, w_out))
    for l in range(depth):
        lw = _layer_weights(l, w_ada, b_ada, norm_g, w_ffn_up, w_ffn_down, w_in, w_out,
                            cmp_pe, cmp_w, sg_ln_g, sg_ln_b, sg_w, sg_b, pool_w, pool_scale)
        mod = ada_mod(c_all, lw["w_ada"], lw["b_ada"], l)
        y_p, s = prompt_layer(y_p, mod[:n_p], lw, tables_p, consts, PROMPT_TM)
        st_p.append(s)
        y_s, s = sample_layer(y_s, mod[n_p:n_p + n_s], lw, tables_s, sel_consts_s, caches_t, l, page_table)
        st_s.append(s)

    def stack(states, i):
        return jnp.stack([s[i] for s in states], axis=0)

    def kv_rows(x_t):
        d, bsz, _, t = x_t.shape
        return jnp.transpose(x_t.reshape(d, bsz, 2, HEAD_DIM, t), (0, 1, 4, 2, 3))

    sb_p = stack(st_p, 3)
    sb_p = jnp.transpose(sb_p.reshape(depth, n_p, 2, N_HEADS, HEAD_DIM, t_len), (0, 1, 5, 2, 3, 4))
    win_new = stack(st_s, 2).reshape(depth, n_s, 1, 2, HEAD_DIM)
    pool_new = stack(st_s, 4)[:, :, None, :]
    return (y_p, y_s.reshape(n_s, 1, D_MODEL),
            kv_rows(stack(st_p, 0)), stack(st_s, 0).reshape(depth, n_s, 1, 2, HEAD_DIM),
            kv_rows(stack(st_p, 1)), stack(st_s, 1).reshape(depth, n_s, 1, 2, HEAD_DIM),
            kv_rows(stack(st_p, 2)), jnp.concatenate([cache_nsa_win[:, :, 1:], win_new], axis=2),
            sb_p, stack(st_s, 3).reshape(depth, n_s, 1, 2, N_HEADS, HEAD_DIM),
            stack(st_p, 4), jnp.concatenate([state_pool[:, :, 1:], pool_new], axis=2),
            stack(st_s, 5)[:, :, None, :])
```

```python
import functools
import math

import numpy as np
import jax
import jax.numpy as jnp
from jax import lax
from jax.experimental import pallas as pl
from jax.experimental.pallas import tpu as pltpu

F32 = jnp.float32
BF = jnp.bfloat16

D_MODEL = 1024
HEAD_DIM = 64
N_HEADS = 4
GROUP_W = N_HEADS * HEAD_DIM
D_FF = 2816
N_MOD = 9
PAGE = 128
CMP_STRIDE = 16
CMP_LEN = 32
SEL_BLOCK = 64
N_TOPK = 16
WINDOW = 512
CHUNK = 128
POOL_WINDOWS = (2, 4, 8, 16)
POOL_MAX = 16
POOL_BUF = POOL_MAX - 1
ROPE_THETA = 10000.0
EPS = 1e-6
FORCE_SCORE = 1e9
NEG = -3.0e38
MASK_BIAS = 1.0e30
LANES = 128
VMEM_LIMIT = 56 * 1024 * 1024

C_QA, C_QAS, C_KV, C_KVS, C_G, C_QB, C_KVB, C_UV, C_DIN, C_END = (
    0, 512, 1024, 1408, 1792, 1920, 2176, 2688, 3200, 3456)


def _params(sem, vmem=VMEM_LIMIT):
    return pltpu.CompilerParams(dimension_semantics=sem, vmem_limit_bytes=vmem)


def _dot(a, b):
    return jnp.dot(a, b, preferred_element_type=F32)


def _dot_nt(a, b):
    return lax.dot_general(a, b, (((1,), (1,)), ((), ())), preferred_element_type=F32)


def _rms(x, g):
    return x * lax.rsqrt(jnp.mean(x * x, axis=-1, keepdims=True) + EPS) * g


def _split3(x):
    hi = x.astype(BF)
    r = x - hi.astype(F32)
    mid = r.astype(BF)
    lo = (r - mid.astype(F32)).astype(BF)
    return hi, mid, lo


def _ada_kernel(c_ref, w_ref, b_ref, o_ref):
    c = c_ref[...]
    s = (c * jax.nn.sigmoid(c)).astype(BF)
    o_ref[...] = _dot(s, w_ref[...]) + b_ref[...]


def _stacked_spec(tail, idx):
    return pl.BlockSpec((None,) * len(idx) + tail, lambda *_: idx + (0,) * len(tail))


def ada_mod(c_all, w_ada, b_ada, layer):
    r = c_all.shape[0]
    n = w_ada.shape[2]
    tn = 2304
    return pl.pallas_call(
        _ada_kernel,
        grid=(n // tn,),
        in_specs=[pl.BlockSpec((r, D_MODEL), lambda j: (0, 0)),
                  pl.BlockSpec((None, D_MODEL, tn), lambda j: (layer, 0, j)),
                  pl.BlockSpec((1, tn), lambda j: (0, j))],
        out_specs=pl.BlockSpec((r, tn), lambda j: (0, j)),
        out_shape=jax.ShapeDtypeStruct((r, n), F32),
        compiler_params=_params(("parallel",)),
        name="ada_mod",
    )(c_all, w_ada, b_ada)


FF_CHUNK = 256


def _ffn_kernel(x_ref, sh_ref, sc_ref, gt_ref, g1_ref, g2_ref, wu_ref, wd_ref, o_ref):
    x = x_ref[...]
    h = _rms(x, g1_ref[...]) * (1.0 + sc_ref[...]) + sh_ref[...]
    hb = h.astype(BF)
    acc = jnp.zeros(x.shape, F32)
    for c in range(D_FF // FF_CHUNK):
        lo = c * FF_CHUNK
        gate = _dot(hb, wu_ref[:, lo:lo + FF_CHUNK])
        up = _dot(hb, wu_ref[:, D_FF + lo:D_FF + lo + FF_CHUNK])
        a = (gate * jax.nn.sigmoid(gate) * up).astype(BF)
        acc = acc + _dot(a, wd_ref[lo:lo + FF_CHUNK, :])
    o_ref[...] = x + 0.5 * gt_ref[...] * _rms(acc, g2_ref[...])


def _mod_spec(mod, tm):
    if mod.shape[1] == 1:
        return pl.BlockSpec((None, 1, D_MODEL), lambda b, t: (b, 0, 0))
    return pl.BlockSpec((None, tm, D_MODEL), lambda b, t: (b, t, 0))


def _row_spec(tm, w):
    return pl.BlockSpec((None, tm, w), lambda b, t: (b, t, 0))


def _const_spec(shape):
    nd = len(shape)
    return pl.BlockSpec(shape, lambda b, t: (0,) * nd)


def ffn_half(x, shift, scale, gate, g1, g2, w_up, w_down, widx, tm):
    bsz, t_len, _ = x.shape
    return pl.pallas_call(
        _ffn_kernel,
        grid=(bsz, t_len // tm),
        in_specs=[_row_spec(tm, D_MODEL), _mod_spec(shift, tm), _mod_spec(scale, tm), _mod_spec(gate, tm),
                  _const_spec((1, D_MODEL)), _const_spec((1, D_MODEL)),
                  _stacked_spec((D_MODEL, 2 * D_FF), widx), _stacked_spec((D_FF, D_MODEL), widx)],
        out_specs=_row_spec(tm, D_MODEL),
        out_shape=jax.ShapeDtypeStruct(x.shape, F32),
        compiler_params=_params(("parallel", "parallel")),
        name="ffn_half",
    )(x, shift, scale, gate, g1, g2, w_up, w_down)


def _gelu_tanh(x):
    return 0.5 * x * (1.0 + jnp.tanh(np.sqrt(2.0 / np.pi).astype(np.float32) * (x + 0.044715 * (x * x * x))))


PROJ_IN = ("x", "shift", "scale", "g", "cos", "sin", "w", "ln_g", "ln_b")
PROJ_IN_T = ("w_t", "cos_t", "sin_t")
PROJ_OUT = (("qa", 2 * GROUP_W, BF), ("kv_cmp", LANES, F32), ("gate", LANES, F32), ("qb", GROUP_W, BF),
            ("u", GROUP_W, F32), ("v", GROUP_W, F32), ("pin", GROUP_W, F32))
PROJ_OUT_ROWS = (("kv_sel", LANES, F32), ("kv_win", LANES, F32), ("kvb", 2 * GROUP_W, F32))
PROJ_OUT_B16 = (("kv_sel_b", LANES, BF), ("kv_win_b", LANES, BF), ("kvb_b", 2 * GROUP_W, BF))
PROJ_OUT_T = (("kvt_cmp", LANES), ("kvt_sel", LANES), ("kvt_win", LANES), ("kvbt", 2 * GROUP_W))


def _proj_kernel(*refs, feature_major):
    names = PROJ_IN + (PROJ_IN_T if feature_major else ())
    names += tuple(n for n, _, _ in PROJ_OUT + (PROJ_OUT_B16 if feature_major else PROJ_OUT_ROWS))
    names += tuple(n for n, _ in PROJ_OUT_T) if feature_major else ()
    r = dict(zip(names, refs))
    x = r["x"][...]
    h = _rms(x, r["g"][...]) * (1.0 + r["scale"][...]) + r["shift"][...]
    hb = h.astype(BF)
    cos = r["cos"][...]
    sin = r["sin"][...]
    w_ref = r["w"]

    def mm(lo, hi):
        return _dot(hb, w_ref[:, lo:hi])

    def cols(a, base, lo, hi):
        return a[:, lo - base:hi - base]

    q_p, q_ps = mm(C_QA, C_QAS), mm(C_QAS, C_KV)
    for j in range(N_HEADS):
        span = slice(LANES * j, LANES * (j + 1))
        r["qa"][:, span] = (q_p[:, span] * cos + q_ps[:, span] * sin).astype(BF)
    lane = lax.broadcasted_iota(jnp.int32, cos.shape, 1)
    ckv = jnp.where(lane < HEAD_DIM, cos, 1.0)
    kvg = mm(C_KV, C_QB)
    for j, nm in enumerate(("kv_cmp", "kv_sel", "kv_win")):
        p = cols(kvg, C_KV, C_KV + LANES * j, C_KV + LANES * (j + 1))
        ps = cols(kvg, C_KV, C_KVS + LANES * j, C_KVS + LANES * (j + 1))
        kv = p * ckv + ps * sin
        if nm in r:
            r[nm][...] = kv
        if nm + "_b" in r:
            r[nm + "_b"][...] = kv.astype(BF)
    r["gate"][...] = jax.nn.sigmoid(cols(kvg, C_KV, C_G, C_QB))
    rest = mm(C_QB, C_END)
    r["qb"][...] = cols(rest, C_QB, C_QB, C_KVB).astype(BF)
    kvb = cols(rest, C_QB, C_KVB, C_UV)
    if feature_major:
        r["kvb_b"][...] = kvb.astype(BF)
    else:
        r["kvb"][...] = kvb
    r["u"][...] = _gelu_tanh(cols(rest, C_QB, C_UV, C_UV + GROUP_W))
    v = _gelu_tanh(cols(rest, C_QB, C_UV + GROUP_W, C_DIN))
    vc = v - jnp.mean(v, axis=-1, keepdims=True)
    vn = vc * lax.rsqrt(jnp.mean(vc * vc, axis=-1, keepdims=True) + EPS)
    r["v"][...] = vn * r["ln_g"][...] + r["ln_b"][...]
    r["pin"][...] = cols(rest, C_QB, C_DIN, C_END)
    if feature_major:
        wt_ref = r["w_t"]
        cos_t = r["cos_t"][...]
        sin_t = r["sin_t"][...]
        n_kv = 3 * LANES
        for j, nm in enumerate(("kvt_cmp", "kvt_sel", "kvt_win")):
            p = _dot_nt(wt_ref[LANES * j:LANES * (j + 1), :], hb)
            ps = _dot_nt(wt_ref[n_kv + LANES * j:n_kv + LANES * (j + 1), :], hb)
            r[nm][...] = p * cos_t + ps * sin_t
        r["kvbt"][...] = _dot_nt(wt_ref[2 * n_kv:2 * n_kv + 2 * GROUP_W, :], hb)


def in_proj(x, shift, scale, g, cos, sin, w_all, ln_g, ln_b, tm, t_side=None):
    bsz, t_len, _ = x.shape
    feature_major = t_side is not None
    tab_spec = pl.BlockSpec((tm, LANES), lambda b, t: (t, 0))
    in_specs = [_row_spec(tm, D_MODEL), _mod_spec(shift, tm), _mod_spec(scale, tm),
                _const_spec((1, D_MODEL)), tab_spec, tab_spec,
                _const_spec((D_MODEL, C_END)), _const_spec((1, GROUP_W)), _const_spec((1, GROUP_W))]
    args = [x, shift, scale, g, cos, sin, w_all, ln_g, ln_b]
    outs = PROJ_OUT + (PROJ_OUT_B16 if feature_major else PROJ_OUT_ROWS)
    out_specs = [_row_spec(tm, w) for _, w, _ in outs]
    out_shape = [jax.ShapeDtypeStruct((bsz, t_len, w), dt) for _, w, dt in outs]
    names = [n for n, _, _ in outs]
    if feature_major:
        tab_t_spec = pl.BlockSpec((LANES, tm), lambda b, t: (0, t))
        in_specs += [_const_spec(t_side[0].shape), tab_t_spec, tab_t_spec]
        args += list(t_side)
        out_specs += [pl.BlockSpec((None, w, tm), lambda b, t: (b, 0, t)) for _, w in PROJ_OUT_T]
        out_shape += [jax.ShapeDtypeStruct((bsz, w, t_len), F32) for _, w in PROJ_OUT_T]
        names += [n for n, _ in PROJ_OUT_T]
    res = pl.pallas_call(
        functools.partial(_proj_kernel, feature_major=feature_major),
        grid=(bsz, t_len // tm),
        in_specs=in_specs, out_specs=out_specs, out_shape=out_shape,
        compiler_params=_params(("parallel", "parallel")),
        name="in_proj",
    )(*args)
    return dict(zip(names, res))


def _outproj_kernel(x_ref, oa_ref, ob_ref, oc_ref, od_ref, gt_ref, g_ref, w_ref, o_ref):
    y = _dot(oa_ref[...], w_ref[0:GROUP_W, :])
    y = y + _dot(ob_ref[...], w_ref[GROUP_W:2 * GROUP_W, :])
    y = y + _dot(oc_ref[...], w_ref[2 * GROUP_W:3 * GROUP_W, :])
    y = y + _dot(od_ref[...], w_ref[3 * GROUP_W:4 * GROUP_W, :])
    o_ref[...] = x_ref[...] + gt_ref[...] * _rms(y, g_ref[...])


def out_proj(x, o_a, o_b, o_c, o_d, gate, g, w_out, layer, tm):
    bsz, t_len, _ = x.shape
    return pl.pallas_call(
        _outproj_kernel,
        grid=(bsz, t_len // tm),
        in_specs=[_row_spec(tm, D_MODEL)] + [_row_spec(tm, GROUP_W)] * 4 +
                 [_mod_spec(gate, tm), _const_spec((1, D_MODEL)), _stacked_spec((D_MODEL, D_MODEL), (layer,))],
        out_specs=_row_spec(tm, D_MODEL),
        out_shape=jax.ShapeDtypeStruct(x.shape, F32),
        compiler_params=_params(("parallel", "parallel")),
        name="out_proj",
    )(x, o_a, o_b, o_c, o_d, gate, g, w_out)


def _compress_rows(row_ref, n_grp, pe_ref, w_ref, base=0):
    parts = [row_ref[pl.ds(base + l, n_grp, stride=CMP_STRIDE), :].astype(BF) for l in range(CMP_STRIDE)]
    both = _dot(jnp.concatenate(parts, axis=1), w_ref[...])
    pe_w = _dot(pe_ref[...], w_ref[...])
    return both[:, :LANES] + pe_w[0:1, :LANES], both[:, LANES:] + pe_w[1:2, LANES:]


def _compress_kernel(row_ref, pe_ref, w_ref, o_ref, hi_sc):
    n_grp = o_ref.shape[0]
    lo, hi = _compress_rows(row_ref, n_grp, pe_ref, w_ref)
    hi_sc[0:n_grp, :] = hi
    hi_sc[n_grp:n_grp + 8, :] = jnp.zeros((8, LANES), F32)
    o_ref[...] = (lo + hi_sc[pl.ds(1, n_grp), :]).astype(BF)


def compress_prompt(kv_cmp, pe, w_cmp):
    bsz, t_len, _ = kv_cmp.shape
    n_grp = t_len // CMP_STRIDE
    return pl.pallas_call(
        _compress_kernel,
        grid=(bsz,),
        in_specs=[pl.BlockSpec((None, t_len, LANES), lambda b: (b, 0, 0)),
                  pl.BlockSpec((8, CMP_STRIDE * LANES), lambda b: (0, 0)),
                  pl.BlockSpec((CMP_STRIDE * LANES, 2 * LANES), lambda b: (0, 0))],
        out_specs=pl.BlockSpec((None, n_grp, LANES), lambda b: (b, 0, 0)),
        out_shape=jax.ShapeDtypeStruct((bsz, n_grp, LANES), BF),
        scratch_shapes=[pltpu.VMEM((n_grp + 8, LANES), F32)],
        compiler_params=_params(("parallel",)),
        name="nsa_compress",
    )(kv_cmp, pe, w_cmp)


NSA_Q = 256
NSA_KC = 512


def _softmax_bias(s, bias):
    sm = s + bias
    m = jnp.maximum(jnp.max(sm, axis=-1, keepdims=True), -0.5 * MASK_BIAS)
    return jnp.exp(sm - m)


def _softmax_rows(s, allow):
    sm = jnp.where(allow, s, NEG)
    m = jnp.max(sm, axis=-1, keepdims=True)
    e = jnp.where(allow, jnp.exp(sm - m), 0.0)
    return e / jnp.maximum(jnp.sum(e, axis=-1, keepdims=True), 1e-30)


def _topk_select(score, valid, n_pick, axis=0):
    n_blk = score.shape[axis]
    j_io = lax.broadcasted_iota(jnp.int32, score.shape, axis)
    sel = jnp.zeros(score.shape, F32)
    sc = score
    for _ in range(n_pick):
        m = jnp.max(sc, axis=axis, keepdims=True)
        idx = jnp.min(jnp.where(sc == m, j_io, n_blk), axis=axis, keepdims=True)
        pick = j_io == idx
        sel = jnp.where(pick, 1.0, sel)
        sc = jnp.where(pick, NEG, sc)
    return jnp.where(valid, sel, 0.0)


def _nsa_prompt_kernel(qa_ref, g_ref, kcmp_ref, ksel_ref, kwin_ref, mt_ref, e_ref, o_ref,
                       m_sc, acc_sc, sa_sc, sb_sc):
    i = pl.program_id(1)
    s0 = i * NSA_Q
    n_cmp = kcmp_ref.shape[0]
    n_sel = mt_ref.shape[0]
    qs = jnp.concatenate([qa_ref[:, LANES * h:LANES * (h + 1)] for h in range(N_HEADS)], axis=0)
    qpos = s0 + lax.broadcasted_iota(jnp.int32, (NSA_Q, 1), 0)

    def heads(x):
        return jnp.concatenate([x] * N_HEADS, axis=0)

    kc = kcmp_ref[...]
    blk_end = lax.broadcasted_iota(jnp.int32, (1, n_cmp), 1) * CMP_STRIDE + (CMP_LEN - 1)
    e = _softmax_bias(_dot_nt(qs, kc), heads(jnp.where(blk_end <= qpos, 0.0, -MASK_BIAS)))
    p = e * (1.0 / jnp.maximum(jnp.sum(e, axis=-1, keepdims=True), 1e-30))
    o_cmp = _dot(p.astype(BF), kc)
    ps = p[0:NSA_Q] + p[NSA_Q:2 * NSA_Q] + p[2 * NSA_Q:3 * NSA_Q] + p[3 * NSA_Q:4 * NSA_Q]
    mt = mt_ref[...]
    p_slc_t = sum(_dot_nt(mt, part) for part in _split3(ps))
    j_io = lax.broadcasted_iota(jnp.int32, (n_sel, NSA_Q), 0)
    cur = (s0 + lax.broadcasted_iota(jnp.int32, (1, NSA_Q), 1)) // SEL_BLOCK
    forced = (j_io == 0) | (j_io == cur) | (j_io == cur - 1)
    valid = j_io <= cur
    score_t = jnp.where(valid, jnp.where(forced, FORCE_SCORE, p_slc_t), NEG)
    sel = _topk_select(score_t, valid, min(N_TOPK, n_sel)).T.astype(BF)

    m_sc[...] = jnp.full(m_sc.shape, NEG, F32)
    acc_sc[...] = jnp.zeros(acc_sc.shape, F32)
    rep = NSA_KC // LANES
    key_lane = lax.broadcasted_iota(jnp.int32, (1, LANES), 1) < HEAD_DIM
    last = (s0 + NSA_Q - 1) // NSA_KC

    def chunk_start(c):
        return pl.multiple_of(jnp.minimum(c, last) * NSA_KC, NSA_KC)

    def scores(c, dst):
        dst[...] = _dot_nt(qs, ksel_ref[pl.ds(chunk_start(c), NSA_KC), :])

    def update(c, src):
        k0 = chunk_start(c)
        kv = ksel_ref[pl.ds(k0, NSA_KC), :]
        kv1 = jnp.where(key_lane, jnp.ones_like(kv), kv)
        selx = _dot(sel, e_ref[:, pl.ds(k0, NSA_KC)])
        tok = k0 + lax.broadcasted_iota(jnp.int32, (1, NSA_KC), 1)
        live = jnp.where(c <= last, 0.0, -MASK_BIAS)
        sm = src[...] + heads(jnp.where((selx > 0.5) & (tok <= qpos), live, -MASK_BIAS))
        m_old = m_sc[...]
        m_new = jnp.maximum(m_old, jnp.max(sm, axis=-1, keepdims=True))
        pe = jnp.exp(sm - jnp.concatenate([m_new] * rep, axis=1))
        acc_sc[...] = jnp.exp(m_old - m_new) * acc_sc[...] + _dot(pe.astype(BF), kv1)
        m_sc[...] = m_new

    scores(0, sa_sc)

    def chunk_pair(j, carry):
        scores(2 * j + 1, sb_sc)
        update(2 * j, sa_sc)
        scores(2 * j + 2, sa_sc)
        update(2 * j + 1, sb_sc)
        return carry

    lax.fori_loop(0, last // 2 + 1, chunk_pair, 0)

    n_win = WINDOW + NSA_Q
    w0 = pl.multiple_of(jnp.maximum(s0 - WINDOW, 0), NSA_Q)
    kvw = kwin_ref[pl.ds(w0, n_win), :]
    dist = qpos - (w0 + lax.broadcasted_iota(jnp.int32, (1, n_win), 1))
    ew = _softmax_bias(_dot_nt(qs, kvw), heads(jnp.where((dist >= 0) & (dist <= WINDOW), 0.0, -MASK_BIAS)))
    o_win = _dot(ew.astype(BF), jnp.where(key_lane, jnp.ones_like(kvw), kvw))

    g = g_ref[...]
    outs = []
    for h in range(N_HEADS):
        rows = slice(h * NSA_Q, (h + 1) * NSA_Q)
        acc = acc_sc[rows, :]
        win = o_win[rows]
        o = (g[:, 3 * h:3 * h + 1] * o_cmp[rows]
             + g[:, 3 * h + 1:3 * h + 2] * (acc / jnp.maximum(acc[:, 0:1], 1e-30))
             + g[:, 3 * h + 2:3 * h + 3] * (win / jnp.maximum(win[:, 0:1], 1e-30)))
        outs.append(o[:, HEAD_DIM:])
    o_ref[...] = jnp.concatenate(outs, axis=1).astype(BF)


def nsa_prompt(qa, gate, kcmp, ksel_b, kwin_b, mt, e_mat):
    bsz, t_len, _ = qa.shape
    n_cmp = kcmp.shape[1]
    n_sel = mt.shape[0]
    return pl.pallas_call(
        _nsa_prompt_kernel,
        grid=(bsz, t_len // NSA_Q),
        in_specs=[_row_spec(NSA_Q, 2 * GROUP_W), _row_spec(NSA_Q, LANES),
                  pl.BlockSpec((None, n_cmp, LANES), lambda b, t: (b, 0, 0)),
                  pl.BlockSpec((None, t_len, LANES), lambda b, t: (b, 0, 0)),
                  pl.BlockSpec((None, t_len, LANES), lambda b, t: (b, 0, 0)),
                  _const_spec((n_sel, n_cmp)), _const_spec((n_sel, t_len))],
        out_specs=_row_spec(NSA_Q, GROUP_W),
        out_shape=jax.ShapeDtypeStruct((bsz, t_len, GROUP_W), BF),
        scratch_shapes=[pltpu.VMEM((N_HEADS * NSA_Q, LANES), F32), pltpu.VMEM((N_HEADS * NSA_Q, LANES), F32),
                        pltpu.VMEM((N_HEADS * NSA_Q, NSA_KC), F32), pltpu.VMEM((N_HEADS * NSA_Q, NSA_KC), F32)],
        compiler_params=_params(("parallel", "arbitrary")),
        name="nsa_prompt",
    )(qa, gate, kcmp, ksel_b, kwin_b, mt, e_mat)


SB_Q = 256
SB_K = 256


SB_STOP = -110.0


def _softplus(z):
    return jnp.maximum(z, 0.0) + jnp.log(1.0 + jnp.exp(-jnp.abs(z)))


def _tri_neg(n):
    r = lax.broadcasted_iota(jnp.int32, (2 * n, n + LANES), 0)
    c = lax.broadcasted_iota(jnp.int32, (2 * n, n + LANES), 1)
    r = jnp.where(r >= n, r - n, r)
    return jnp.where((r > c) | (c >= n), -1.0, 0.0).astype(BF)


def _sb_weights(z, tri, carry, mask):
    n_k = z.shape[1]
    sp = _softplus(z)
    spm = sp if mask is None else jnp.where(mask, sp, 0.0)
    hi = spm.astype(BF)
    lo = (spm - hi.astype(F32)).astype(BF)
    cs = _dot(jnp.concatenate([hi, lo], axis=1), tri)
    after = cs[:, :n_k] + jnp.concatenate([carry] * (n_k // LANES), axis=1)
    a = jnp.exp(z - sp + after)
    if mask is not None:
        a = jnp.where(mask, a, 0.0)
    return a.astype(BF), carry + cs[:, n_k:]


def _sb_prompt_kernel(qb_ref, kvb_ref, o_ref, acc_sc, car_sc):
    i = pl.program_id(1)
    q = qb_ref[...]
    lane = lax.broadcasted_iota(jnp.int32, (1, GROUP_W), 1)
    qh = [jnp.where(lane // HEAD_DIM == h, q, jnp.zeros_like(q)) for h in range(N_HEADS)]
    tri = _tri_neg(SB_K)
    r = lax.broadcasted_iota(jnp.int32, (SB_Q, SB_K), 0)
    c = lax.broadcasted_iota(jnp.int32, (SB_Q, SB_K), 1)
    diag_mask = c < r

    def chunk(k0, mask, first):
        k = kvb_ref[pl.ds(k0, SB_K), 0:GROUP_W]
        v = kvb_ref[pl.ds(k0, SB_K), GROUP_W:2 * GROUP_W]
        car_max = None
        for h in range(N_HEADS):
            car = jnp.zeros((SB_Q, LANES), F32) if first else car_sc[h]
            a, car = _sb_weights(_dot_nt(qh[h], k), tri, car, mask)
            pv = _dot(a, v)
            acc_sc[h] = pv if first else acc_sc[h] + pv
            car_sc[h] = car
            car_max = car if car_max is None else jnp.maximum(car_max, car)
        return jnp.max(car_max)

    car_max = chunk(pl.multiple_of(i * SB_Q, SB_Q), diag_mask, True)

    def more(st):
        j, car_max = st
        return (j < i) & (car_max > SB_STOP)

    def body(st):
        j, _ = st
        return j + 1, chunk(pl.multiple_of((i - 1 - j) * SB_K, SB_K), None, False)

    lax.while_loop(more, body, (0, car_max))
    o = jnp.zeros((SB_Q, GROUP_W), F32)
    for h in range(N_HEADS):
        o = jnp.where(lane // HEAD_DIM == h, acc_sc[h], o)
    o_ref[...] = o.astype(BF)


def sb_prompt(qb, kvb_b):
    bsz, t_len, _ = qb.shape
    return pl.pallas_call(
        _sb_prompt_kernel,
        grid=(bsz, t_len // SB_Q),
        in_specs=[_row_spec(SB_Q, GROUP_W),
                  pl.BlockSpec((None, t_len, 2 * GROUP_W), lambda b, t: (b, 0, 0))],
        out_specs=_row_spec(SB_Q, GROUP_W),
        out_shape=jax.ShapeDtypeStruct((bsz, t_len, GROUP_W), BF),
        scratch_shapes=[pltpu.VMEM((N_HEADS, SB_Q, GROUP_W), F32),
                        pltpu.VMEM((N_HEADS, SB_Q, LANES), F32)],
        compiler_params=_params(("parallel", "arbitrary")),
        name="sb_prompt",
    )(qb, kvb_b)


def _pool_mix(ext_ref, tm, tpos, pool_w_ref, pool_s_ref):
    def shifted(ref, k):
        return ref[pl.ds(POOL_MAX - k, tm), :]
    x = shifted(ext_ref, 0)
    lane = lax.broadcasted_iota(jnp.int32, (1, GROUP_W), 1)
    grp = lane // HEAD_DIM
    s2 = x + shifted(ext_ref, 1)
    s4 = s2 + shifted(ext_ref, 2) + shifted(ext_ref, 3)
    s8 = s4 + sum(shifted(ext_ref, k) for k in range(4, 8))
    s16 = s8 + sum(shifted(ext_ref, k) for k in range(8, 16))
    tot = jnp.where(grp == 0, s2, jnp.where(grp == 1, s4, jnp.where(grp == 2, s8, s16)))
    wlen = jnp.where(grp == 0, 2, jnp.where(grp == 1, 4, jnp.where(grp == 2, 8, 16)))
    cnt = jnp.minimum(wlen, tpos + 1).astype(F32)
    d = tot / cnt - x
    return _dot(d.astype(BF), pool_w_ref[...]) * pool_s_ref[...]


def _gmlp_pool_kernel(u_ref, v_ref, pin_ref, halo_ref, sgw_ref, sgb_ref, pw_ref, ps_ref,
                      oc_ref, od_ref, ext_sc):
    t = pl.program_id(1)
    tm = u_ref.shape[0]
    lane = lax.broadcasted_iota(jnp.int32, (1, GROUP_W), 1)
    for c in range(tm // CHUNK):
        rows = slice(c * CHUNK, (c + 1) * CHUNK)
        v = v_ref[rows, :]
        vz = jnp.zeros_like(v)
        vst = jnp.concatenate([jnp.where(lane // HEAD_DIM == h, v, vz) for h in range(N_HEADS)], axis=0)
        s = _dot(sgw_ref[...], vst.astype(BF)) + sgb_ref[...]
        oc_ref[rows, :] = (u_ref[rows, :] * s).astype(BF)
    halo = halo_ref[...]
    ext_sc[0:POOL_MAX, :] = jnp.where(t > 0, halo, jnp.zeros_like(halo))
    ext_sc[POOL_MAX:POOL_MAX + tm, :] = pin_ref[...]
    tpos = t * tm + lax.broadcasted_iota(jnp.int32, (tm, 1), 0)
    od_ref[...] = _pool_mix(ext_sc, tm, tpos, pw_ref, ps_ref).astype(BF)


def gmlp_pool_prompt(u, v, pin, sgw_cat, sgb_full, pool_w_bd, pool_scale, tm):
    bsz, t_len, _ = u.shape
    per = tm // POOL_MAX
    halo_spec = pl.BlockSpec((None, POOL_MAX, GROUP_W), lambda b, t: (b, jnp.maximum(t * per - 1, 0), 0))
    return pl.pallas_call(
        _gmlp_pool_kernel,
        grid=(bsz, t_len // tm),
        in_specs=[_row_spec(tm, GROUP_W), _row_spec(tm, GROUP_W), _row_spec(tm, GROUP_W), halo_spec,
                  _const_spec((CHUNK, N_HEADS * CHUNK)), _const_spec((CHUNK, GROUP_W)),
                  _const_spec((GROUP_W, GROUP_W)), _const_spec((1, GROUP_W))],
        out_specs=[_row_spec(tm, GROUP_W), _row_spec(tm, GROUP_W)],
        out_shape=[jax.ShapeDtypeStruct((bsz, t_len, GROUP_W), BF)] * 2,
        scratch_shapes=[pltpu.VMEM((POOL_MAX + tm, GROUP_W), F32)],
        compiler_params=_params(("parallel", "parallel")),
        name="gmlp_pool",
    )(u, v, pin, pin, sgw_cat, sgb_full, pool_w_bd, pool_scale)


def _swap_neg(w):
    half = HEAD_DIM // 2
    w = w.reshape(w.shape[0], -1, 2, half)
    return jnp.stack([-w[:, :, 1], w[:, :, 0]], axis=2).reshape(w.shape[0], -1)


def _pad_heads(w):
    w = w.reshape(w.shape[0], -1, HEAD_DIM)
    return jnp.concatenate([w, jnp.zeros_like(w)], axis=2).reshape(w.shape[0], -1)


def _proj_weights(w_in):
    a_q, a_kv, a_g, b_qkv, c_uv, d_in = jnp.split(
        w_in, np.cumsum([GROUP_W, 6 * HEAD_DIM, 3 * N_HEADS, 3 * GROUP_W, 2 * GROUP_W]).tolist(), axis=1)
    scale = HEAD_DIM ** -0.5
    a_q = a_q * scale
    kv = a_kv.reshape(-1, 3, 2, HEAD_DIM)
    kv_sw = jnp.concatenate([_swap_neg(kv[:, :, 0].reshape(-1, 3 * HEAD_DIM)).reshape(-1, 3, 1, HEAD_DIM),
                             jnp.zeros_like(kv[:, :, 1:2])], axis=2).reshape(-1, 6 * HEAD_DIM)
    g_pad = jnp.pad(a_g, ((0, 0), (0, LANES - 3 * N_HEADS)))
    b_q = b_qkv[:, :GROUP_W] * scale
    w_all = jnp.concatenate([_pad_heads(a_q), _pad_heads(_swap_neg(a_q)), a_kv, kv_sw, g_pad,
                             b_q, b_qkv[:, GROUP_W:], c_uv, d_in], axis=1)
    w_t = jnp.concatenate([a_kv, kv_sw, b_qkv[:, GROUP_W:]], axis=1).T
    return w_all.astype(BF), w_t.astype(BF)


def _rope_tables(pos):
    half = HEAD_DIM // 2
    inv = ROPE_THETA ** (-jnp.arange(half, dtype=F32) / half)
    ang = pos.astype(F32)[:, None] * inv[None, :]
    cos, sin = jnp.cos(ang), jnp.sin(ang)
    cos_t = jnp.concatenate([cos.T, cos.T, jnp.ones((HEAD_DIM, pos.shape[0]), F32)], axis=0)
    sin_t = jnp.concatenate([sin.T, sin.T, jnp.zeros((HEAD_DIM, pos.shape[0]), F32)], axis=0)
    return jnp.tile(cos, (1, 4)), jnp.tile(sin, (1, 4)), cos_t, sin_t


def _cmp_weights(cmp_w):
    w = jnp.zeros((CMP_LEN, 2, HEAD_DIM, 2, HEAD_DIM), F32)
    w = w.at[:, 0, :, 0, :].set(cmp_w[0]).at[:, 1, :, 1, :].set(cmp_w[1])
    w = w.reshape(2, CMP_STRIDE * LANES, LANES)
    return jnp.concatenate([w[0], w[1]], axis=1).astype(BF)


def _cmp_pe_rows(cmp_pe):
    return jnp.pad(cmp_pe.reshape(2, CMP_STRIDE * LANES), ((0, 6), (0, 0))).astype(BF)


def _sel_constants(n_cmp, n_sel, t_len):
    n = np.arange(n_cmp)[None, :]
    j = np.arange(n_sel)[:, None]
    mt = ((n >= 4 * j - 1) & (n <= 4 * j + 3)).astype(np.float32) + ((n >= 4 * j) & (n <= 4 * j + 2))
    e = (np.arange(t_len)[None, :] // SEL_BLOCK == j).astype(np.float32)
    return jnp.asarray(mt, BF), jnp.asarray(e, BF)


def _gmlp_weights(sg_w, sg_b):
    wm = sg_w * jnp.tril(jnp.ones((CHUNK, CHUNK), sg_w.dtype))
    w_cat = jnp.transpose(wm, (1, 0, 2)).reshape(CHUNK, N_HEADS * CHUNK).astype(BF)
    b_full = jnp.repeat(sg_b.T, HEAD_DIM, axis=1)
    return w_cat, b_full


def _pool_weights(pool_w):
    w = jnp.zeros((N_HEADS, HEAD_DIM, N_HEADS, HEAD_DIM), F32)
    for g in range(N_HEADS):
        w = w.at[g, :, g, :].set(pool_w[g])
    return w.reshape(GROUP_W, GROUP_W).astype(BF)


def _sample_sel_weights(n_cmp_pad, n_sel_pad, n_cmp, n_sel):
    n = np.arange(n_cmp_pad)[:, None]
    j = np.arange(n_sel_pad)[None, :]
    ms = ((n >= 4 * j - 1) & (n <= 4 * j + 3)).astype(np.float32) + ((n >= 4 * j) & (n <= 4 * j + 2))
    ms = ms * ((n < n_cmp) & (j < n_sel))
    return jnp.asarray(ms, BF)


def _layer_weights(l, w_ada, b_ada, norm_g, w_ffn_up, w_ffn_down, w_in, w_out,
                   cmp_pe, cmp_w, sg_ln_g, sg_ln_b, sg_w, sg_b, pool_w, pool_scale):
    sgw_cat, sgb_full = _gmlp_weights(sg_w[l], sg_b[l])
    w_proj, w_proj_t = _proj_weights(w_in[l])
    return dict(
        layer=l, w_ada=w_ada, b_ada=b_ada[l][None, :], norm_g=norm_g[l][:, None, :],
        w_up=w_ffn_up, w_down=w_ffn_down, w_in=w_proj, w_in_t=w_proj_t, w_out=w_out,
        cmp_pe=_cmp_pe_rows(cmp_pe[l]), cmp_w=_cmp_weights(cmp_w[l]),
        ln_g=sg_ln_g[l][None, :], ln_b=sg_ln_b[l][None, :], sgw_cat=sgw_cat, sgb_full=sgb_full,
        sg_w00=jnp.repeat(sg_w[l][:, 0, 0], HEAD_DIM)[None, :], sg_b0=jnp.repeat(sg_b[l][:, 0], HEAD_DIM)[None, :],
        pool_w=_pool_weights(pool_w[l]), pool_scale=pool_scale[l][None, :])


def _mods(mod, shape):
    m = mod.reshape(mod.shape[0], N_MOD, D_MODEL)
    return [m[:, k].reshape(shape) for k in range(N_MOD)]


def prompt_layer(x, mod, lw, tables, consts, tm):
    bsz, t_len, _ = x.shape
    m = _mods(mod, (bsz, 1, D_MODEL))
    g = lw["norm_g"]
    cos, sin, cos_t, sin_t = tables
    l = lw["layer"]
    x = ffn_half(x, m[0], m[1], m[2], g[0], g[1], lw["w_up"], lw["w_down"], (l, 0), tm)
    pr = in_proj(x, m[3], m[4], g[2], cos, sin, lw["w_in"], lw["ln_g"], lw["ln_b"], tm,
                 t_side=(lw["w_in_t"], cos_t, sin_t))
    kcmp = compress_prompt(pr["kv_cmp"], lw["cmp_pe"], lw["cmp_w"])
    o_a = nsa_prompt(pr["qa"], pr["gate"], kcmp, pr["kv_sel_b"], pr["kv_win_b"], *consts)
    o_b = sb_prompt(pr["qb"], pr["kvb_b"])
    o_c, o_d = gmlp_pool_prompt(pr["u"], pr["v"], pr["pin"], lw["sgw_cat"], lw["sgb_full"],
                                lw["pool_w"], lw["pool_scale"], tm)
    x = out_proj(x, o_a, o_b, o_c, o_d, m[5], g[3], lw["w_out"], l, tm)
    x = ffn_half(x, m[6], m[7], m[8], g[4], g[5], lw["w_up"], lw["w_down"], (l, 1), tm)
    n_win = min(WINDOW, t_len)
    state = (pr["kvt_cmp"], pr["kvt_sel"], pr["kvt_win"][:, :, t_len - n_win:], pr["kvbt"],
             pr["pin"][:, t_len - POOL_BUF:])
    return x, state


def _page_copies(cache_ref, layer, pt_ref, b, first_page, n_pages, buf_ref, slot, sem_ref):
    return [pltpu.make_async_copy(cache_ref.at[layer, pt_ref[b, first_page + p]], buf_ref.at[slot, p],
                                  sem_ref.at[slot]) for p in range(n_pages)]


SB_GROUP = 4


def _sb_sample_kernel(pt_ref, q_ref, cache_ref, o_ref, buf0, buf, sem0, sem, *, layer, n_pages):
    b = pl.program_id(0)
    n_grp = n_pages // SB_GROUP

    def copies(seq, grp, dst, slot, sm):
        return _page_copies(cache_ref, layer, pt_ref, seq, (n_grp - 1 - grp) * SB_GROUP, SB_GROUP, dst, slot, sm)

    @pl.when(b == 0)
    def _():
        for c in copies(0, 0, buf0, 0, sem0):
            c.start()

    @pl.when(b + 1 < pl.num_programs(0))
    def _():
        for c in copies(b + 1, 0, buf0, (b + 1) % 2, sem0):
            c.start()

    if n_grp > 1:
        for c in copies(b, 1, buf, 1, sem):
            c.start()

    row = lax.broadcasted_iota(jnp.int32, (8, GROUP_W), 0)
    lane = lax.broadcasted_iota(jnp.int32, (8, GROUP_W), 1)
    head_lanes = lane // HEAD_DIM == row
    q = jnp.broadcast_to(q_ref[...].astype(F32), (8, GROUP_W))
    qm = jnp.where(head_lanes, q, 0.0).astype(BF)
    tri = _tri_neg(PAGE)

    def sweep(src, slot, acc, car):
        for p in reversed(range(SB_GROUP)):
            kt = src[slot, p, 0:GROUP_W, :].astype(BF)
            vt = src[slot, p, GROUP_W:2 * GROUP_W, :].astype(BF)
            a, car = _sb_weights(_dot(qm, kt), tri, car, None)
            acc = acc + _dot_nt(a, vt)
        return acc, car

    for c in copies(b, 0, buf0, b % 2, sem0):
        c.wait()
    acc, car = sweep(buf0, b % 2, jnp.zeros((8, GROUP_W), F32), jnp.zeros((8, LANES), F32))

    def more(st):
        g, car_max, _, _ = st
        return (g < n_grp) & (car_max > SB_STOP)

    def body(st):
        g, _, acc, car = st
        slot = g % 2

        @pl.when(g + 1 < n_grp)
        def _():
            for c in copies(b, g + 1, buf, 1 - slot, sem):
                c.start()

        for c in copies(b, g, buf, slot, sem):
            c.wait()
        acc, car = sweep(buf, slot, acc, car)
        return g + 1, jnp.max(car), acc, car

    g, _, acc, _ = lax.while_loop(more, body, (1, jnp.max(car), acc, car))

    @pl.when(g < n_grp)
    def _():
        for c in copies(b, g, buf, g % 2, sem):
            c.wait()

    o_ref[...] = jnp.sum(jnp.where(head_lanes, acc, 0.0), axis=0, keepdims=True)


def sb_sample(page_table, qb, cache_t, layer):
    n_seq = qb.shape[0]
    n_pages = page_table.shape[1]
    grp_buf = pltpu.VMEM((2, SB_GROUP, 2 * GROUP_W, PAGE), F32)
    return pl.pallas_call(
        functools.partial(_sb_sample_kernel, layer=layer, n_pages=n_pages),
        grid_spec=pltpu.PrefetchScalarGridSpec(
            num_scalar_prefetch=1, grid=(n_seq,),
            in_specs=[pl.BlockSpec((None, 1, GROUP_W), lambda b, pt: (b, 0, 0)),
                      pl.BlockSpec(memory_space=pl.ANY)],
            out_specs=pl.BlockSpec((None, 1, GROUP_W), lambda b, pt: (b, 0, 0)),
            scratch_shapes=[grp_buf, grp_buf, pltpu.SemaphoreType.DMA((2,)), pltpu.SemaphoreType.DMA((2,))]),
        out_shape=jax.ShapeDtypeStruct((n_seq, 1, GROUP_W), F32),
        compiler_params=_params(("arbitrary",)),
        name="sb_sample",
    )(page_table, qb, cache_t)


def _slab_copies(cache_ref, layer, pt_ref, b, n_pages, buf_ref, slot, sem_ref):
    return [pltpu.make_async_copy(cache_ref.at[layer, pt_ref[b, p]],
                                  buf_ref.at[slot, :, pl.ds(p * PAGE, PAGE)], sem_ref.at[slot])
            for p in range(n_pages)]


CMP_CHUNK = 256
TAIL_ROWS = 128
XPOSE_TOKENS = 1024
SLAB_CAST = 2048


def _nsa_cmp_sample_kernel(pt_ref, q_ref, new_ref, pe_ref, w_ref, ms_ref, cache_ref, ocmp_ref, sel_ref,
                           buf, sem, rows_sc, lo_sc, hi_sc, *, layer, n_pages):
    b = pl.program_id(0)
    slot = b % 2

    def copies(seq, sl):
        return _slab_copies(cache_ref, layer, pt_ref, seq, n_pages, buf, sl, sem)

    @pl.when(b == 0)
    def _():
        for c in copies(0, 0):
            c.start()

    @pl.when(b + 1 < pl.num_programs(0))
    def _():
        for c in copies(b + 1, 1 - slot):
            c.start()

    for c in copies(b, slot):
        c.wait()

    past = n_pages * PAGE
    n_pad = lo_sc.shape[0]
    for c in range(past // XPOSE_TOKENS):
        span = slice(c * XPOSE_TOKENS, (c + 1) * XPOSE_TOKENS)
        rows_sc[span, :] = buf[slot, :, span].T
    r_io = lax.broadcasted_iota(jnp.int32, (TAIL_ROWS, LANES), 0)
    rows_sc[past:past + TAIL_ROWS, :] = jnp.where(r_io == 0, jnp.broadcast_to(new_ref[...], (TAIL_ROWS, LANES)), 0.0)

    chunk = math.gcd(CMP_CHUNK, past // CMP_STRIDE)

    def cmp_chunk(c, carry):
        g0 = pl.multiple_of(c * chunk, chunk)
        lo, hi = _compress_rows(rows_sc, chunk, pe_ref, w_ref, base=g0 * CMP_STRIDE)
        lo_sc[pl.ds(g0, chunk), :] = lo
        hi_sc[pl.ds(g0, chunk), :] = hi
        return carry

    lax.fori_loop(0, past // (CMP_STRIDE * chunk), cmp_chunk, 0)
    n_tail = TAIL_ROWS // CMP_STRIDE
    g_tail = past // CMP_STRIDE
    lo, hi = _compress_rows(rows_sc, n_tail, pe_ref, w_ref, base=past)
    lo_sc[g_tail:g_tail + n_tail, :] = lo
    hi_sc[g_tail:g_tail + n_tail, :] = hi
    lo_sc[g_tail + n_tail:n_pad, :] = jnp.zeros((n_pad - g_tail - n_tail, LANES), F32)
    hi_sc[g_tail + n_tail:n_pad + 8, :] = jnp.zeros((n_pad + 8 - g_tail - n_tail, LANES), F32)
    kc = (lo_sc[...] + hi_sc[pl.ds(1, n_pad), :]).astype(BF)

    q8 = q_ref[...]
    qpos = past
    s = _dot_nt(q8, kc)
    blk_end = lax.broadcasted_iota(jnp.int32, (1, n_pad), 1) * CMP_STRIDE + (CMP_LEN - 1)
    p = _softmax_rows(s, blk_end <= qpos)
    ocmp_ref[...] = _dot(p.astype(BF), kc)
    ps = jnp.broadcast_to(jnp.sum(p[0:N_HEADS], axis=0, keepdims=True), (8, n_pad))
    ms = ms_ref[...]
    p_slc = sum(_dot(part, ms) for part in _split3(ps))
    j_io = lax.broadcasted_iota(jnp.int32, p_slc.shape, 1)
    cur = qpos // SEL_BLOCK
    forced = (j_io == 0) | (j_io == cur) | (j_io == cur - 1)
    valid = j_io <= cur
    score = jnp.where(valid, jnp.where(forced, FORCE_SCORE, p_slc), NEG)
    sel_ref[...] = _topk_select(score, valid, N_TOPK, axis=1)


def nsa_cmp_sample(page_table, q8, new_cmp, pe, w_cmp, ms, cache_t, layer):
    n_seq = q8.shape[0]
    n_pages = page_table.shape[1]
    n_pad, n_sel_pad = ms.shape
    seq_spec = lambda w: pl.BlockSpec((None, 8, w), lambda b, pt: (b, 0, 0))
    const2 = lambda shape: pl.BlockSpec(shape, lambda b, pt: (0,) * len(shape))
    return pl.pallas_call(
        functools.partial(_nsa_cmp_sample_kernel, layer=layer, n_pages=n_pages),
        grid_spec=pltpu.PrefetchScalarGridSpec(
            num_scalar_prefetch=1, grid=(n_seq,),
            in_specs=[seq_spec(LANES), pl.BlockSpec((None, 1, LANES), lambda b, pt: (b, 0, 0)),
                      const2((8, CMP_STRIDE * LANES)), const2((CMP_STRIDE * LANES, 2 * LANES)),
                      const2((n_pad, n_sel_pad)),
                      pl.BlockSpec(memory_space=pl.ANY)],
            out_specs=[seq_spec(LANES), seq_spec(n_sel_pad)],
            scratch_shapes=[pltpu.VMEM((2, LANES, n_pages * PAGE), F32), pltpu.SemaphoreType.DMA((2,)),
                            pltpu.VMEM((n_pages * PAGE + TAIL_ROWS, LANES), F32),
                            pltpu.VMEM((n_pad, LANES), F32), pltpu.VMEM((n_pad + 8, LANES), F32)]),
        out_shape=[jax.ShapeDtypeStruct((n_seq, 8, LANES), F32), jax.ShapeDtypeStruct((n_seq, 8, n_sel_pad), F32)],
        compiler_params=_params(("arbitrary",)),
        name="nsa_cmp_sample",
    )(page_table, q8, new_cmp, pe, w_cmp, ms, cache_t)


def _nsa_sel_sample_kernel(pt_ref, q_ref, sel_ref, ocmp_ref, gate_ref, news_ref, neww_ref, e_ref, win_ref,
                           cache_ref, o_ref, buf, sem, kv_sc, *, layer, n_pages):
    b = pl.program_id(0)
    slot = b % 2

    def copies(seq, sl):
        return _slab_copies(cache_ref, layer, pt_ref, seq, n_pages, buf, sl, sem)

    @pl.when(b == 0)
    def _():
        for c in copies(0, 0):
            c.start()

    @pl.when(b + 1 < pl.num_programs(0))
    def _():
        for c in copies(b + 1, 1 - slot):
            c.start()

    for c in copies(b, slot):
        c.wait()

    past = n_pages * PAGE
    qpos = past
    q8 = q_ref[...]
    q8f = q8.astype(F32)
    sel = sel_ref[...]
    cast = math.gcd(SLAB_CAST, past)
    for c in range(past // cast):
        span = slice(c * cast, (c + 1) * cast)
        kv_sc[:, span] = buf[slot, :, span].astype(BF)
    kvt = kv_sc[...]

    def new_token(new_ref, allowed):
        kn = new_ref[...].astype(BF).astype(F32)
        s_new = jnp.sum(q8f * kn, axis=-1, keepdims=True)
        return kn, jnp.where(allowed, s_new, NEG)

    blk = e_ref.shape[0]
    sel_b = sel.astype(BF)
    selx = jnp.concatenate([_dot(sel_b[:, c * blk:(c + 1) * blk], e_ref[...])
                            for c in range(past // (blk * SEL_BLOCK))], axis=1)
    allow = (selx > 0.5) & (lax.broadcasted_iota(jnp.int32, (1, past), 1) <= qpos)
    sm = jnp.where(allow, _dot(q8, kvt), NEG)
    kn_s, s_new = new_token(news_ref, (sel[:, qpos // SEL_BLOCK:qpos // SEL_BLOCK + 1] > 0.5))
    m = jnp.maximum(jnp.max(sm, axis=-1, keepdims=True), s_new)
    e = jnp.where(allow, jnp.exp(sm - m), 0.0)
    e_new = jnp.where(s_new > 0.5 * NEG, jnp.exp(s_new - m), 0.0)
    l_sel = jnp.sum(e, axis=-1, keepdims=True) + e_new
    o_sel = (_dot_nt(e.astype(BF), kvt) + e_new * kn_s) / jnp.maximum(l_sel, 1e-30)

    n_buf = win_ref.shape[1]
    wb = win_ref[...].astype(BF)
    kwpos = past - n_buf + lax.broadcasted_iota(jnp.int32, (1, n_buf), 1)
    dist = qpos - kwpos
    allow_w = (dist >= 0) & (dist <= WINDOW) & (kwpos >= 0)
    sw = jnp.where(allow_w, _dot(q8, wb), NEG)
    kn_w, sw_new = new_token(neww_ref, True)
    mw = jnp.maximum(jnp.max(sw, axis=-1, keepdims=True), sw_new)
    ew = jnp.where(allow_w, jnp.exp(sw - mw), 0.0)
    ew_new = jnp.exp(sw_new - mw)
    l_w = jnp.sum(ew, axis=-1, keepdims=True) + ew_new
    o_win = (_dot_nt(ew.astype(BF), wb) + ew_new * kn_w) / jnp.maximum(l_w, 1e-30)

    g = jnp.broadcast_to(gate_ref[...], (8, LANES))
    g_row = lax.broadcasted_iota(jnp.int32, (8, LANES), 0)
    g_lane = lax.broadcasted_iota(jnp.int32, (8, LANES), 1)
    gk = [jnp.sum(jnp.where(g_lane == 3 * g_row + k, g, 0.0), axis=-1, keepdims=True) for k in range(3)]
    o_ref[...] = gk[0] * ocmp_ref[...] + gk[1] * o_sel + gk[2] * o_win


def nsa_sel_sample(page_table, q8, sel, o_cmp, gate, new_sel, new_win, e_mat, win_t, cache_t, layer):
    n_seq = q8.shape[0]
    n_pages = page_table.shape[1]
    n_buf = win_t.shape[3]
    seq_spec = lambda r, w: pl.BlockSpec((None, r, w), lambda b, pt: (b, 0, 0))
    return pl.pallas_call(
        functools.partial(_nsa_sel_sample_kernel, layer=layer, n_pages=n_pages),
        grid_spec=pltpu.PrefetchScalarGridSpec(
            num_scalar_prefetch=1, grid=(n_seq,),
            in_specs=[seq_spec(8, LANES), seq_spec(8, sel.shape[2]), seq_spec(8, LANES), seq_spec(1, LANES),
                      seq_spec(1, LANES), seq_spec(1, LANES),
                      pl.BlockSpec(e_mat.shape, lambda b, pt: (0, 0)),
                      pl.BlockSpec((None, None, LANES, n_buf), lambda b, pt: (layer, b, 0, 0)),
                      pl.BlockSpec(memory_space=pl.ANY)],
            out_specs=seq_spec(8, LANES),
            scratch_shapes=[pltpu.VMEM((2, LANES, n_pages * PAGE), F32), pltpu.SemaphoreType.DMA((2,)),
                            pltpu.VMEM((LANES, n_pages * PAGE), BF)]),
        out_shape=jax.ShapeDtypeStruct((n_seq, 8, LANES), F32),
        compiler_params=_params(("arbitrary",)),
        name="nsa_sel_sample",
    )(page_table, q8, sel, o_cmp, gate, new_sel, new_win, e_mat, win_t, cache_t)


def _gmlp_pool_sample_kernel(u_ref, v_ref, pin_ref, hist_ref, w00_ref, b0_ref, pw_ref, ps_ref, oc_ref, od_ref,
                             *, past_len):
    oc_ref[...] = u_ref[...] * (w00_ref[...] * v_ref[...] + b0_ref[...])
    x = pin_ref[...]
    lane = lax.broadcasted_iota(jnp.int32, (1, GROUP_W), 1)
    grp = lane // HEAD_DIM
    sums = []
    tot = x
    k = 1
    for wlen in POOL_WINDOWS:
        while k < wlen:
            tot = tot + hist_ref[POOL_BUF - k]
            k += 1
        sums.append(tot)
    tot = jnp.where(grp == 0, sums[0], jnp.where(grp == 1, sums[1], jnp.where(grp == 2, sums[2], sums[3])))
    wlen = jnp.where(grp == 0, POOL_WINDOWS[0], jnp.where(grp == 1, POOL_WINDOWS[1],
                                                          jnp.where(grp == 2, POOL_WINDOWS[2], POOL_WINDOWS[3])))
    cnt = jnp.minimum(wlen, past_len + 1).astype(F32)
    d = tot / cnt - x
    od_ref[...] = _dot(d.astype(BF), pw_ref[...]) * ps_ref[...]


def gmlp_pool_sample(u, v, pin, hist_t, layer, w00, b0, pool_w_bd, pool_scale, past_len):
    n_seq = u.shape[0]
    full = lambda shape: pl.BlockSpec(shape, lambda i: (0,) * len(shape))
    return pl.pallas_call(
        functools.partial(_gmlp_pool_sample_kernel, past_len=past_len),
        grid=(1,),
        in_specs=[full((n_seq, GROUP_W))] * 3 +
                 [pl.BlockSpec((None, POOL_BUF, n_seq, GROUP_W), lambda i: (layer, 0, 0, 0)),
                  full((1, GROUP_W)), full((1, GROUP_W)), full((GROUP_W, GROUP_W)), full((1, GROUP_W))],
        out_specs=[full((n_seq, GROUP_W))] * 2,
        out_shape=[jax.ShapeDtypeStruct((n_seq, GROUP_W), F32)] * 2,
        compiler_params=_params(("arbitrary",)),
        name="gmlp_pool_sample",
    )(u, v, pin, hist_t, w00, b0, pool_w_bd, pool_scale)


def sample_layer(x, mod, lw, tables, sel_consts, caches_t, layer, page_table):
    n_seq = x.shape[1]
    past_len = page_table.shape[1] * PAGE
    cmp_t, sel_t, win_t, sb_t, pool_t = caches_t
    ms, e_mat = sel_consts
    m = _mods(mod, (1, n_seq, D_MODEL))
    g = lw["norm_g"]
    x = ffn_half(x, m[0], m[1], m[2], g[0], g[1], lw["w_up"], lw["w_down"], (layer, 0), n_seq)
    pr = in_proj(x, m[3], m[4], g[2], tables[0], tables[1], lw["w_in"], lw["ln_g"], lw["ln_b"], n_seq)
    rows = {k: v[0] for k, v in pr.items()}
    q8 = jnp.pad(rows["qa"].reshape(n_seq, N_HEADS, LANES), ((0, 0), (0, 8 - N_HEADS), (0, 0)))
    o_cmp, sel = nsa_cmp_sample(page_table, q8, rows["kv_cmp"][:, None, :], lw["cmp_pe"], lw["cmp_w"], ms,
                                cmp_t, layer)
    o8 = nsa_sel_sample(page_table, q8, sel, o_cmp, rows["gate"][:, None, :], rows["kv_sel"][:, None, :],
                        rows["kv_win"][:, None, :], e_mat, win_t, sel_t, layer)
    o_a = o8[:, :N_HEADS, HEAD_DIM:].reshape(1, n_seq, GROUP_W).astype(BF)
    o_b = sb_sample(page_table, rows["qb"][:, None, :], sb_t, layer).reshape(1, n_seq, GROUP_W).astype(BF)
    o_c, o_d = gmlp_pool_sample(rows["u"], rows["v"], rows["pin"], pool_t, layer, lw["sg_w00"], lw["sg_b0"],
                                lw["pool_w"], lw["pool_scale"], past_len)
    x = out_proj(x, o_a, o_b, o_c[None].astype(BF), o_d[None].astype(BF), m[5], g[3], lw["w_out"], layer, n_seq)
    x = ffn_half(x, m[6], m[7], m[8], g[4], g[5], lw["w_up"], lw["w_down"], (layer, 1), n_seq)
    state = (rows["kv_cmp"], rows["kv_sel"], rows["kv_win"], rows["kvb"], rows["pin"], rows["v"])
    return x, state


PROMPT_TM = 512


def kernel(x_prompt, x_sample, cache_nsa_cmp, cache_nsa_sel, cache_nsa_win, cache_sb, state_pool, page_table,
           c_prompt, c_sample, w_ada, b_ada, norm_g, w_ffn_up, w_ffn_down, w_in, w_out,
           cmp_pe, cmp_w, sg_ln_g, sg_ln_b, sg_w, sg_b, pool_w, pool_scale):
    n_p, t_len, _ = x_prompt.shape
    n_s, t_dec, _ = x_sample.shape
    assert t_dec == 1, "the sample step advances one token per sequence"
    depth = w_ada.shape[0]
    n_phys = cache_sb.shape[1]
    past_len = page_table.shape[1] * PAGE
    c_all = jnp.concatenate([c_prompt, c_sample], axis=0)
    c_all = jnp.pad(c_all, ((0, (-c_all.shape[0]) % 8), (0, 0)))
    tables_p = _rope_tables(jnp.arange(t_len))
    tables_s = _rope_tables(jnp.full((n_s,), past_len))
    consts = _sel_constants(t_len // CMP_STRIDE, t_len // SEL_BLOCK, t_len)
    n_rows = -(-(past_len + t_dec) // SEL_BLOCK) * SEL_BLOCK
    n_cmp = n_rows // CMP_STRIDE - 1
    n_sel = (n_cmp + 1) // (SEL_BLOCK // CMP_STRIDE)
    ms = _sample_sel_weights(-(-((past_len + TAIL_ROWS) // CMP_STRIDE) // LANES) * LANES,
                             -(-n_sel // LANES) * LANES, n_cmp, n_sel)
    blk_s = math.gcd(LANES, past_len // SEL_BLOCK)
    sel_consts_s = (ms, _sel_constants(1, blk_s, blk_s * SEL_BLOCK)[1])
    caches_t = (jnp.transpose(cache_nsa_cmp, (0, 1, 3, 4, 2)).reshape(depth, n_phys, LANES, PAGE),
                jnp.transpose(cache_nsa_sel, (0, 1, 3, 4, 2)).reshape(depth, n_phys, LANES, PAGE),
                jnp.transpose(cache_nsa_win, (0, 1, 3, 4, 2)).reshape(depth, n_s, LANES, -1),
                jnp.transpose(cache_sb, (0, 1, 3, 4, 5, 2)).reshape(depth, n_phys, 2 * GROUP_W, PAGE),
                jnp.transpose(state_pool, (0, 2, 1, 3)))
    y_p, y_s = x_prompt, x_sample.reshape(1, n_s, D_MODEL)
    st_p, st_s = [], []
    w_ada, w_ffn_up, w_ffn_down, w_out = (w.astype(BF) for w in (w_ada, w_ffn_up, w_ffn_down, w_out))
    for l in range(depth):
        lw = _layer_weights(l, w_ada, b_ada, norm_g, w_ffn_up, w_ffn_down, w_in, w_out,
                            cmp_pe, cmp_w, sg_ln_g, sg_ln_b, sg_w, sg_b, pool_w, pool_scale)
        mod = ada_mod(c_all, lw["w_ada"], lw["b_ada"], l)
        y_p, s = prompt_layer(y_p, mod[:n_p], lw, tables_p, consts, PROMPT_TM)
        st_p.append(s)
        y_s, s = sample_layer(y_s, mod[n_p:n_p + n_s], lw, tables_s, sel_consts_s, caches_t, l, page_table)
        st_s.append(s)

    def stack(states, i):
        return jnp.stack([s[i] for s in states], axis=0)

    def kv_rows(x_t):
        d, bsz, _, t = x_t.shape
        return jnp.transpose(x_t.reshape(d, bsz, 2, HEAD_DIM, t), (0, 1, 4, 2, 3))

    sb_p = stack(st_p, 3)
    sb_p = jnp.transpose(sb_p.reshape(depth, n_p, 2, N_HEADS, HEAD_DIM, t_len), (0, 1, 5, 2, 3, 4))
    win_new = stack(st_s, 2).reshape(depth, n_s, 1, 2, HEAD_DIM)
    pool_new = stack(st_s, 4)[:, :, None, :]
    return (y_p, y_s.reshape(n_s, 1, D_MODEL),
            kv_rows(stack(st_p, 0)), stack(st_s, 0).reshape(depth, n_s, 1, 2, HEAD_DIM),
            kv_rows(stack(st_p, 1)), stack(st_s, 1).reshape(depth, n_s, 1, 2, HEAD_DIM),
            kv_rows(stack(st_p, 2)), jnp.concatenate([cache_nsa_win[:, :, 1:], win_new], axis=2),
            sb_p, stack(st_s, 3).reshape(depth, n_s, 1, 2, N_HEADS, HEAD_DIM),
            stack(st_p, 4), jnp.concatenate([state_pool[:, :, 1:], pool_new], axis=2),
            stack(st_s, 5)[:, :, None, :])
```

```python
import functools
import math

import numpy as np
import jax
import jax.numpy as jnp
from jax import lax
from jax.experimental import pallas as pl
from jax.experimental.pallas import tpu as pltpu

F32 = jnp.float32
BF = jnp.bfloat16

D_MODEL = 1024
HEAD_DIM = 64
N_HEADS = 4
GROUP_W = N_HEADS * HEAD_DIM
D_FF = 2816
N_MOD = 9
PAGE = 128
CMP_STRIDE = 16
CMP_LEN = 32
SEL_BLOCK = 64
N_TOPK = 16
WINDOW = 512
CHUNK = 128
POOL_WINDOWS = (2, 4, 8, 16)
POOL_MAX = 16
POOL_BUF = POOL_MAX - 1
ROPE_THETA = 10000.0
EPS = 1e-6
FORCE_SCORE = 1e9
NEG = -3.0e38
MASK_BIAS = 1.0e30
LANES = 128
VMEM_LIMIT = 56 * 1024 * 1024

C_QA, C_QAS, C_KV, C_KVS, C_G, C_QB, C_KVB, C_UV, C_DIN, C_END = (
    0, 512, 1024, 1408, 1792, 1920, 2176, 2688, 3200, 3456)


def _params(sem, vmem=VMEM_LIMIT):
    return pltpu.CompilerParams(dimension_semantics=sem, vmem_limit_bytes=vmem)


def _dot(a, b):
    return jnp.dot(a, b, preferred_element_type=F32)


def _dot_nt(a, b):
    return lax.dot_general(a, b, (((1,), (1,)), ((), ())), preferred_element_type=F32)


def _rms(x, g):
    return x * lax.rsqrt(jnp.mean(x * x, axis=-1, keepdims=True) + EPS) * g


def _split3(x):
    hi = x.astype(BF)
    r = x - hi.astype(F32)
    mid = r.astype(BF)
    lo = (r - mid.astype(F32)).astype(BF)
    return hi, mid, lo


def _ada_kernel(c_ref, w_ref, b_ref, o_ref):
    c = c_ref[...]
    s = (c * jax.nn.sigmoid(c)).astype(BF)
    o_ref[...] = _dot(s, w_ref[...]) + b_ref[...]


def _stacked_spec(tail, idx):
    return pl.BlockSpec((None,) * len(idx) + tail, lambda *_: idx + (0,) * len(tail))


def ada_mod(c_all, w_ada, b_ada, layer):
    r = c_all.shape[0]
    n = w_ada.shape[2]
    tn = 2304
    return pl.pallas_call(
        _ada_kernel,
        grid=(n // tn,),
        in_specs=[pl.BlockSpec((r, D_MODEL), lambda j: (0, 0)),
                  pl.BlockSpec((None, D_MODEL, tn), lambda j: (layer, 0, j)),
                  pl.BlockSpec((1, tn), lambda j: (0, j))],
        out_specs=pl.BlockSpec((r, tn), lambda j: (0, j)),
        out_shape=jax.ShapeDtypeStruct((r, n), F32),
        compiler_params=_params(("parallel",)),
        name="ada_mod",
    )(c_all, w_ada, b_ada)


FF_CHUNK = 256


def _ffn_kernel(x_ref, sh_ref, sc_ref, gt_ref, g1_ref, g2_ref, wu_ref, wd_ref, o_ref):
    o_ref[...] = _swiglu_half(x_ref[...], sh_ref, sc_ref, gt_ref, g1_ref, g2_ref, wu_ref, wd_ref)


def _swiglu_half(x, sh_ref, sc_ref, gt_ref, g1_ref, g2_ref, wu_ref, wd_ref):
    h = _rms(x, g1_ref[...]) * (1.0 + sc_ref[...]) + sh_ref[...]
    hb = h.astype(BF)
    acc = jnp.zeros(x.shape, F32)
    for c in range(D_FF // FF_CHUNK):
        lo = c * FF_CHUNK
        gate = _dot(hb, wu_ref[:, lo:lo + FF_CHUNK])
        up = _dot(hb, wu_ref[:, D_FF + lo:D_FF + lo + FF_CHUNK])
        a = (gate * jax.nn.sigmoid(gate) * up).astype(BF)
        acc = acc + _dot(a, wd_ref[lo:lo + FF_CHUNK, :])
    return x + 0.5 * gt_ref[...] * _rms(acc, g2_ref[...])


def _mod_spec(mod, tm):
    if mod.shape[1] == 1:
        return pl.BlockSpec((None, 1, D_MODEL), lambda b, t: (b, 0, 0))
    return pl.BlockSpec((None, tm, D_MODEL), lambda b, t: (b, t, 0))


def _row_spec(tm, w):
    return pl.BlockSpec((None, tm, w), lambda b, t: (b, t, 0))


def _const_spec(shape):
    nd = len(shape)
    return pl.BlockSpec(shape, lambda b, t: (0,) * nd)


def ffn_half(x, shift, scale, gate, g1, g2, w_up, w_down, widx, tm):
    bsz, t_len, _ = x.shape
    return pl.pallas_call(
        _ffn_kernel,
        grid=(bsz, t_len // tm),
        in_specs=[_row_spec(tm, D_MODEL), _mod_spec(shift, tm), _mod_spec(scale, tm), _mod_spec(gate, tm),
                  _const_spec((1, D_MODEL)), _const_spec((1, D_MODEL)),
                  _stacked_spec((D_MODEL, 2 * D_FF), widx), _stacked_spec((D_FF, D_MODEL), widx)],
        out_specs=_row_spec(tm, D_MODEL),
        out_shape=jax.ShapeDtypeStruct(x.shape, F32),
        compiler_params=_params(("parallel", "parallel")),
        name="ffn_half",
    )(x, shift, scale, gate, g1, g2, w_up, w_down)


def _gelu_tanh(x):
    return 0.5 * x * (1.0 + jnp.tanh(np.sqrt(2.0 / np.pi).astype(np.float32) * (x + 0.044715 * (x * x * x))))


PROJ_IN = ("x", "shift", "scale", "g", "cos", "sin", "w", "ln_g", "ln_b")
PROJ_IN_T = ("w_t", "cos_t", "sin_t")
PROJ_OUT = (("qa", 2 * GROUP_W, BF), ("kv_cmp", LANES, F32), ("gate", LANES, F32), ("qb", GROUP_W, BF),
            ("u", GROUP_W, F32), ("v", GROUP_W, F32), ("pin", GROUP_W, F32))
PROJ_OUT_ROWS = (("kv_sel", LANES, F32), ("kv_win", LANES, F32), ("kvb", 2 * GROUP_W, F32))
PROJ_OUT_B16 = (("kv_sel_b", LANES, BF), ("kv_win_b", LANES, BF), ("kvb_b", 2 * GROUP_W, BF))
PROJ_OUT_T = (("kvt_cmp", LANES), ("kvt_sel", LANES), ("kvt_win", LANES), ("kvbt", 2 * GROUP_W))


def _proj_kernel(*refs, feature_major):
    names = PROJ_IN + (PROJ_IN_T if feature_major else ())
    names += tuple(n for n, _, _ in PROJ_OUT + (PROJ_OUT_B16 if feature_major else PROJ_OUT_ROWS))
    names += tuple(n for n, _ in PROJ_OUT_T) if feature_major else ()
    r = dict(zip(names, refs))
    x = r["x"][...]
    h = _rms(x, r["g"][...]) * (1.0 + r["scale"][...]) + r["shift"][...]
    hb = h.astype(BF)
    cos = r["cos"][...]
    sin = r["sin"][...]
    w_ref = r["w"]

    def mm(lo, hi):
        return _dot(hb, w_ref[:, lo:hi])

    def cols(a, base, lo, hi):
        return a[:, lo - base:hi - base]

    q_p, q_ps = mm(C_QA, C_QAS), mm(C_QAS, C_KV)
    for j in range(N_HEADS):
        span = slice(LANES * j, LANES * (j + 1))
        r["qa"][:, span] = (q_p[:, span] * cos + q_ps[:, span] * sin).astype(BF)
    lane = lax.broadcasted_iota(jnp.int32, cos.shape, 1)
    ckv = jnp.where(lane < HEAD_DIM, cos, 1.0)
    kvg = mm(C_KV, C_QB)
    for j, nm in enumerate(("kv_cmp", "kv_sel", "kv_win")):
        p = cols(kvg, C_KV, C_KV + LANES * j, C_KV + LANES * (j + 1))
        ps = cols(kvg, C_KV, C_KVS + LANES * j, C_KVS + LANES * (j + 1))
        kv = p * ckv + ps * sin
        if nm in r:
            r[nm][...] = kv
        if nm + "_b" in r:
            r[nm + "_b"][...] = kv.astype(BF)
    r["gate"][...] = jax.nn.sigmoid(cols(kvg, C_KV, C_G, C_QB))
    rest = mm(C_QB, C_END)
    r["qb"][...] = cols(rest, C_QB, C_QB, C_KVB).astype(BF)
    kvb = cols(rest, C_QB, C_KVB, C_UV)
    if feature_major:
        r["kvb_b"][...] = kvb.astype(BF)
    else:
        r["kvb"][...] = kvb
    r["u"][...] = _gelu_tanh(cols(rest, C_QB, C_UV, C_UV + GROUP_W))
    v = _gelu_tanh(cols(rest, C_QB, C_UV + GROUP_W, C_DIN))
    vc = v - jnp.mean(v, axis=-1, keepdims=True)
    vn = vc * lax.rsqrt(jnp.mean(vc * vc, axis=-1, keepdims=True) + EPS)
    r["v"][...] = vn * r["ln_g"][...] + r["ln_b"][...]
    r["pin"][...] = cols(rest, C_QB, C_DIN, C_END)
    if feature_major:
        wt_ref = r["w_t"]
        cos_t = r["cos_t"][...]
        sin_t = r["sin_t"][...]
        n_kv = 3 * LANES
        for j, nm in enumerate(("kvt_cmp", "kvt_sel", "kvt_win")):
            p = _dot_nt(wt_ref[LANES * j:LANES * (j + 1), :], hb)
            ps = _dot_nt(wt_ref[n_kv + LANES * j:n_kv + LANES * (j + 1), :], hb)
            r[nm][...] = p * cos_t + ps * sin_t
        r["kvbt"][...] = _dot_nt(wt_ref[2 * n_kv:2 * n_kv + 2 * GROUP_W, :], hb)


def in_proj(x, shift, scale, g, cos, sin, w_all, ln_g, ln_b, tm, t_side=None):
    bsz, t_len, _ = x.shape
    feature_major = t_side is not None
    tab_spec = pl.BlockSpec((tm, LANES), lambda b, t: (t, 0))
    in_specs = [_row_spec(tm, D_MODEL), _mod_spec(shift, tm), _mod_spec(scale, tm),
                _const_spec((1, D_MODEL)), tab_spec, tab_spec,
                _const_spec((D_MODEL, C_END)), _const_spec((1, GROUP_W)), _const_spec((1, GROUP_W))]
    args = [x, shift, scale, g, cos, sin, w_all, ln_g, ln_b]
    outs = PROJ_OUT + (PROJ_OUT_B16 if feature_major else PROJ_OUT_ROWS)
    out_specs = [_row_spec(tm, w) for _, w, _ in outs]
    out_shape = [jax.ShapeDtypeStruct((bsz, t_len, w), dt) for _, w, dt in outs]
    names = [n for n, _, _ in outs]
    if feature_major:
        tab_t_spec = pl.BlockSpec((LANES, tm), lambda b, t: (0, t))
        in_specs += [_const_spec(t_side[0].shape), tab_t_spec, tab_t_spec]
        args += list(t_side)
        out_specs += [pl.BlockSpec((None, w, tm), lambda b, t: (b, 0, t)) for _, w in PROJ_OUT_T]
        out_shape += [jax.ShapeDtypeStruct((bsz, w, t_len), F32) for _, w in PROJ_OUT_T]
        names += [n for n, _ in PROJ_OUT_T]
    res = pl.pallas_call(
        functools.partial(_proj_kernel, feature_major=feature_major),
        grid=(bsz, t_len // tm),
        in_specs=in_specs, out_specs=out_specs, out_shape=out_shape,
        compiler_params=_params(("parallel", "parallel")),
        name="in_proj",
    )(*args)
    return dict(zip(names, res))


def _outproj_ffn_kernel(x_ref, oa_ref, ob_ref, oc_ref, od_ref, gto_ref, go_ref, wo_ref,
                        sh_ref, sc_ref, gt_ref, g1_ref, g2_ref, wu_ref, wd_ref, o_ref):
    y = _dot(oa_ref[...], wo_ref[0:GROUP_W, :])
    y = y + _dot(ob_ref[...], wo_ref[GROUP_W:2 * GROUP_W, :])
    y = y + _dot(oc_ref[...], wo_ref[2 * GROUP_W:3 * GROUP_W, :])
    y = y + _dot(od_ref[...], wo_ref[3 * GROUP_W:4 * GROUP_W, :])
    x = x_ref[...] + gto_ref[...] * _rms(y, go_ref[...])
    o_ref[...] = _swiglu_half(x, sh_ref, sc_ref, gt_ref, g1_ref, g2_ref, wu_ref, wd_ref)


def out_proj_ffn(x, mix, gate_o, g_o, w_out, layer, shift, scale, gate, g1, g2, w_up, w_down, widx, tm):
    bsz, t_len, _ = x.shape
    return pl.pallas_call(
        _outproj_ffn_kernel,
        grid=(bsz, t_len // tm),
        in_specs=[_row_spec(tm, D_MODEL)] + [_row_spec(tm, GROUP_W)] * 4 +
                 [_mod_spec(gate_o, tm), _const_spec((1, D_MODEL)), _stacked_spec((D_MODEL, D_MODEL), (layer,)),
                  _mod_spec(shift, tm), _mod_spec(scale, tm), _mod_spec(gate, tm),
                  _const_spec((1, D_MODEL)), _const_spec((1, D_MODEL)),
                  _stacked_spec((D_MODEL, 2 * D_FF), widx), _stacked_spec((D_FF, D_MODEL), widx)],
        out_specs=_row_spec(tm, D_MODEL),
        out_shape=jax.ShapeDtypeStruct(x.shape, F32),
        compiler_params=_params(("parallel", "parallel")),
        name="out_proj_ffn",
    )(x, *mix, gate_o, g_o, w_out, shift, scale, gate, g1, g2, w_up, w_down)


def _compress_rows(row_ref, n_grp, pe_ref, w_ref, base=0):
    parts = [row_ref[pl.ds(base + l, n_grp, stride=CMP_STRIDE), :].astype(BF) for l in range(CMP_STRIDE)]
    both = _dot(jnp.concatenate(parts, axis=1), w_ref[...])
    pe_w = _dot(pe_ref[...], w_ref[...])
    return both[:, :LANES] + pe_w[0:1, :LANES], both[:, LANES:] + pe_w[1:2, LANES:]


def _compress_kernel(row_ref, pe_ref, w_ref, o_ref, hi_sc):
    n_grp = o_ref.shape[0]
    lo, hi = _compress_rows(row_ref, n_grp, pe_ref, w_ref)
    hi_sc[0:n_grp, :] = hi
    hi_sc[n_grp:n_grp + 8, :] = jnp.zeros((8, LANES), F32)
    o_ref[...] = (lo + hi_sc[pl.ds(1, n_grp), :]).astype(BF)


def compress_prompt(kv_cmp, pe, w_cmp):
    bsz, t_len, _ = kv_cmp.shape
    n_grp = t_len // CMP_STRIDE
    return pl.pallas_call(
        _compress_kernel,
        grid=(bsz,),
        in_specs=[pl.BlockSpec((None, t_len, LANES), lambda b: (b, 0, 0)),
                  pl.BlockSpec((8, CMP_STRIDE * LANES), lambda b: (0, 0)),
                  pl.BlockSpec((CMP_STRIDE * LANES, 2 * LANES), lambda b: (0, 0))],
        out_specs=pl.BlockSpec((None, n_grp, LANES), lambda b: (b, 0, 0)),
        out_shape=jax.ShapeDtypeStruct((bsz, n_grp, LANES), BF),
        scratch_shapes=[pltpu.VMEM((n_grp + 8, LANES), F32)],
        compiler_params=_params(("parallel",)),
        name="nsa_compress",
    )(kv_cmp, pe, w_cmp)


NSA_Q = 256
NSA_KC = 512


def _softmax_bias(s, bias):
    sm = s + bias
    m = jnp.maximum(jnp.max(sm, axis=-1, keepdims=True), -0.5 * MASK_BIAS)
    return jnp.exp(sm - m)


def _softmax_rows(s, allow):
    sm = jnp.where(allow, s, NEG)
    m = jnp.max(sm, axis=-1, keepdims=True)
    e = jnp.where(allow, jnp.exp(sm - m), 0.0)
    return e / jnp.maximum(jnp.sum(e, axis=-1, keepdims=True), 1e-30)


def _topk_select(score, valid, n_pick, axis=0):
    n_blk = score.shape[axis]
    j_io = lax.broadcasted_iota(jnp.int32, score.shape, axis)
    sel = jnp.zeros(score.shape, F32)
    sc = score
    for _ in range(n_pick):
        m = jnp.max(sc, axis=axis, keepdims=True)
        idx = jnp.min(jnp.where(sc == m, j_io, n_blk), axis=axis, keepdims=True)
        pick = j_io == idx
        sel = jnp.where(pick, 1.0, sel)
        sc = jnp.where(pick, NEG, sc)
    return jnp.where(valid, sel, 0.0)


def _nsa_prompt_kernel(qa_ref, g_ref, kcmp_ref, ksel_ref, kwin_ref, mt_ref, e_ref, o_ref,
                       m_sc, acc_sc, sa_sc, sb_sc):
    i = pl.program_id(1)
    s0 = i * NSA_Q
    n_cmp = kcmp_ref.shape[0]
    n_sel = mt_ref.shape[0]
    qs = jnp.concatenate([qa_ref[:, LANES * h:LANES * (h + 1)] for h in range(N_HEADS)], axis=0)
    qpos = s0 + lax.broadcasted_iota(jnp.int32, (NSA_Q, 1), 0)

    def heads(x):
        return jnp.concatenate([x] * N_HEADS, axis=0)

    kc = kcmp_ref[...]
    blk_end = lax.broadcasted_iota(jnp.int32, (1, n_cmp), 1) * CMP_STRIDE + (CMP_LEN - 1)
    e = _softmax_bias(_dot_nt(qs, kc), heads(jnp.where(blk_end <= qpos, 0.0, -MASK_BIAS)))
    p = e * (1.0 / jnp.maximum(jnp.sum(e, axis=-1, keepdims=True), 1e-30))
    o_cmp = _dot(p.astype(BF), kc)
    ps = p[0:NSA_Q] + p[NSA_Q:2 * NSA_Q] + p[2 * NSA_Q:3 * NSA_Q] + p[3 * NSA_Q:4 * NSA_Q]
    mt = mt_ref[...]
    p_slc_t = sum(_dot_nt(mt, part) for part in _split3(ps))
    j_io = lax.broadcasted_iota(jnp.int32, (n_sel, NSA_Q), 0)
    cur = (s0 + lax.broadcasted_iota(jnp.int32, (1, NSA_Q), 1)) // SEL_BLOCK
    forced = (j_io == 0) | (j_io == cur) | (j_io == cur - 1)
    valid = j_io <= cur
    score_t = jnp.where(valid, jnp.where(forced, FORCE_SCORE, p_slc_t), NEG)
    sel = _topk_select(score_t, valid, min(N_TOPK, n_sel)).T.astype(BF)

    m_sc[...] = jnp.full(m_sc.shape, NEG, F32)
    acc_sc[...] = jnp.zeros(acc_sc.shape, F32)
    rep = NSA_KC // LANES
    key_lane = lax.broadcasted_iota(jnp.int32, (1, LANES), 1) < HEAD_DIM
    last = (s0 + NSA_Q - 1) // NSA_KC

    def chunk_start(c):
        return pl.multiple_of(jnp.minimum(c, last) * NSA_KC, NSA_KC)

    def scores(c, dst):
        dst[...] = _dot_nt(qs, ksel_ref[pl.ds(chunk_start(c), NSA_KC), :])

    def update(c, src):
        k0 = chunk_start(c)
        kv = ksel_ref[pl.ds(k0, NSA_KC), :]
        kv1 = jnp.where(key_lane, jnp.ones_like(kv), kv)
        selx = _dot(sel, e_ref[:, pl.ds(k0, NSA_KC)])
        tok = k0 + lax.broadcasted_iota(jnp.int32, (1, NSA_KC), 1)
        live = jnp.where(c <= last, 0.0, -MASK_BIAS)
        sm = src[...] + heads(jnp.where((selx > 0.5) & (tok <= qpos), live, -MASK_BIAS))
        m_old = m_sc[...]
        m_new = jnp.maximum(m_old, jnp.max(sm, axis=-1, keepdims=True))
        pe = jnp.exp(sm - jnp.concatenate([m_new] * rep, axis=1))
        acc_sc[...] = jnp.exp(m_old - m_new) * acc_sc[...] + _dot(pe.astype(BF), kv1)
        m_sc[...] = m_new

    scores(0, sa_sc)

    def chunk_pair(j, carry):
        scores(2 * j + 1, sb_sc)
        update(2 * j, sa_sc)
        scores(2 * j + 2, sa_sc)
        update(2 * j + 1, sb_sc)
        return carry

    lax.fori_loop(0, last // 2 + 1, chunk_pair, 0)

    n_win = WINDOW + NSA_Q
    w0 = pl.multiple_of(jnp.maximum(s0 - WINDOW, 0), NSA_Q)
    kvw = kwin_ref[pl.ds(w0, n_win), :]
    dist = qpos - (w0 + lax.broadcasted_iota(jnp.int32, (1, n_win), 1))
    ew = _softmax_bias(_dot_nt(qs, kvw), heads(jnp.where((dist >= 0) & (dist <= WINDOW), 0.0, -MASK_BIAS)))
    o_win = _dot(ew.astype(BF), jnp.where(key_lane, jnp.ones_like(kvw), kvw))

    g = g_ref[...]
    outs = []
    for h in range(N_HEADS):
        rows = slice(h * NSA_Q, (h + 1) * NSA_Q)
        acc = acc_sc[rows, :]
        win = o_win[rows]
        o = (g[:, 3 * h:3 * h + 1] * o_cmp[rows]
             + g[:, 3 * h + 1:3 * h + 2] * (acc / jnp.maximum(acc[:, 0:1], 1e-30))
             + g[:, 3 * h + 2:3 * h + 3] * (win / jnp.maximum(win[:, 0:1], 1e-30)))
        outs.append(o[:, HEAD_DIM:])
    o_ref[...] = jnp.concatenate(outs, axis=1).astype(BF)


def nsa_prompt(qa, gate, kcmp, ksel_b, kwin_b, mt, e_mat):
    bsz, t_len, _ = qa.shape
    n_cmp = kcmp.shape[1]
    n_sel = mt.shape[0]
    return pl.pallas_call(
        _nsa_prompt_kernel,
        grid=(bsz, t_len // NSA_Q),
        in_specs=[_row_spec(NSA_Q, 2 * GROUP_W), _row_spec(NSA_Q, LANES),
                  pl.BlockSpec((None, n_cmp, LANES), lambda b, t: (b, 0, 0)),
                  pl.BlockSpec((None, t_len, LANES), lambda b, t: (b, 0, 0)),
                  pl.BlockSpec((None, t_len, LANES), lambda b, t: (b, 0, 0)),
                  _const_spec((n_sel, n_cmp)), _const_spec((n_sel, t_len))],
        out_specs=_row_spec(NSA_Q, GROUP_W),
        out_shape=jax.ShapeDtypeStruct((bsz, t_len, GROUP_W), BF),
        scratch_shapes=[pltpu.VMEM((N_HEADS * NSA_Q, LANES), F32), pltpu.VMEM((N_HEADS * NSA_Q, LANES), F32),
                        pltpu.VMEM((N_HEADS * NSA_Q, NSA_KC), F32), pltpu.VMEM((N_HEADS * NSA_Q, NSA_KC), F32)],
        compiler_params=_params(("parallel", "arbitrary")),
        name="nsa_prompt",
    )(qa, gate, kcmp, ksel_b, kwin_b, mt, e_mat)


SB_Q = 256
SB_K = 256


SB_STOP = -110.0


def _softplus(z):
    return jnp.maximum(z, 0.0) + jnp.log(1.0 + jnp.exp(-jnp.abs(z)))


def _tri_neg(n):
    r = lax.broadcasted_iota(jnp.int32, (2 * n, n), 0)
    c = lax.broadcasted_iota(jnp.int32, (2 * n, n), 1)
    r = jnp.where(r >= n, r - n, r)
    return jnp.where(r > c, -1.0, 0.0).astype(BF)


def _sb_weights(z, tri, carry, mask):
    n_k = z.shape[1]
    sp = _softplus(z)
    spm = sp if mask is None else jnp.where(mask, sp, 0.0)
    hi = spm.astype(BF)
    lo = (spm - hi.astype(F32)).astype(BF)
    cs = _dot(jnp.concatenate([hi, lo], axis=1), tri)
    after = cs + jnp.concatenate([carry] * (n_k // LANES), axis=1)
    a = jnp.exp(z - sp + after)
    if mask is not None:
        a = jnp.where(mask, a, 0.0)
    return a.astype(BF), carry + (cs[:, 0:1] - spm[:, 0:1])


def _sb_prompt_kernel(qb_ref, kvb_ref, o_ref, acc_sc, car_sc):
    i = pl.program_id(1)
    q = qb_ref[...]
    lane = lax.broadcasted_iota(jnp.int32, (1, GROUP_W), 1)
    qh = [jnp.where(lane // HEAD_DIM == h, q, jnp.zeros_like(q)) for h in range(N_HEADS)]
    tri = _tri_neg(SB_K)
    r = lax.broadcasted_iota(jnp.int32, (SB_Q, SB_K), 0)
    c = lax.broadcasted_iota(jnp.int32, (SB_Q, SB_K), 1)
    diag_mask = c < r

    def chunk(k0, mask, first):
        k = kvb_ref[pl.ds(k0, SB_K), 0:GROUP_W]
        v = kvb_ref[pl.ds(k0, SB_K), GROUP_W:2 * GROUP_W]
        car_max = None
        for h in range(N_HEADS):
            car = jnp.zeros((SB_Q, LANES), F32) if first else car_sc[h]
            a, car = _sb_weights(_dot_nt(qh[h], k), tri, car, mask)
            pv = _dot(a, v)
            acc_sc[h] = pv if first else acc_sc[h] + pv
            car_sc[h] = car
            car_max = car if car_max is None else jnp.maximum(car_max, car)
        return jnp.max(car_max)

    car_max = chunk(pl.multiple_of(i * SB_Q, SB_Q), diag_mask, True)

    def more(st):
        j, car_max = st
        return (j < i) & (car_max > SB_STOP)

    def body(st):
        j, _ = st
        return j + 1, chunk(pl.multiple_of((i - 1 - j) * SB_K, SB_K), None, False)

    lax.while_loop(more, body, (0, car_max))
    o = jnp.zeros((SB_Q, GROUP_W), F32)
    for h in range(N_HEADS):
        o = jnp.where(lane // HEAD_DIM == h, acc_sc[h], o)
    o_ref[...] = o.astype(BF)


def sb_prompt(qb, kvb_b):
    bsz, t_len, _ = qb.shape
    return pl.pallas_call(
        _sb_prompt_kernel,
        grid=(bsz, t_len // SB_Q),
        in_specs=[_row_spec(SB_Q, GROUP_W),
                  pl.BlockSpec((None, t_len, 2 * GROUP_W), lambda b, t: (b, 0, 0))],
        out_specs=_row_spec(SB_Q, GROUP_W),
        out_shape=jax.ShapeDtypeStruct((bsz, t_len, GROUP_W), BF),
        scratch_shapes=[pltpu.VMEM((N_HEADS, SB_Q, GROUP_W), F32),
                        pltpu.VMEM((N_HEADS, SB_Q, LANES), F32)],
        compiler_params=_params(("parallel", "arbitrary")),
        name="sb_prompt",
    )(qb, kvb_b)


def _pool_mix(ext_ref, tm, tpos, pool_w_ref, pool_s_ref):
    def shifted(ref, k):
        return ref[pl.ds(POOL_MAX - k, tm), :]
    x = shifted(ext_ref, 0)
    lane = lax.broadcasted_iota(jnp.int32, (1, GROUP_W), 1)
    grp = lane // HEAD_DIM
    s2 = x + shifted(ext_ref, 1)
    s4 = s2 + shifted(ext_ref, 2) + shifted(ext_ref, 3)
    s8 = s4 + sum(shifted(ext_ref, k) for k in range(4, 8))
    s16 = s8 + sum(shifted(ext_ref, k) for k in range(8, 16))
    tot = jnp.where(grp == 0, s2, jnp.where(grp == 1, s4, jnp.where(grp == 2, s8, s16)))
    wlen = jnp.where(grp == 0, 2, jnp.where(grp == 1, 4, jnp.where(grp == 2, 8, 16)))
    cnt = jnp.minimum(wlen, tpos + 1).astype(F32)
    d = tot / cnt - x
    return _dot(d.astype(BF), pool_w_ref[...]) * pool_s_ref[...]


def _gmlp_pool_kernel(u_ref, v_ref, pin_ref, halo_ref, sgw_ref, sgb_ref, pw_ref, ps_ref,
                      oc_ref, od_ref, ext_sc):
    t = pl.program_id(1)
    tm = u_ref.shape[0]
    lane = lax.broadcasted_iota(jnp.int32, (1, GROUP_W), 1)
    for c in range(tm // CHUNK):
        rows = slice(c * CHUNK, (c + 1) * CHUNK)
        v = v_ref[rows, :]
        vz = jnp.zeros_like(v)
        vst = jnp.concatenate([jnp.where(lane // HEAD_DIM == h, v, vz) for h in range(N_HEADS)], axis=0)
        s = _dot(sgw_ref[...], vst.astype(BF)) + sgb_ref[...]
        oc_ref[rows, :] = (u_ref[rows, :] * s).astype(BF)
    halo = halo_ref[...]
    ext_sc[0:POOL_MAX, :] = jnp.where(t > 0, halo, jnp.zeros_like(halo))
    ext_sc[POOL_MAX:POOL_MAX + tm, :] = pin_ref[...]
    tpos = t * tm + lax.broadcasted_iota(jnp.int32, (tm, 1), 0)
    od_ref[...] = _pool_mix(ext_sc, tm, tpos, pw_ref, ps_ref).astype(BF)


def gmlp_pool_prompt(u, v, pin, sgw_cat, sgb_full, pool_w_bd, pool_scale, tm):
    bsz, t_len, _ = u.shape
    per = tm // POOL_MAX
    halo_spec = pl.BlockSpec((None, POOL_MAX, GROUP_W), lambda b, t: (b, jnp.maximum(t * per - 1, 0), 0))
    return pl.pallas_call(
        _gmlp_pool_kernel,
        grid=(bsz, t_len // tm),
        in_specs=[_row_spec(tm, GROUP_W), _row_spec(tm, GROUP_W), _row_spec(tm, GROUP_W), halo_spec,
                  _const_spec((CHUNK, N_HEADS * CHUNK)), _const_spec((CHUNK, GROUP_W)),
                  _const_spec((GROUP_W, GROUP_W)), _const_spec((1, GROUP_W))],
        out_specs=[_row_spec(tm, GROUP_W), _row_spec(tm, GROUP_W)],
        out_shape=[jax.ShapeDtypeStruct((bsz, t_len, GROUP_W), BF)] * 2,
        scratch_shapes=[pltpu.VMEM((POOL_MAX + tm, GROUP_W), F32)],
        compiler_params=_params(("parallel", "parallel")),
        name="gmlp_pool",
    )(u, v, pin, pin, sgw_cat, sgb_full, pool_w_bd, pool_scale)


def _swap_neg(w):
    half = HEAD_DIM // 2
    w = w.reshape(w.shape[0], -1, 2, half)
    return jnp.stack([-w[:, :, 1], w[:, :, 0]], axis=2).reshape(w.shape[0], -1)


def _pad_heads(w):
    w = w.reshape(w.shape[0], -1, HEAD_DIM)
    return jnp.concatenate([w, jnp.zeros_like(w)], axis=2).reshape(w.shape[0], -1)


def _proj_weights(w_in):
    a_q, a_kv, a_g, b_qkv, c_uv, d_in = jnp.split(
        w_in, np.cumsum([GROUP_W, 6 * HEAD_DIM, 3 * N_HEADS, 3 * GROUP_W, 2 * GROUP_W]).tolist(), axis=1)
    scale = HEAD_DIM ** -0.5
    a_q = a_q * scale
    kv = a_kv.reshape(-1, 3, 2, HEAD_DIM)
    kv_sw = jnp.concatenate([_swap_neg(kv[:, :, 0].reshape(-1, 3 * HEAD_DIM)).reshape(-1, 3, 1, HEAD_DIM),
                             jnp.zeros_like(kv[:, :, 1:2])], axis=2).reshape(-1, 6 * HEAD_DIM)
    g_pad = jnp.pad(a_g, ((0, 0), (0, LANES - 3 * N_HEADS)))
    b_q = b_qkv[:, :GROUP_W] * scale
    w_all = jnp.concatenate([_pad_heads(a_q), _pad_heads(_swap_neg(a_q)), a_kv, kv_sw, g_pad,
                             b_q, b_qkv[:, GROUP_W:], c_uv, d_in], axis=1)
    w_t = jnp.concatenate([a_kv, kv_sw, b_qkv[:, GROUP_W:]], axis=1).T
    return w_all.astype(BF), w_t.astype(BF)


def _rope_tables(pos):
    half = HEAD_DIM // 2
    inv = ROPE_THETA ** (-jnp.arange(half, dtype=F32) / half)
    ang = pos.astype(F32)[:, None] * inv[None, :]
    cos, sin = jnp.cos(ang), jnp.sin(ang)
    cos_t = jnp.concatenate([cos.T, cos.T, jnp.ones((HEAD_DIM, pos.shape[0]), F32)], axis=0)
    sin_t = jnp.concatenate([sin.T, sin.T, jnp.zeros((HEAD_DIM, pos.shape[0]), F32)], axis=0)
    return jnp.tile(cos, (1, 4)), jnp.tile(sin, (1, 4)), cos_t, sin_t


def _cmp_weights(cmp_w):
    w = jnp.zeros((CMP_LEN, 2, HEAD_DIM, 2, HEAD_DIM), F32)
    w = w.at[:, 0, :, 0, :].set(cmp_w[0]).at[:, 1, :, 1, :].set(cmp_w[1])
    w = w.reshape(2, CMP_STRIDE * LANES, LANES)
    return jnp.concatenate([w[0], w[1]], axis=1).astype(BF)


def _cmp_pe_rows(cmp_pe):
    return jnp.pad(cmp_pe.reshape(2, CMP_STRIDE * LANES), ((0, 6), (0, 0))).astype(BF)


def _sel_constants(n_cmp, n_sel, t_len):
    n = np.arange(n_cmp)[None, :]
    j = np.arange(n_sel)[:, None]
    mt = ((n >= 4 * j - 1) & (n <= 4 * j + 3)).astype(np.float32) + ((n >= 4 * j) & (n <= 4 * j + 2))
    e = (np.arange(t_len)[None, :] // SEL_BLOCK == j).astype(np.float32)
    return jnp.asarray(mt, BF), jnp.asarray(e, BF)


def _gmlp_weights(sg_w, sg_b):
    wm = sg_w * jnp.tril(jnp.ones((CHUNK, CHUNK), sg_w.dtype))
    w_cat = jnp.transpose(wm, (1, 0, 2)).reshape(CHUNK, N_HEADS * CHUNK).astype(BF)
    b_full = jnp.repeat(sg_b.T, HEAD_DIM, axis=1)
    return w_cat, b_full


def _pool_weights(pool_w):
    w = jnp.zeros((N_HEADS, HEAD_DIM, N_HEADS, HEAD_DIM), F32)
    for g in range(N_HEADS):
        w = w.at[g, :, g, :].set(pool_w[g])
    return w.reshape(GROUP_W, GROUP_W).astype(BF)


def _sample_sel_weights(n_cmp_pad, n_sel_pad, n_cmp, n_sel):
    n = np.arange(n_cmp_pad)[:, None]
    j = np.arange(n_sel_pad)[None, :]
    ms = ((n >= 4 * j - 1) & (n <= 4 * j + 3)).astype(np.float32) + ((n >= 4 * j) & (n <= 4 * j + 2))
    ms = ms * ((n < n_cmp) & (j < n_sel))
    return jnp.asarray(ms, BF)


def _layer_weights(l, w_ada, b_ada, norm_g, w_ffn_up, w_ffn_down, w_in, w_out,
                   cmp_pe, cmp_w, sg_ln_g, sg_ln_b, sg_w, sg_b, pool_w, pool_scale):
    sgw_cat, sgb_full = _gmlp_weights(sg_w[l], sg_b[l])
    w_proj, w_proj_t = _proj_weights(w_in[l])
    return dict(
        layer=l, w_ada=w_ada, b_ada=b_ada[l][None, :], norm_g=norm_g[l][:, None, :],
        w_up=w_ffn_up, w_down=w_ffn_down, w_in=w_proj, w_in_t=w_proj_t, w_out=w_out,
        cmp_pe=_cmp_pe_rows(cmp_pe[l]), cmp_w=_cmp_weights(cmp_w[l]),
        ln_g=sg_ln_g[l][None, :], ln_b=sg_ln_b[l][None, :], sgw_cat=sgw_cat, sgb_full=sgb_full,
        sg_w00=jnp.repeat(sg_w[l][:, 0, 0], HEAD_DIM)[None, :], sg_b0=jnp.repeat(sg_b[l][:, 0], HEAD_DIM)[None, :],
        pool_w=_pool_weights(pool_w[l]), pool_scale=pool_scale[l][None, :])


def _mods(mod, shape):
    m = mod.reshape(mod.shape[0], N_MOD, D_MODEL)
    return [m[:, k].reshape(shape) for k in range(N_MOD)]


def prompt_layer(x, mod, lw, tables, consts, tm):
    bsz, t_len, _ = x.shape
    m = _mods(mod, (bsz, 1, D_MODEL))
    g = lw["norm_g"]
    cos, sin, cos_t, sin_t = tables
    l = lw["layer"]
    x = ffn_half(x, m[0], m[1], m[2], g[0], g[1], lw["w_up"], lw["w_down"], (l, 0), tm)
    pr = in_proj(x, m[3], m[4], g[2], cos, sin, lw["w_in"], lw["ln_g"], lw["ln_b"], tm,
                 t_side=(lw["w_in_t"], cos_t, sin_t))
    kcmp = compress_prompt(pr["kv_cmp"], lw["cmp_pe"], lw["cmp_w"])
    o_a = nsa_prompt(pr["qa"], pr["gate"], kcmp, pr["kv_sel_b"], pr["kv_win_b"], *consts)
    o_b = sb_prompt(pr["qb"], pr["kvb_b"])
    o_c, o_d = gmlp_pool_prompt(pr["u"], pr["v"], pr["pin"], lw["sgw_cat"], lw["sgb_full"],
                                lw["pool_w"], lw["pool_scale"], tm)
    x = out_proj_ffn(x, (o_a, o_b, o_c, o_d), m[5], g[3], lw["w_out"], l,
                     m[6], m[7], m[8], g[4], g[5], lw["w_up"], lw["w_down"], (l, 1), tm)
    n_win = min(WINDOW, t_len)
    state = (pr["kvt_cmp"], pr["kvt_sel"], pr["kvt_win"][:, :, t_len - n_win:], pr["kvbt"],
             pr["pin"][:, t_len - POOL_BUF:])
    return x, state


def _page_copies(cache_ref, layer, pt_ref, b, first_page, n_pages, buf_ref, slot, sem_ref):
    return [pltpu.make_async_copy(cache_ref.at[layer, pt_ref[b, first_page + p]], buf_ref.at[slot, p],
                                  sem_ref.at[slot]) for p in range(n_pages)]


SB_GROUP = 4


def _sb_sample_kernel(pt_ref, q_ref, cache_ref, o_ref, buf0, buf, sem0, sem, *, layer, n_pages):
    b = pl.program_id(0)
    n_grp = n_pages // SB_GROUP

    def copies(seq, grp, dst, slot, sm):
        return _page_copies(cache_ref, layer, pt_ref, seq, (n_grp - 1 - grp) * SB_GROUP, SB_GROUP, dst, slot, sm)

    @pl.when(b == 0)
    def _():
        for c in copies(0, 0, buf0, 0, sem0):
            c.start()

    @pl.when(b + 1 < pl.num_programs(0))
    def _():
        for c in copies(b + 1, 0, buf0, (b + 1) % 2, sem0):
            c.start()

    if n_grp > 1:
        for c in copies(b, 1, buf, 1, sem):
            c.start()

    row = lax.broadcasted_iota(jnp.int32, (8, GROUP_W), 0)
    lane = lax.broadcasted_iota(jnp.int32, (8, GROUP_W), 1)
    head_lanes = lane // HEAD_DIM == row
    q = jnp.broadcast_to(q_ref[...].astype(F32), (8, GROUP_W))
    qm = jnp.where(head_lanes, q, 0.0).astype(BF)
    tri = _tri_neg(PAGE)

    def sweep(src, slot, acc, car):
        for p in reversed(range(SB_GROUP)):
            kt = src[slot, p, 0:GROUP_W, :].astype(BF)
            vt = src[slot, p, GROUP_W:2 * GROUP_W, :].astype(BF)
            a, car = _sb_weights(_dot(qm, kt), tri, car, None)
            acc = acc + _dot_nt(a, vt)
        return acc, car

    for c in copies(b, 0, buf0, b % 2, sem0):
        c.wait()
    acc, car = sweep(buf0, b % 2, jnp.zeros((8, GROUP_W), F32), jnp.zeros((8, LANES), F32))

    def more(st):
        g, car_max, _, _ = st
        return (g < n_grp) & (car_max > SB_STOP)

    def body(st):
        g, _, acc, car = st
        slot = g % 2

        @pl.when(g + 1 < n_grp)
        def _():
            for c in copies(b, g + 1, buf, 1 - slot, sem):
                c.start()

        for c in copies(b, g, buf, slot, sem):
            c.wait()
        acc, car = sweep(buf, slot, acc, car)
        return g + 1, jnp.max(car), acc, car

    g, _, acc, _ = lax.while_loop(more, body, (1, jnp.max(car), acc, car))

    @pl.when(g < n_grp)
    def _():
        for c in copies(b, g, buf, g % 2, sem):
            c.wait()

    o_ref[...] = jnp.sum(jnp.where(head_lanes, acc, 0.0), axis=0, keepdims=True)


def sb_sample(page_table, qb, cache_t, layer):
    n_seq = qb.shape[0]
    n_pages = page_table.shape[1]
    grp_buf = pltpu.VMEM((2, SB_GROUP, 2 * GROUP_W, PAGE), F32)
    return pl.pallas_call(
        functools.partial(_sb_sample_kernel, layer=layer, n_pages=n_pages),
        grid_spec=pltpu.PrefetchScalarGridSpec(
            num_scalar_prefetch=1, grid=(n_seq,),
            in_specs=[pl.BlockSpec((None, 1, GROUP_W), lambda b, pt: (b, 0, 0)),
                      pl.BlockSpec(memory_space=pl.ANY)],
            out_specs=pl.BlockSpec((None, 1, GROUP_W), lambda b, pt: (b, 0, 0)),
            scratch_shapes=[grp_buf, grp_buf, pltpu.SemaphoreType.DMA((2,)), pltpu.SemaphoreType.DMA((2,))]),
        out_shape=jax.ShapeDtypeStruct((n_seq, 1, GROUP_W), F32),
        compiler_params=_params(("arbitrary",)),
        name="sb_sample",
    )(page_table, qb, cache_t)


def _slab_copies(cache_ref, layer, pt_ref, b, n_pages, buf_ref, slot, sem_ref):
    return [pltpu.make_async_copy(cache_ref.at[layer, pt_ref[b, p]],
                                  buf_ref.at[slot, :, pl.ds(p * PAGE, PAGE)], sem_ref.at[slot])
            for p in range(n_pages)]


CMP_CHUNK = 256
TAIL_ROWS = 128
XPOSE_TOKENS = 1024
SLAB_CAST = 2048


def _nsa_cmp_sample_kernel(pt_ref, q_ref, new_ref, pe_ref, w_ref, ms_ref, cache_ref, ocmp_ref, sel_ref,
                           buf, sem, rows_sc, lo_sc, hi_sc, *, layer, n_pages):
    b = pl.program_id(0)
    slot = b % 2

    def copies(seq, sl):
        return _slab_copies(cache_ref, layer, pt_ref, seq, n_pages, buf, sl, sem)

    @pl.when(b == 0)
    def _():
        for c in copies(0, 0):
            c.start()

    @pl.when(b + 1 < pl.num_programs(0))
    def _():
        for c in copies(b + 1, 1 - slot):
            c.start()

    for c in copies(b, slot):
        c.wait()

    past = n_pages * PAGE
    n_pad = lo_sc.shape[0]
    for c in range(past // XPOSE_TOKENS):
        span = slice(c * XPOSE_TOKENS, (c + 1) * XPOSE_TOKENS)
        rows_sc[span, :] = buf[slot, :, span].T
    r_io = lax.broadcasted_iota(jnp.int32, (TAIL_ROWS, LANES), 0)
    rows_sc[past:past + TAIL_ROWS, :] = jnp.where(r_io == 0, jnp.broadcast_to(new_ref[...], (TAIL_ROWS, LANES)), 0.0)

    chunk = math.gcd(CMP_CHUNK, past // CMP_STRIDE)

    def cmp_chunk(c, carry):
        g0 = pl.multiple_of(c * chunk, chunk)
        lo, hi = _compress_rows(rows_sc, chunk, pe_ref, w_ref, base=g0 * CMP_STRIDE)
        lo_sc[pl.ds(g0, chunk), :] = lo
        hi_sc[pl.ds(g0, chunk), :] = hi
        return carry

    lax.fori_loop(0, past // (CMP_STRIDE * chunk), cmp_chunk, 0)
    n_tail = TAIL_ROWS // CMP_STRIDE
    g_tail = past // CMP_STRIDE
    lo, hi = _compress_rows(rows_sc, n_tail, pe_ref, w_ref, base=past)
    lo_sc[g_tail:g_tail + n_tail, :] = lo
    hi_sc[g_tail:g_tail + n_tail, :] = hi
    lo_sc[g_tail + n_tail:n_pad, :] = jnp.zeros((n_pad - g_tail - n_tail, LANES), F32)
    hi_sc[g_tail + n_tail:n_pad + 8, :] = jnp.zeros((n_pad + 8 - g_tail - n_tail, LANES), F32)
    kc = (lo_sc[...] + hi_sc[pl.ds(1, n_pad), :]).astype(BF)

    q8 = q_ref[...]
    qpos = past
    s = _dot_nt(q8, kc)
    blk_end = lax.broadcasted_iota(jnp.int32, (1, n_pad), 1) * CMP_STRIDE + (CMP_LEN - 1)
    p = _softmax_rows(s, blk_end <= qpos)
    ocmp_ref[...] = _dot(p.astype(BF), kc)
    ps = jnp.broadcast_to(jnp.sum(p[0:N_HEADS], axis=0, keepdims=True), (8, n_pad))
    ms = ms_ref[...]
    p_slc = sum(_dot(part, ms) for part in _split3(ps))
    j_io = lax.broadcasted_iota(jnp.int32, p_slc.shape, 1)
    cur = qpos // SEL_BLOCK
    forced = (j_io == 0) | (j_io == cur) | (j_io == cur - 1)
    valid = j_io <= cur
    score = jnp.where(valid, jnp.where(forced, FORCE_SCORE, p_slc), NEG)
    sel_ref[...] = _topk_select(score, valid, N_TOPK, axis=1)


def nsa_cmp_sample(page_table, q8, new_cmp, pe, w_cmp, ms, cache_t, layer):
    n_seq = q8.shape[0]
    n_pages = page_table.shape[1]
    n_pad, n_sel_pad = ms.shape
    seq_spec = lambda w: pl.BlockSpec((None, 8, w), lambda b, pt: (b, 0, 0))
    const2 = lambda shape: pl.BlockSpec(shape, lambda b, pt: (0,) * len(shape))
    return pl.pallas_call(
        functools.partial(_nsa_cmp_sample_kernel, layer=layer, n_pages=n_pages),
        grid_spec=pltpu.PrefetchScalarGridSpec(
            num_scalar_prefetch=1, grid=(n_seq,),
            in_specs=[seq_spec(LANES), pl.BlockSpec((None, 1, LANES), lambda b, pt: (b, 0, 0)),
                      const2((8, CMP_STRIDE * LANES)), const2((CMP_STRIDE * LANES, 2 * LANES)),
                      const2((n_pad, n_sel_pad)),
                      pl.BlockSpec(memory_space=pl.ANY)],
            out_specs=[seq_spec(LANES), seq_spec(n_sel_pad)],
            scratch_shapes=[pltpu.VMEM((2, LANES, n_pages * PAGE), F32), pltpu.SemaphoreType.DMA((2,)),
                            pltpu.VMEM((n_pages * PAGE + TAIL_ROWS, LANES), F32),
                            pltpu.VMEM((n_pad, LANES), F32), pltpu.VMEM((n_pad + 8, LANES), F32)]),
        out_shape=[jax.ShapeDtypeStruct((n_seq, 8, LANES), F32), jax.ShapeDtypeStruct((n_seq, 8, n_sel_pad), F32)],
        compiler_params=_params(("arbitrary",)),
        name="nsa_cmp_sample",
    )(page_table, q8, new_cmp, pe, w_cmp, ms, cache_t)


def _nsa_sel_sample_kernel(pt_ref, q_ref, sel_ref, ocmp_ref, gate_ref, news_ref, neww_ref, e_ref, win_ref,
                           cache_ref, o_ref, buf, sem, kv_sc, *, layer, n_pages):
    b = pl.program_id(0)
    slot = b % 2

    def copies(seq, sl):
        return _slab_copies(cache_ref, layer, pt_ref, seq, n_pages, buf, sl, sem)

    @pl.when(b == 0)
    def _():
        for c in copies(0, 0):
            c.start()

    @pl.when(b + 1 < pl.num_programs(0))
    def _():
        for c in copies(b + 1, 1 - slot):
            c.start()

    for c in copies(b, slot):
        c.wait()

    past = n_pages * PAGE
    qpos = past
    q8 = q_ref[...]
    q8f = q8.astype(F32)
    sel = sel_ref[...]
    cast = math.gcd(SLAB_CAST, past)
    for c in range(past // cast):
        span = slice(c * cast, (c + 1) * cast)
        kv_sc[:, span] = buf[slot, :, span].astype(BF)
    kvt = kv_sc[...]

    def new_token(new_ref, allowed):
        kn = new_ref[...].astype(BF).astype(F32)
        s_new = jnp.sum(q8f * kn, axis=-1, keepdims=True)
        return kn, jnp.where(allowed, s_new, NEG)

    blk = e_ref.shape[0]
    sel_b = sel.astype(BF)
    selx = jnp.concatenate([_dot(sel_b[:, c * blk:(c + 1) * blk], e_ref[...])
                            for c in range(past // (blk * SEL_BLOCK))], axis=1)
    allow = (selx > 0.5) & (lax.broadcasted_iota(jnp.int32, (1, past), 1) <= qpos)
    sm = jnp.where(allow, _dot(q8, kvt), NEG)
    kn_s, s_new = new_token(news_ref, (sel[:, qpos // SEL_BLOCK:qpos // SEL_BLOCK + 1] > 0.5))
    m = jnp.maximum(jnp.max(sm, axis=-1, keepdims=True), s_new)
    e = jnp.where(allow, jnp.exp(sm - m), 0.0)
    e_new = jnp.where(s_new > 0.5 * NEG, jnp.exp(s_new - m), 0.0)
    l_sel = jnp.sum(e, axis=-1, keepdims=True) + e_new
    o_sel = (_dot_nt(e.astype(BF), kvt) + e_new * kn_s) / jnp.maximum(l_sel, 1e-30)

    n_buf = win_ref.shape[1]
    wb = win_ref[...].astype(BF)
    kwpos = past - n_buf + lax.broadcasted_iota(jnp.int32, (1, n_buf), 1)
    dist = qpos - kwpos
    allow_w = (dist >= 0) & (dist <= WINDOW) & (kwpos >= 0)
    sw = jnp.where(allow_w, _dot(q8, wb), NEG)
    kn_w, sw_new = new_token(neww_ref, True)
    mw = jnp.maximum(jnp.max(sw, axis=-1, keepdims=True), sw_new)
    ew = jnp.where(allow_w, jnp.exp(sw - mw), 0.0)
    ew_new = jnp.exp(sw_new - mw)
    l_w = jnp.sum(ew, axis=-1, keepdims=True) + ew_new
    o_win = (_dot_nt(ew.astype(BF), wb) + ew_new * kn_w) / jnp.maximum(l_w, 1e-30)

    g = jnp.broadcast_to(gate_ref[...], (8, LANES))
    g_row = lax.broadcasted_iota(jnp.int32, (8, LANES), 0)
    g_lane = lax.broadcasted_iota(jnp.int32, (8, LANES), 1)
    gk = [jnp.sum(jnp.where(g_lane == 3 * g_row + k, g, 0.0), axis=-1, keepdims=True) for k in range(3)]
    o_ref[...] = gk[0] * ocmp_ref[...] + gk[1] * o_sel + gk[2] * o_win


def nsa_sel_sample(page_table, q8, sel, o_cmp, gate, new_sel, new_win, e_mat, win_t, cache_t, layer):
    n_seq = q8.shape[0]
    n_pages = page_table.shape[1]
    n_buf = win_t.shape[3]
    seq_spec = lambda r, w: pl.BlockSpec((None, r, w), lambda b, pt: (b, 0, 0))
    return pl.pallas_call(
        functools.partial(_nsa_sel_sample_kernel, layer=layer, n_pages=n_pages),
        grid_spec=pltpu.PrefetchScalarGridSpec(
            num_scalar_prefetch=1, grid=(n_seq,),
            in_specs=[seq_spec(8, LANES), seq_spec(8, sel.shape[2]), seq_spec(8, LANES), seq_spec(1, LANES),
                      seq_spec(1, LANES), seq_spec(1, LANES),
                      pl.BlockSpec(e_mat.shape, lambda b, pt: (0, 0)),
                      pl.BlockSpec((None, None, LANES, n_buf), lambda b, pt: (layer, b, 0, 0)),
                      pl.BlockSpec(memory_space=pl.ANY)],
            out_specs=seq_spec(8, LANES),
            scratch_shapes=[pltpu.VMEM((2, LANES, n_pages * PAGE), F32), pltpu.SemaphoreType.DMA((2,)),
                            pltpu.VMEM((LANES, n_pages * PAGE), BF)]),
        out_shape=jax.ShapeDtypeStruct((n_seq, 8, LANES), F32),
        compiler_params=_params(("arbitrary",)),
        name="nsa_sel_sample",
    )(page_table, q8, sel, o_cmp, gate, new_sel, new_win, e_mat, win_t, cache_t)


def _gmlp_pool_sample_kernel(u_ref, v_ref, pin_ref, hist_ref, w00_ref, b0_ref, pw_ref, ps_ref, oc_ref, od_ref,
                             *, past_len):
    oc_ref[...] = u_ref[...] * (w00_ref[...] * v_ref[...] + b0_ref[...])
    x = pin_ref[...]
    lane = lax.broadcasted_iota(jnp.int32, (1, GROUP_W), 1)
    grp = lane // HEAD_DIM
    sums = []
    tot = x
    k = 1
    for wlen in POOL_WINDOWS:
        while k < wlen:
            tot = tot + hist_ref[POOL_BUF - k]
            k += 1
        sums.append(tot)
    tot = jnp.where(grp == 0, sums[0], jnp.where(grp == 1, sums[1], jnp.where(grp == 2, sums[2], sums[3])))
    wlen = jnp.where(grp == 0, POOL_WINDOWS[0], jnp.where(grp == 1, POOL_WINDOWS[1],
                                                          jnp.where(grp == 2, POOL_WINDOWS[2], POOL_WINDOWS[3])))
    cnt = jnp.minimum(wlen, past_len + 1).astype(F32)
    d = tot / cnt - x
    od_ref[...] = _dot(d.astype(BF), pw_ref[...]) * ps_ref[...]


def gmlp_pool_sample(u, v, pin, hist_t, layer, w00, b0, pool_w_bd, pool_scale, past_len):
    n_seq = u.shape[0]
    full = lambda shape: pl.BlockSpec(shape, lambda i: (0,) * len(shape))
    return pl.pallas_call(
        functools.partial(_gmlp_pool_sample_kernel, past_len=past_len),
        grid=(1,),
        in_specs=[full((n_seq, GROUP_W))] * 3 +
                 [pl.BlockSpec((None, POOL_BUF, n_seq, GROUP_W), lambda i: (layer, 0, 0, 0)),
                  full((1, GROUP_W)), full((1, GROUP_W)), full((GROUP_W, GROUP_W)), full((1, GROUP_W))],
        out_specs=[full((n_seq, GROUP_W))] * 2,
        out_shape=[jax.ShapeDtypeStruct((n_seq, GROUP_W), F32)] * 2,
        compiler_params=_params(("arbitrary",)),
        name="gmlp_pool_sample",
    )(u, v, pin, hist_t, w00, b0, pool_w_bd, pool_scale)


def sample_layer(x, mod, lw, tables, sel_consts, caches_t, layer, page_table):
    n_seq = x.shape[1]
    past_len = page_table.shape[1] * PAGE
    cmp_t, sel_t, win_t, sb_t, pool_t = caches_t
    ms, e_mat = sel_consts
    m = _mods(mod, (1, n_seq, D_MODEL))
    g = lw["norm_g"]
    x = ffn_half(x, m[0], m[1], m[2], g[0], g[1], lw["w_up"], lw["w_down"], (layer, 0), n_seq)
    pr = in_proj(x, m[3], m[4], g[2], tables[0], tables[1], lw["w_in"], lw["ln_g"], lw["ln_b"], n_seq)
    rows = {k: v[0] for k, v in pr.items()}
    q8 = jnp.pad(rows["qa"].reshape(n_seq, N_HEADS, LANES), ((0, 0), (0, 8 - N_HEADS), (0, 0)))
    o_cmp, sel = nsa_cmp_sample(page_table, q8, rows["kv_cmp"][:, None, :], lw["cmp_pe"], lw["cmp_w"], ms,
                                cmp_t, layer)
    o8 = nsa_sel_sample(page_table, q8, sel, o_cmp, rows["gate"][:, None, :], rows["kv_sel"][:, None, :],
                        rows["kv_win"][:, None, :], e_mat, win_t, sel_t, layer)
    o_a = o8[:, :N_HEADS, HEAD_DIM:].reshape(1, n_seq, GROUP_W).astype(BF)
    o_b = sb_sample(page_table, rows["qb"][:, None, :], sb_t, layer).reshape(1, n_seq, GROUP_W).astype(BF)
    o_c, o_d = gmlp_pool_sample(rows["u"], rows["v"], rows["pin"], pool_t, layer, lw["sg_w00"], lw["sg_b0"],
                                lw["pool_w"], lw["pool_scale"], past_len)
    x = out_proj_ffn(x, (o_a, o_b, o_c[None].astype(BF), o_d[None].astype(BF)), m[5], g[3], lw["w_out"], layer,
                     m[6], m[7], m[8], g[4], g[5], lw["w_up"], lw["w_down"], (layer, 1), n_seq)
    state = (rows["kv_cmp"], rows["kv_sel"], rows["kv_win"], rows["kvb"], rows["pin"], rows["v"])
    return x, state


PROMPT_TM = 512


def kernel(x_prompt, x_sample, cache_nsa_cmp, cache_nsa_sel, cache_nsa_win, cache_sb, state_pool, page_table,
           c_prompt, c_sample, w_ada, b_ada, norm_g, w_ffn_up, w_ffn_down, w_in, w_out,
           cmp_pe, cmp_w, sg_ln_g, sg_ln_b, sg_w, sg_b, pool_w, pool_scale):
    n_p, t_len, _ = x_prompt.shape
    n_s, t_dec, _ = x_sample.shape
    assert t_dec == 1, "the sample step advances one token per sequence"
    depth = w_ada.shape[0]
    n_phys = cache_sb.shape[1]
    past_len = page_table.shape[1] * PAGE
    c_all = jnp.concatenate([c_prompt, c_sample], axis=0)
    c_all = jnp.pad(c_all, ((0, (-c_all.shape[0]) % 8), (0, 0)))
    tables_p = _rope_tables(jnp.arange(t_len))
    tables_s = _rope_tables(jnp.full((n_s,), past_len))
    consts = _sel_constants(t_len // CMP_STRIDE, t_len // SEL_BLOCK, t_len)
    n_rows = -(-(past_len + t_dec) // SEL_BLOCK) * SEL_BLOCK
    n_cmp = n_rows // CMP_STRIDE - 1
    n_sel = (n_cmp + 1) // (SEL_BLOCK // CMP_STRIDE)
    ms = _sample_sel_weights(-(-((past_len + TAIL_ROWS) // CMP_STRIDE) // LANES) * LANES,
                             -(-n_sel // LANES) * LANES, n_cmp, n_sel)
    blk_s = math.gcd(LANES, past_len // SEL_BLOCK)
    sel_consts_s = (ms, _sel_constants(1, blk_s, blk_s * SEL_BLOCK)[1])
    caches_t = (jnp.transpose(cache_nsa_cmp, (0, 1, 3, 4, 2)).reshape(depth, n_phys, LANES, PAGE),
                jnp.transpose(cache_nsa_sel, (0, 1, 3, 4, 2)).reshape(depth, n_phys, LANES, PAGE),
                jnp.transpose(cache_nsa_win, (0, 1, 3, 4, 2)).reshape(depth, n_s, LANES, -1),
                jnp.transpose(cache_sb, (0, 1, 3, 4, 5, 2)).reshape(depth, n_phys, 2 * GROUP_W, PAGE),
                jnp.transpose(state_pool, (0, 2, 1, 3)))
    y_p, y_s = x_prompt, x_sample.reshape(1, n_s, D_MODEL)
    st_p, st_s = [], []
    w_ada, w_ffn_up, w_ffn_down, w_out = (w.astype(BF) for w in (w_ada, w_ffn_up, w_ffn_down, w_out))
    for l in range(depth):
        lw = _layer_weights(l, w_ada, b_ada, norm_g, w_ffn_up, w_ffn_down, w_in, w_out,
                            cmp_pe, cmp_w, sg_ln_g, sg_ln_b, sg_w, sg_b, pool_w, pool_scale)
        mod = ada_mod(c_all, lw["w_ada"], lw["b_ada"], l)
        y_p, s = prompt_layer(y_p, mod[:n_p], lw, tables_p, consts, PROMPT_TM)
        st_p.append(s)
        y_s, s = sample_layer(y_s, mod[n_p:n_p + n_s], lw, tables_s, sel_consts_s, caches_t, l, page_table)
        st_s.append(s)

    def stack(states, i):
        return jnp.stack([s[i] for s in states], axis=0)

    def kv_rows(x_t):
        d, bsz, _, t = x_t.shape
        return jnp.transpose(x_t.reshape(d, bsz, 2, HEAD_DIM, t), (0, 1, 4, 2, 3))

    sb_p = stack(st_p, 3)
    sb_p = jnp.transpose(sb_p.reshape(depth, n_p, 2, N_HEADS, HEAD_DIM, t_len), (0, 1, 5, 2, 3, 4))
    win_new = stack(st_s, 2).reshape(depth, n_s, 1, 2, HEAD_DIM)
    pool_new = stack(st_s, 4)[:, :, None, :]
    return (y_p, y_s.reshape(n_s, 1, D_MODEL),
            kv_rows(stack(st_p, 0)), stack(st_s, 0).reshape(depth, n_s, 1, 2, HEAD_DIM),
            kv_rows(stack(st_p, 1)), stack(st_s, 1).reshape(depth, n_s, 1, 2, HEAD_DIM),
            kv_rows(stack(st_p, 2)), jnp.concatenate([cache_nsa_win[:, :, 1:], win_new], axis=2),
            sb_p, stack(st_s, 3).reshape(depth, n_s, 1, 2, N_HEADS, HEAD_DIM),
            stack(st_p, 4), jnp.concatenate([state_pool[:, :, 1:], pool_new], axis=2),
            stack(st_s, 5)[:, :, None, :])
```

```python
import functools
import math

import numpy as np
import jax
import jax.numpy as jnp
from jax import lax
from jax.experimental import pallas as pl
from jax.experimental.pallas import tpu as pltpu

F32 = jnp.float32
BF = jnp.bfloat16

D_MODEL = 1024
HEAD_DIM = 64
N_HEADS = 4
GROUP_W = N_HEADS * HEAD_DIM
D_FF = 2816
N_MOD = 9
PAGE = 128
CMP_STRIDE = 16
CMP_LEN = 32
SEL_BLOCK = 64
N_TOPK = 16
WINDOW = 512
CHUNK = 128
POOL_WINDOWS = (2, 4, 8, 16)
POOL_MAX = 16
POOL_BUF = POOL_MAX - 1
ROPE_THETA = 10000.0
EPS = 1e-6
FORCE_SCORE = 1e9
NEG = -3.0e38
MASK_BIAS = 1.0e30
LANES = 128
VMEM_LIMIT = 56 * 1024 * 1024

C_QA, C_QAS, C_KV, C_KVS, C_G, C_QB, C_KVB, C_UV, C_DIN, C_END = (
    0, 512, 1024, 1408, 1792, 1920, 2176, 2688, 3200, 3456)


def _params(sem, vmem=VMEM_LIMIT):
    return pltpu.CompilerParams(dimension_semantics=sem, vmem_limit_bytes=vmem)


def _dot(a, b):
    return jnp.dot(a, b, preferred_element_type=F32)


def _dot_nt(a, b):
    return lax.dot_general(a, b, (((1,), (1,)), ((), ())), preferred_element_type=F32)


def _rms(x, g):
    return x * lax.rsqrt(jnp.mean(x * x, axis=-1, keepdims=True) + EPS) * g


def _split3(x):
    hi = x.astype(BF)
    r = x - hi.astype(F32)
    mid = r.astype(BF)
    lo = (r - mid.astype(F32)).astype(BF)
    return hi, mid, lo


def _ada_kernel(c_ref, w_ref, b_ref, o_ref):
    c = c_ref[...]
    s = (c * jax.nn.sigmoid(c)).astype(BF)
    o_ref[...] = _dot(s, w_ref[...]) + b_ref[...]


def _stacked_spec(tail, idx):
    return pl.BlockSpec((None,) * len(idx) + tail, lambda *_: idx + (0,) * len(tail))


def ada_mod(c_all, w_ada, b_ada, layer):
    r = c_all.shape[0]
    n = w_ada.shape[2]
    tn = 2304
    return pl.pallas_call(
        _ada_kernel,
        grid=(n // tn,),
        in_specs=[pl.BlockSpec((r, D_MODEL), lambda j: (0, 0)),
                  pl.BlockSpec((None, D_MODEL, tn), lambda j: (layer, 0, j)),
                  pl.BlockSpec((1, tn), lambda j: (0, j))],
        out_specs=pl.BlockSpec((r, tn), lambda j: (0, j)),
        out_shape=jax.ShapeDtypeStruct((r, n), F32),
        compiler_params=_params(("parallel",)),
        name="ada_mod",
    )(c_all, w_ada, b_ada)


FF_CHUNK = 256


def _ffn_kernel(x_ref, sh_ref, sc_ref, gt_ref, g1_ref, g2_ref, wu_ref, wd_ref, o_ref):
    o_ref[...] = _swiglu_half(x_ref[...], sh_ref, sc_ref, gt_ref, g1_ref, g2_ref, wu_ref, wd_ref)


def _swiglu_half(x, sh_ref, sc_ref, gt_ref, g1_ref, g2_ref, wu_ref, wd_ref):
    h = _rms(x, g1_ref[...]) * (1.0 + sc_ref[...]) + sh_ref[...]
    hb = h.astype(BF)
    acc = jnp.zeros(x.shape, F32)
    for c in range(D_FF // FF_CHUNK):
        lo = c * FF_CHUNK
        gate = _dot(hb, wu_ref[:, lo:lo + FF_CHUNK])
        up = _dot(hb, wu_ref[:, D_FF + lo:D_FF + lo + FF_CHUNK])
        a = (gate * jax.nn.sigmoid(gate) * up).astype(BF)
        acc = acc + _dot(a, wd_ref[lo:lo + FF_CHUNK, :])
    return x + 0.5 * gt_ref[...] * _rms(acc, g2_ref[...])


def _mod_spec(mod, tm):
    if mod.shape[1] == 1:
        return pl.BlockSpec((None, 1, D_MODEL), lambda b, t: (b, 0, 0))
    return pl.BlockSpec((None, tm, D_MODEL), lambda b, t: (b, t, 0))


def _row_spec(tm, w):
    return pl.BlockSpec((None, tm, w), lambda b, t: (b, t, 0))


def _const_spec(shape):
    nd = len(shape)
    return pl.BlockSpec(shape, lambda b, t: (0,) * nd)


def ffn_half(x, shift, scale, gate, g1, g2, w_up, w_down, widx, tm):
    bsz, t_len, _ = x.shape
    return pl.pallas_call(
        _ffn_kernel,
        grid=(bsz, t_len // tm),
        in_specs=[_row_spec(tm, D_MODEL), _mod_spec(shift, tm), _mod_spec(scale, tm), _mod_spec(gate, tm),
                  _const_spec((1, D_MODEL)), _const_spec((1, D_MODEL)),
                  _stacked_spec((D_MODEL, 2 * D_FF), widx), _stacked_spec((D_FF, D_MODEL), widx)],
        out_specs=_row_spec(tm, D_MODEL),
        out_shape=jax.ShapeDtypeStruct(x.shape, F32),
        compiler_params=_params(("parallel", "parallel")),
        name="ffn_half",
    )(x, shift, scale, gate, g1, g2, w_up, w_down)


def _gelu_tanh(x):
    return 0.5 * x * (1.0 + jnp.tanh(np.sqrt(2.0 / np.pi).astype(np.float32) * (x + 0.044715 * (x * x * x))))


PROJ_IN = ("x", "shift", "scale", "g", "cos", "sin", "w", "ln_g", "ln_b")
PROJ_IN_T = ("w_t", "cos_t", "sin_t")
PROJ_OUT = (("qa", 2 * GROUP_W, BF), ("kv_cmp", LANES, F32), ("gate", LANES, F32), ("qb", GROUP_W, BF),
            ("u", GROUP_W, F32), ("v", GROUP_W, F32), ("pin", GROUP_W, F32))
PROJ_OUT_ROWS = (("kv_sel", LANES, F32), ("kv_win", LANES, F32), ("kvb", 2 * GROUP_W, F32))
PROJ_OUT_B16 = (("kv_sel_b", LANES, BF), ("kv_win_b", LANES, BF), ("kvb_b", 2 * GROUP_W, BF))
PROJ_OUT_T = (("kvt_cmp", LANES), ("kvt_sel", LANES), ("kvt_win", LANES), ("kvbt", 2 * GROUP_W))


def _proj_kernel(*refs, feature_major):
    names = PROJ_IN + (PROJ_IN_T if feature_major else ())
    names += tuple(n for n, _, _ in PROJ_OUT + (PROJ_OUT_B16 if feature_major else PROJ_OUT_ROWS))
    names += tuple(n for n, _ in PROJ_OUT_T) if feature_major else ()
    r = dict(zip(names, refs))
    x = r["x"][...]
    h = _rms(x, r["g"][...]) * (1.0 + r["scale"][...]) + r["shift"][...]
    hb = h.astype(BF)
    cos = r["cos"][...]
    sin = r["sin"][...]
    w_ref = r["w"]

    def mm(lo, hi):
        return _dot(hb, w_ref[:, lo:hi])

    def cols(a, base, lo, hi):
        return a[:, lo - base:hi - base]

    q_p, q_ps = mm(C_QA, C_QAS), mm(C_QAS, C_KV)
    for j in range(N_HEADS):
        span = slice(LANES * j, LANES * (j + 1))
        r["qa"][:, span] = (q_p[:, span] * cos + q_ps[:, span] * sin).astype(BF)
    lane = lax.broadcasted_iota(jnp.int32, cos.shape, 1)
    ckv = jnp.where(lane < HEAD_DIM, cos, 1.0)
    kvg = mm(C_KV, C_QB)
    for j, nm in enumerate(("kv_cmp", "kv_sel", "kv_win")):
        p = cols(kvg, C_KV, C_KV + LANES * j, C_KV + LANES * (j + 1))
        ps = cols(kvg, C_KV, C_KVS + LANES * j, C_KVS + LANES * (j + 1))
        kv = p * ckv + ps * sin
        if nm in r:
            r[nm][...] = kv
        if nm + "_b" in r:
            r[nm + "_b"][...] = kv.astype(BF)
    r["gate"][...] = jax.nn.sigmoid(cols(kvg, C_KV, C_G, C_QB))
    rest = mm(C_QB, C_END)
    r["qb"][...] = cols(rest, C_QB, C_QB, C_KVB).astype(BF)
    kvb = cols(rest, C_QB, C_KVB, C_UV)
    if feature_major:
        r["kvb_b"][...] = kvb.astype(BF)
    else:
        r["kvb"][...] = kvb
    r["u"][...] = _gelu_tanh(cols(rest, C_QB, C_UV, C_UV + GROUP_W))
    v = _gelu_tanh(cols(rest, C_QB, C_UV + GROUP_W, C_DIN))
    vc = v - jnp.mean(v, axis=-1, keepdims=True)
    vn = vc * lax.rsqrt(jnp.mean(vc * vc, axis=-1, keepdims=True) + EPS)
    r["v"][...] = vn * r["ln_g"][...] + r["ln_b"][...]
    r["pin"][...] = cols(rest, C_QB, C_DIN, C_END)
    if feature_major:
        wt_ref = r["w_t"]
        cos_t = r["cos_t"][...]
        sin_t = r["sin_t"][...]
        n_kv = 3 * LANES
        for j, nm in enumerate(("kvt_cmp", "kvt_sel", "kvt_win")):
            p = _dot_nt(wt_ref[LANES * j:LANES * (j + 1), :], hb)
            ps = _dot_nt(wt_ref[n_kv + LANES * j:n_kv + LANES * (j + 1), :], hb)
            r[nm][...] = p * cos_t + ps * sin_t
        r["kvbt"][...] = _dot_nt(wt_ref[2 * n_kv:2 * n_kv + 2 * GROUP_W, :], hb)


def in_proj(x, shift, scale, g, cos, sin, w_all, ln_g, ln_b, tm, t_side=None):
    bsz, t_len, _ = x.shape
    feature_major = t_side is not None
    tab_spec = pl.BlockSpec((tm, LANES), lambda b, t: (t, 0))
    in_specs = [_row_spec(tm, D_MODEL), _mod_spec(shift, tm), _mod_spec(scale, tm),
                _const_spec((1, D_MODEL)), tab_spec, tab_spec,
                _const_spec((D_MODEL, C_END)), _const_spec((1, GROUP_W)), _const_spec((1, GROUP_W))]
    args = [x, shift, scale, g, cos, sin, w_all, ln_g, ln_b]
    outs = PROJ_OUT + (PROJ_OUT_B16 if feature_major else PROJ_OUT_ROWS)
    out_specs = [_row_spec(tm, w) for _, w, _ in outs]
    out_shape = [jax.ShapeDtypeStruct((bsz, t_len, w), dt) for _, w, dt in outs]
    names = [n for n, _, _ in outs]
    if feature_major:
        tab_t_spec = pl.BlockSpec((LANES, tm), lambda b, t: (0, t))
        in_specs += [_const_spec(t_side[0].shape), tab_t_spec, tab_t_spec]
        args += list(t_side)
        out_specs += [pl.BlockSpec((None, w, tm), lambda b, t: (b, 0, t)) for _, w in PROJ_OUT_T]
        out_shape += [jax.ShapeDtypeStruct((bsz, w, t_len), F32) for _, w in PROJ_OUT_T]
        names += [n for n, _ in PROJ_OUT_T]
    res = pl.pallas_call(
        functools.partial(_proj_kernel, feature_major=feature_major),
        grid=(bsz, t_len // tm),
        in_specs=in_specs, out_specs=out_specs, out_shape=out_shape,
        compiler_params=_params(("parallel", "parallel")),
        name="in_proj",
    )(*args)
    return dict(zip(names, res))


def _outproj_ffn_kernel(x_ref, oa_ref, ob_ref, oc_ref, od_ref, gto_ref, go_ref, wo_ref,
                        sh_ref, sc_ref, gt_ref, g1_ref, g2_ref, wu_ref, wd_ref, o_ref):
    y = _dot(oa_ref[...], wo_ref[0:GROUP_W, :])
    y = y + _dot(ob_ref[...], wo_ref[GROUP_W:2 * GROUP_W, :])
    y = y + _dot(oc_ref[...], wo_ref[2 * GROUP_W:3 * GROUP_W, :])
    y = y + _dot(od_ref[...], wo_ref[3 * GROUP_W:4 * GROUP_W, :])
    x = x_ref[...] + gto_ref[...] * _rms(y, go_ref[...])
    o_ref[...] = _swiglu_half(x, sh_ref, sc_ref, gt_ref, g1_ref, g2_ref, wu_ref, wd_ref)


def out_proj_ffn(x, mix, gate_o, g_o, w_out, layer, shift, scale, gate, g1, g2, w_up, w_down, widx, tm):
    bsz, t_len, _ = x.shape
    return pl.pallas_call(
        _outproj_ffn_kernel,
        grid=(bsz, t_len // tm),
        in_specs=[_row_spec(tm, D_MODEL)] + [_row_spec(tm, GROUP_W)] * 4 +
                 [_mod_spec(gate_o, tm), _const_spec((1, D_MODEL)), _stacked_spec((D_MODEL, D_MODEL), (layer,)),
                  _mod_spec(shift, tm), _mod_spec(scale, tm), _mod_spec(gate, tm),
                  _const_spec((1, D_MODEL)), _const_spec((1, D_MODEL)),
                  _stacked_spec((D_MODEL, 2 * D_FF), widx), _stacked_spec((D_FF, D_MODEL), widx)],
        out_specs=_row_spec(tm, D_MODEL),
        out_shape=jax.ShapeDtypeStruct(x.shape, F32),
        compiler_params=_params(("parallel", "parallel")),
        name="out_proj_ffn",
    )(x, *mix, gate_o, g_o, w_out, shift, scale, gate, g1, g2, w_up, w_down)


def _compress_rows(row_ref, n_grp, pe_ref, w_ref, base=0):
    parts = [row_ref[pl.ds(base + l, n_grp, stride=CMP_STRIDE), :].astype(BF) for l in range(CMP_STRIDE)]
    both = _dot(jnp.concatenate(parts, axis=1), w_ref[...])
    pe_w = _dot(pe_ref[...], w_ref[...])
    return both[:, :LANES] + pe_w[0:1, :LANES], both[:, LANES:] + pe_w[1:2, LANES:]


def _compress_kernel(row_ref, pe_ref, w_ref, o_ref, hi_sc):
    n_grp = o_ref.shape[0]
    lo, hi = _compress_rows(row_ref, n_grp, pe_ref, w_ref)
    hi_sc[0:n_grp, :] = hi
    hi_sc[n_grp:n_grp + 8, :] = jnp.zeros((8, LANES), F32)
    o_ref[...] = (lo + hi_sc[pl.ds(1, n_grp), :]).astype(BF)


def compress_prompt(kv_cmp, pe, w_cmp):
    bsz, t_len, _ = kv_cmp.shape
    n_grp = t_len // CMP_STRIDE
    return pl.pallas_call(
        _compress_kernel,
        grid=(bsz,),
        in_specs=[pl.BlockSpec((None, t_len, LANES), lambda b: (b, 0, 0)),
                  pl.BlockSpec((8, CMP_STRIDE * LANES), lambda b: (0, 0)),
                  pl.BlockSpec((CMP_STRIDE * LANES, 2 * LANES), lambda b: (0, 0))],
        out_specs=pl.BlockSpec((None, n_grp, LANES), lambda b: (b, 0, 0)),
        out_shape=jax.ShapeDtypeStruct((bsz, n_grp, LANES), BF),
        scratch_shapes=[pltpu.VMEM((n_grp + 8, LANES), F32)],
        compiler_params=_params(("parallel",)),
        name="nsa_compress",
    )(kv_cmp, pe, w_cmp)


NSA_Q = 256
NSA_KC = 512


def _softmax_bias(s, bias):
    sm = s + bias
    m = jnp.maximum(jnp.max(sm, axis=-1, keepdims=True), -0.5 * MASK_BIAS)
    return jnp.exp(sm - m)


def _softmax_rows(s, allow):
    sm = jnp.where(allow, s, NEG)
    m = jnp.max(sm, axis=-1, keepdims=True)
    e = jnp.where(allow, jnp.exp(sm - m), 0.0)
    return e / jnp.maximum(jnp.sum(e, axis=-1, keepdims=True), 1e-30)


def _topk_select(score, valid, n_pick, axis=0):
    n_blk = score.shape[axis]
    j_io = lax.broadcasted_iota(jnp.int32, score.shape, axis)
    sel = jnp.zeros(score.shape, F32)
    sc = score
    for _ in range(n_pick):
        m = jnp.max(sc, axis=axis, keepdims=True)
        idx = jnp.min(jnp.where(sc == m, j_io, n_blk), axis=axis, keepdims=True)
        pick = j_io == idx
        sel = jnp.where(pick, 1.0, sel)
        sc = jnp.where(pick, NEG, sc)
    return jnp.where(valid, sel, 0.0)


def _nsa_prompt_kernel(qa_ref, g_ref, kcmp_ref, ksel_ref, kwin_ref, mt_ref, e_ref, o_ref,
                       m_sc, acc_sc, sa_sc, sb_sc):
    i = pl.program_id(1)
    s0 = i * NSA_Q
    n_cmp = kcmp_ref.shape[0]
    n_sel = mt_ref.shape[0]
    qs = jnp.concatenate([qa_ref[:, LANES * h:LANES * (h + 1)] for h in range(N_HEADS)], axis=0)
    qpos = s0 + lax.broadcasted_iota(jnp.int32, (NSA_Q, 1), 0)

    def heads(x):
        return jnp.concatenate([x] * N_HEADS, axis=0)

    kc = kcmp_ref[...]
    blk_end = lax.broadcasted_iota(jnp.int32, (1, n_cmp), 1) * CMP_STRIDE + (CMP_LEN - 1)
    e = _softmax_bias(_dot_nt(qs, kc), heads(jnp.where(blk_end <= qpos, 0.0, -MASK_BIAS)))
    p = e * (1.0 / jnp.maximum(jnp.sum(e, axis=-1, keepdims=True), 1e-30))
    o_cmp = _dot(p.astype(BF), kc)
    ps = p[0:NSA_Q] + p[NSA_Q:2 * NSA_Q] + p[2 * NSA_Q:3 * NSA_Q] + p[3 * NSA_Q:4 * NSA_Q]
    mt = mt_ref[...]
    p_slc_t = sum(_dot_nt(mt, part) for part in _split3(ps))
    j_io = lax.broadcasted_iota(jnp.int32, (n_sel, NSA_Q), 0)
    cur = (s0 + lax.broadcasted_iota(jnp.int32, (1, NSA_Q), 1)) // SEL_BLOCK
    forced = (j_io == 0) | (j_io == cur) | (j_io == cur - 1)
    valid = j_io <= cur
    score_t = jnp.where(valid, jnp.where(forced, FORCE_SCORE, p_slc_t), NEG)
    sel = _topk_select(score_t, valid, min(N_TOPK, n_sel)).T.astype(BF)

    m_sc[...] = jnp.full(m_sc.shape, NEG, F32)
    acc_sc[...] = jnp.zeros(acc_sc.shape, F32)
    rep = NSA_KC // LANES
    key_lane = lax.broadcasted_iota(jnp.int32, (1, LANES), 1) < HEAD_DIM
    last = (s0 + NSA_Q - 1) // NSA_KC

    def chunk_start(c):
        return pl.multiple_of(jnp.minimum(c, last) * NSA_KC, NSA_KC)

    def scores(c, dst):
        dst[...] = _dot_nt(qs, ksel_ref[pl.ds(chunk_start(c), NSA_KC), :])

    def update(c, src):
        k0 = chunk_start(c)
        kv = ksel_ref[pl.ds(k0, NSA_KC), :]
        kv1 = jnp.where(key_lane, jnp.ones_like(kv), kv)
        selx = _dot(sel, e_ref[:, pl.ds(k0, NSA_KC)])
        tok = k0 + lax.broadcasted_iota(jnp.int32, (1, NSA_KC), 1)
        live = jnp.where(c <= last, 0.0, -MASK_BIAS)
        sm = src[...] + heads(jnp.where((selx > 0.5) & (tok <= qpos), live, -MASK_BIAS))
        m_old = m_sc[...]
        m_new = jnp.maximum(m_old, jnp.max(sm, axis=-1, keepdims=True))
        pe = jnp.exp(sm - jnp.concatenate([m_new] * rep, axis=1))
        acc_sc[...] = jnp.exp(m_old - m_new) * acc_sc[...] + _dot(pe.astype(BF), kv1)
        m_sc[...] = m_new

    scores(0, sa_sc)

    def chunk_pair(j, carry):
        scores(2 * j + 1, sb_sc)
        update(2 * j, sa_sc)
        scores(2 * j + 2, sa_sc)
        update(2 * j + 1, sb_sc)
        return carry

    lax.fori_loop(0, last // 2 + 1, chunk_pair, 0)

    n_win = WINDOW + NSA_Q
    w0 = pl.multiple_of(jnp.maximum(s0 - WINDOW, 0), NSA_Q)
    kvw = kwin_ref[pl.ds(w0, n_win), :]
    dist = qpos - (w0 + lax.broadcasted_iota(jnp.int32, (1, n_win), 1))
    ew = _softmax_bias(_dot_nt(qs, kvw), heads(jnp.where((dist >= 0) & (dist <= WINDOW), 0.0, -MASK_BIAS)))
    o_win = _dot(ew.astype(BF), jnp.where(key_lane, jnp.ones_like(kvw), kvw))

    g = g_ref[...]
    outs = []
    for h in range(N_HEADS):
        rows = slice(h * NSA_Q, (h + 1) * NSA_Q)
        acc = acc_sc[rows, :]
        win = o_win[rows]
        o = (g[:, 3 * h:3 * h + 1] * o_cmp[rows]
             + g[:, 3 * h + 1:3 * h + 2] * (acc / jnp.maximum(acc[:, 0:1], 1e-30))
             + g[:, 3 * h + 2:3 * h + 3] * (win / jnp.maximum(win[:, 0:1], 1e-30)))
        outs.append(o[:, HEAD_DIM:])
    o_ref[...] = jnp.concatenate(outs, axis=1).astype(BF)


def nsa_prompt(qa, gate, kcmp, ksel_b, kwin_b, mt, e_mat):
    bsz, t_len, _ = qa.shape
    n_cmp = kcmp.shape[1]
    n_sel = mt.shape[0]
    return pl.pallas_call(
        _nsa_prompt_kernel,
        grid=(bsz, t_len // NSA_Q),
        in_specs=[_row_spec(NSA_Q, 2 * GROUP_W), _row_spec(NSA_Q, LANES),
                  pl.BlockSpec((None, n_cmp, LANES), lambda b, t: (b, 0, 0)),
                  pl.BlockSpec((None, t_len, LANES), lambda b, t: (b, 0, 0)),
                  pl.BlockSpec((None, t_len, LANES), lambda b, t: (b, 0, 0)),
                  _const_spec((n_sel, n_cmp)), _const_spec((n_sel, t_len))],
        out_specs=_row_spec(NSA_Q, GROUP_W),
        out_shape=jax.ShapeDtypeStruct((bsz, t_len, GROUP_W), BF),
        scratch_shapes=[pltpu.VMEM((N_HEADS * NSA_Q, LANES), F32), pltpu.VMEM((N_HEADS * NSA_Q, LANES), F32),
                        pltpu.VMEM((N_HEADS * NSA_Q, NSA_KC), F32), pltpu.VMEM((N_HEADS * NSA_Q, NSA_KC), F32)],
        compiler_params=_params(("parallel", "arbitrary")),
        name="nsa_prompt",
    )(qa, gate, kcmp, ksel_b, kwin_b, mt, e_mat)


SB_Q = 256
SB_K = 256


SB_STOP = -110.0


def _softplus(z):
    return jnp.maximum(z, 0.0) + jnp.log(1.0 + jnp.exp(-jnp.abs(z)))


def _tri_neg(n):
    r = lax.broadcasted_iota(jnp.int32, (2 * n, n + LANES), 0)
    c = lax.broadcasted_iota(jnp.int32, (2 * n, n + LANES), 1)
    r = jnp.where(r >= n, r - n, r)
    return jnp.where((r > c) | (c >= n), -1.0, 0.0).astype(BF)


def _sb_weights(z, tri, carry, mask):
    n_k = z.shape[1]
    sp = _softplus(z)
    spm = sp if mask is None else jnp.where(mask, sp, 0.0)
    hi = spm.astype(BF)
    lo = (spm - hi.astype(F32)).astype(BF)
    cs = _dot(jnp.concatenate([hi, lo], axis=1), tri)
    after = cs[:, :n_k] + jnp.concatenate([carry] * (n_k // LANES), axis=1)
    a = jnp.exp(z - sp + after)
    if mask is not None:
        a = jnp.where(mask, a, 0.0)
    return a.astype(BF), carry + cs[:, n_k:]


def _sb_prompt_kernel(qb_ref, kvb_ref, o_ref, acc_sc, car_sc):
    i = pl.program_id(1)
    q = qb_ref[...]
    lane = lax.broadcasted_iota(jnp.int32, (1, GROUP_W), 1)
    qh = [jnp.where(lane // HEAD_DIM == h, q, jnp.zeros_like(q)) for h in range(N_HEADS)]
    tri = _tri_neg(SB_K)
    r = lax.broadcasted_iota(jnp.int32, (SB_Q, SB_K), 0)
    c = lax.broadcasted_iota(jnp.int32, (SB_Q, SB_K), 1)
    diag_mask = c < r

    def chunk(k0, mask, first):
        k = kvb_ref[pl.ds(k0, SB_K), 0:GROUP_W]
        v = kvb_ref[pl.ds(k0, SB_K), GROUP_W:2 * GROUP_W]
        car_max = None
        for h in range(N_HEADS):
            car = jnp.zeros((SB_Q, LANES), F32) if first else car_sc[h]
            a, car = _sb_weights(_dot_nt(qh[h], k), tri, car, mask)
            pv = _dot(a, v)
            acc_sc[h] = pv if first else acc_sc[h] + pv
            car_sc[h] = car
            car_max = car if car_max is None else jnp.maximum(car_max, car)
        return jnp.max(car_max)

    car_max = chunk(pl.multiple_of(i * SB_Q, SB_Q), diag_mask, True)

    def more(st):
        j, car_max = st
        return (j < i) & (car_max > SB_STOP)

    def body(st):
        j, _ = st
        return j + 1, chunk(pl.multiple_of((i - 1 - j) * SB_K, SB_K), None, False)

    lax.while_loop(more, body, (0, car_max))
    o = jnp.zeros((SB_Q, GROUP_W), F32)
    for h in range(N_HEADS):
        o = jnp.where(lane // HEAD_DIM == h, acc_sc[h], o)
    o_ref[...] = o.astype(BF)


def sb_prompt(qb, kvb_b):
    bsz, t_len, _ = qb.shape
    return pl.pallas_call(
        _sb_prompt_kernel,
        grid=(bsz, t_len // SB_Q),
        in_specs=[_row_spec(SB_Q, GROUP_W),
                  pl.BlockSpec((None, t_len, 2 * GROUP_W), lambda b, t: (b, 0, 0))],
        out_specs=_row_spec(SB_Q, GROUP_W),
        out_shape=jax.ShapeDtypeStruct((bsz, t_len, GROUP_W), BF),
        scratch_shapes=[pltpu.VMEM((N_HEADS, SB_Q, GROUP_W), F32),
                        pltpu.VMEM((N_HEADS, SB_Q, LANES), F32)],
        compiler_params=_params(("parallel", "arbitrary")),
        name="sb_prompt",
    )(qb, kvb_b)


def _pool_mix(ext_ref, tm, tpos, pool_w_ref, pool_s_ref):
    def shifted(ref, k):
        return ref[pl.ds(POOL_MAX - k, tm), :]
    x = shifted(ext_ref, 0)
    lane = lax.broadcasted_iota(jnp.int32, (1, GROUP_W), 1)
    grp = lane // HEAD_DIM
    s2 = x + shifted(ext_ref, 1)
    s4 = s2 + shifted(ext_ref, 2) + shifted(ext_ref, 3)
    s8 = s4 + sum(shifted(ext_ref, k) for k in range(4, 8))
    s16 = s8 + sum(shifted(ext_ref, k) for k in range(8, 16))
    tot = jnp.where(grp == 0, s2, jnp.where(grp == 1, s4, jnp.where(grp == 2, s8, s16)))
    wlen = jnp.where(grp == 0, 2, jnp.where(grp == 1, 4, jnp.where(grp == 2, 8, 16)))
    cnt = jnp.minimum(wlen, tpos + 1).astype(F32)
    d = tot / cnt - x
    return _dot(d.astype(BF), pool_w_ref[...]) * pool_s_ref[...]


def _gmlp_pool_kernel(u_ref, v_ref, pin_ref, halo_ref, sgw_ref, sgb_ref, pw_ref, ps_ref,
                      oc_ref, od_ref, ext_sc):
    t = pl.program_id(1)
    tm = u_ref.shape[0]
    lane = lax.broadcasted_iota(jnp.int32, (1, GROUP_W), 1)
    for c in range(tm // CHUNK):
        rows = slice(c * CHUNK, (c + 1) * CHUNK)
        v = v_ref[rows, :]
        vz = jnp.zeros_like(v)
        vst = jnp.concatenate([jnp.where(lane // HEAD_DIM == h, v, vz) for h in range(N_HEADS)], axis=0)
        s = _dot(sgw_ref[...], vst.astype(BF)) + sgb_ref[...]
        oc_ref[rows, :] = (u_ref[rows, :] * s).astype(BF)
    halo = halo_ref[...]
    ext_sc[0:POOL_MAX, :] = jnp.where(t > 0, halo, jnp.zeros_like(halo))
    ext_sc[POOL_MAX:POOL_MAX + tm, :] = pin_ref[...]
    tpos = t * tm + lax.broadcasted_iota(jnp.int32, (tm, 1), 0)
    od_ref[...] = _pool_mix(ext_sc, tm, tpos, pw_ref, ps_ref).astype(BF)


def gmlp_pool_prompt(u, v, pin, sgw_cat, sgb_full, pool_w_bd, pool_scale, tm):
    bsz, t_len, _ = u.shape
    per = tm // POOL_MAX
    halo_spec = pl.BlockSpec((None, POOL_MAX, GROUP_W), lambda b, t: (b, jnp.maximum(t * per - 1, 0), 0))
    return pl.pallas_call(
        _gmlp_pool_kernel,
        grid=(bsz, t_len // tm),
        in_specs=[_row_spec(tm, GROUP_W), _row_spec(tm, GROUP_W), _row_spec(tm, GROUP_W), halo_spec,
                  _const_spec((CHUNK, N_HEADS * CHUNK)), _const_spec((CHUNK, GROUP_W)),
                  _const_spec((GROUP_W, GROUP_W)), _const_spec((1, GROUP_W))],
        out_specs=[_row_spec(tm, GROUP_W), _row_spec(tm, GROUP_W)],
        out_shape=[jax.ShapeDtypeStruct((bsz, t_len, GROUP_W), BF)] * 2,
        scratch_shapes=[pltpu.VMEM((POOL_MAX + tm, GROUP_W), F32)],
        compiler_params=_params(("parallel", "parallel")),
        name="gmlp_pool",
    )(u, v, pin, pin, sgw_cat, sgb_full, pool_w_bd, pool_scale)


def _swap_neg(w):
    half = HEAD_DIM // 2
    w = w.reshape(w.shape[0], -1, 2, half)
    return jnp.stack([-w[:, :, 1], w[:, :, 0]], axis=2).reshape(w.shape[0], -1)


def _pad_heads(w):
    w = w.reshape(w.shape[0], -1, HEAD_DIM)
    return jnp.concatenate([w, jnp.zeros_like(w)], axis=2).reshape(w.shape[0], -1)


def _proj_weights(w_in):
    a_q, a_kv, a_g, b_qkv, c_uv, d_in = jnp.split(
        w_in, np.cumsum([GROUP_W, 6 * HEAD_DIM, 3 * N_HEADS, 3 * GROUP_W, 2 * GROUP_W]).tolist(), axis=1)
    scale = HEAD_DIM ** -0.5
    a_q = a_q * scale
    kv = a_kv.reshape(-1, 3, 2, HEAD_DIM)
    kv_sw = jnp.concatenate([_swap_neg(kv[:, :, 0].reshape(-1, 3 * HEAD_DIM)).reshape(-1, 3, 1, HEAD_DIM),
                             jnp.zeros_like(kv[:, :, 1:2])], axis=2).reshape(-1, 6 * HEAD_DIM)
    g_pad = jnp.pad(a_g, ((0, 0), (0, LANES - 3 * N_HEADS)))
    b_q = b_qkv[:, :GROUP_W] * scale
    w_all = jnp.concatenate([_pad_heads(a_q), _pad_heads(_swap_neg(a_q)), a_kv, kv_sw, g_pad,
                             b_q, b_qkv[:, GROUP_W:], c_uv, d_in], axis=1)
    w_t = jnp.concatenate([a_kv, kv_sw, b_qkv[:, GROUP_W:]], axis=1).T
    return w_all.astype(BF), w_t.astype(BF)


def _rope_tables(pos):
    half = HEAD_DIM // 2
    inv = ROPE_THETA ** (-jnp.arange(half, dtype=F32) / half)
    ang = pos.astype(F32)[:, None] * inv[None, :]
    cos, sin = jnp.cos(ang), jnp.sin(ang)
    cos_t = jnp.concatenate([cos.T, cos.T, jnp.ones((HEAD_DIM, pos.shape[0]), F32)], axis=0)
    sin_t = jnp.concatenate([sin.T, sin.T, jnp.zeros((HEAD_DIM, pos.shape[0]), F32)], axis=0)
    return jnp.tile(cos, (1, 4)), jnp.tile(sin, (1, 4)), cos_t, sin_t


def _cmp_weights(cmp_w):
    w = jnp.zeros((CMP_LEN, 2, HEAD_DIM, 2, HEAD_DIM), F32)
    w = w.at[:, 0, :, 0, :].set(cmp_w[0]).at[:, 1, :, 1, :].set(cmp_w[1])
    w = w.reshape(2, CMP_STRIDE * LANES, LANES)
    return jnp.concatenate([w[0], w[1]], axis=1).astype(BF)


def _cmp_pe_rows(cmp_pe):
    return jnp.pad(cmp_pe.reshape(2, CMP_STRIDE * LANES), ((0, 6), (0, 0))).astype(BF)


def _sel_constants(n_cmp, n_sel, t_len):
    n = np.arange(n_cmp)[None, :]
    j = np.arange(n_sel)[:, None]
    mt = ((n >= 4 * j - 1) & (n <= 4 * j + 3)).astype(np.float32) + ((n >= 4 * j) & (n <= 4 * j + 2))
    e = (np.arange(t_len)[None, :] // SEL_BLOCK == j).astype(np.float32)
    return jnp.asarray(mt, BF), jnp.asarray(e, BF)


def _gmlp_weights(sg_w, sg_b):
    wm = sg_w * jnp.tril(jnp.ones((CHUNK, CHUNK), sg_w.dtype))
    w_cat = jnp.transpose(wm, (1, 0, 2)).reshape(CHUNK, N_HEADS * CHUNK).astype(BF)
    b_full = jnp.repeat(sg_b.T, HEAD_DIM, axis=1)
    return w_cat, b_full


def _pool_weights(pool_w):
    w = jnp.zeros((N_HEADS, HEAD_DIM, N_HEADS, HEAD_DIM), F32)
    for g in range(N_HEADS):
        w = w.at[g, :, g, :].set(pool_w[g])
    return w.reshape(GROUP_W, GROUP_W).astype(BF)


def _sample_sel_weights(n_cmp_pad, n_sel_pad, n_cmp, n_sel):
    n = np.arange(n_cmp_pad)[:, None]
    j = np.arange(n_sel_pad)[None, :]
    ms = ((n >= 4 * j - 1) & (n <= 4 * j + 3)).astype(np.float32) + ((n >= 4 * j) & (n <= 4 * j + 2))
    ms = ms * ((n < n_cmp) & (j < n_sel))
    return jnp.asarray(ms, BF)


def _layer_weights(l, w_ada, b_ada, norm_g, w_ffn_up, w_ffn_down, w_in, w_out,
                   cmp_pe, cmp_w, sg_ln_g, sg_ln_b, sg_w, sg_b, pool_w, pool_scale):
    sgw_cat, sgb_full = _gmlp_weights(sg_w[l], sg_b[l])
    w_proj, w_proj_t = _proj_weights(w_in[l])
    return dict(
        layer=l, w_ada=w_ada, b_ada=b_ada[l][None, :], norm_g=norm_g[l][:, None, :],
        w_up=w_ffn_up, w_down=w_ffn_down, w_in=w_proj, w_in_t=w_proj_t, w_out=w_out,
        cmp_pe=_cmp_pe_rows(cmp_pe[l]), cmp_w=_cmp_weights(cmp_w[l]),
        ln_g=sg_ln_g[l][None, :], ln_b=sg_ln_b[l][None, :], sgw_cat=sgw_cat, sgb_full=sgb_full,
        sg_w00=jnp.repeat(sg_w[l][:, 0, 0], HEAD_DIM)[None, :], sg_b0=jnp.repeat(sg_b[l][:, 0], HEAD_DIM)[None, :],
        pool_w=_pool_weights(pool_w[l]), pool_scale=pool_scale[l][None, :])


def _mods(mod, shape):
    m = mod.reshape(mod.shape[0], N_MOD, D_MODEL)
    return [m[:, k].reshape(shape) for k in range(N_MOD)]


def prompt_layer(x, mod, lw, tables, consts, tm):
    bsz, t_len, _ = x.shape
    m = _mods(mod, (bsz, 1, D_MODEL))
    g = lw["norm_g"]
    cos, sin, cos_t, sin_t = tables
    l = lw["layer"]
    x = ffn_half(x, m[0], m[1], m[2], g[0], g[1], lw["w_up"], lw["w_down"], (l, 0), tm)
    pr = in_proj(x, m[3], m[4], g[2], cos, sin, lw["w_in"], lw["ln_g"], lw["ln_b"], tm,
                 t_side=(lw["w_in_t"], cos_t, sin_t))
    kcmp = compress_prompt(pr["kv_cmp"], lw["cmp_pe"], lw["cmp_w"])
    o_a = nsa_prompt(pr["qa"], pr["gate"], kcmp, pr["kv_sel_b"], pr["kv_win_b"], *consts)
    o_b = sb_prompt(pr["qb"], pr["kvb_b"])
    o_c, o_d = gmlp_pool_prompt(pr["u"], pr["v"], pr["pin"], lw["sgw_cat"], lw["sgb_full"],
                                lw["pool_w"], lw["pool_scale"], tm)
    x = out_proj_ffn(x, (o_a, o_b, o_c, o_d), m[5], g[3], lw["w_out"], l,
                     m[6], m[7], m[8], g[4], g[5], lw["w_up"], lw["w_down"], (l, 1), tm)
    n_win = min(WINDOW, t_len)
    state = (pr["kvt_cmp"], pr["kvt_sel"], pr["kvt_win"][:, :, t_len - n_win:], pr["kvbt"],
             pr["pin"][:, t_len - POOL_BUF:])
    return x, state


def _page_copies(cache_ref, layer, pt_ref, b, first_page, n_pages, buf_ref, slot, sem_ref):
    return [pltpu.make_async_copy(cache_ref.at[layer, pt_ref[b, first_page + p]], buf_ref.at[slot, p],
                                  sem_ref.at[slot]) for p in range(n_pages)]


SB_GROUP = 4


def _sb_sample_kernel(pt_ref, q_ref, cache_ref, o_ref, buf0, buf, sem0, sem, *, layer, n_pages):
    b = pl.program_id(0)
    n_grp = n_pages // SB_GROUP

    def copies(seq, grp, dst, slot, sm):
        return _page_copies(cache_ref, layer, pt_ref, seq, (n_grp - 1 - grp) * SB_GROUP, SB_GROUP, dst, slot, sm)

    @pl.when(b == 0)
    def _():
        for c in copies(0, 0, buf0, 0, sem0):
            c.start()

    @pl.when(b + 1 < pl.num_programs(0))
    def _():
        for c in copies(b + 1, 0, buf0, (b + 1) % 2, sem0):
            c.start()

    if n_grp > 1:
        for c in copies(b, 1, buf, 1, sem):
            c.start()

    row = lax.broadcasted_iota(jnp.int32, (8, GROUP_W), 0)
    lane = lax.broadcasted_iota(jnp.int32, (8, GROUP_W), 1)
    head_lanes = lane // HEAD_DIM == row
    q = jnp.broadcast_to(q_ref[...].astype(F32), (8, GROUP_W))
    qm = jnp.where(head_lanes, q, 0.0).astype(BF)
    tri = _tri_neg(PAGE)

    def sweep(src, slot, acc, car):
        for p in reversed(range(SB_GROUP)):
            kt = src[slot, p, 0:GROUP_W, :].astype(BF)
            vt = src[slot, p, GROUP_W:2 * GROUP_W, :].astype(BF)
            a, car = _sb_weights(_dot(qm, kt), tri, car, None)
            acc = acc + _dot_nt(a, vt)
        return acc, car

    for c in copies(b, 0, buf0, b % 2, sem0):
        c.wait()
    acc, car = sweep(buf0, b % 2, jnp.zeros((8, GROUP_W), F32), jnp.zeros((8, LANES), F32))

    def more(st):
        g, car_max, _, _ = st
        return (g < n_grp) & (car_max > SB_STOP)

    def body(st):
        g, _, acc, car = st
        slot = g % 2

        @pl.when(g + 1 < n_grp)
        def _():
            for c in copies(b, g + 1, buf, 1 - slot, sem):
                c.start()

        for c in copies(b, g, buf, slot, sem):
            c.wait()
        acc, car = sweep(buf, slot, acc, car)
        return g + 1, jnp.max(car), acc, car

    g, _, acc, _ = lax.while_loop(more, body, (1, jnp.max(car), acc, car))

    @pl.when(g < n_grp)
    def _():
        for c in copies(b, g, buf, g % 2, sem):
            c.wait()

    o_ref[...] = jnp.sum(jnp.where(head_lanes, acc, 0.0), axis=0, keepdims=True)


def sb_sample(page_table, qb, cache_t, layer):
    n_seq = qb.shape[0]
    n_pages = page_table.shape[1]
    grp_buf = pltpu.VMEM((2, SB_GROUP, 2 * GROUP_W, PAGE), F32)
    return pl.pallas_call(
        functools.partial(_sb_sample_kernel, layer=layer, n_pages=n_pages),
        grid_spec=pltpu.PrefetchScalarGridSpec(
            num_scalar_prefetch=1, grid=(n_seq,),
            in_specs=[pl.BlockSpec((None, 1, GROUP_W), lambda b, pt: (b, 0, 0)),
                      pl.BlockSpec(memory_space=pl.ANY)],
            out_specs=pl.BlockSpec((None, 1, GROUP_W), lambda b, pt: (b, 0, 0)),
            scratch_shapes=[grp_buf, grp_buf, pltpu.SemaphoreType.DMA((2,)), pltpu.SemaphoreType.DMA((2,))]),
        out_shape=jax.ShapeDtypeStruct((n_seq, 1, GROUP_W), F32),
        compiler_params=_params(("arbitrary",)),
        name="sb_sample",
    )(page_table, qb, cache_t)


def _slab_copies(cache_ref, layer, pt_ref, b, n_pages, buf_ref, slot, sem_ref):
    return [pltpu.make_async_copy(cache_ref.at[layer, pt_ref[b, p]],
                                  buf_ref.at[slot, :, pl.ds(p * PAGE, PAGE)], sem_ref.at[slot])
            for p in range(n_pages)]


CMP_CHUNK = 256
TAIL_ROWS = 128
XPOSE_TOKENS = 1024
SLAB_CAST = 2048


def _nsa_cmp_sample_kernel(pt_ref, q_ref, new_ref, pe_ref, w_ref, ms_ref, cache_ref, ocmp_ref, sel_ref,
                           buf, sem, rows_sc, lo_sc, hi_sc, *, layer, n_pages):
    b = pl.program_id(0)
    slot = b % 2

    def copies(seq, sl):
        return _slab_copies(cache_ref, layer, pt_ref, seq, n_pages, buf, sl, sem)

    @pl.when(b == 0)
    def _():
        for c in copies(0, 0):
            c.start()

    @pl.when(b + 1 < pl.num_programs(0))
    def _():
        for c in copies(b + 1, 1 - slot):
            c.start()

    for c in copies(b, slot):
        c.wait()

    past = n_pages * PAGE
    n_pad = lo_sc.shape[0]
    for c in range(past // XPOSE_TOKENS):
        span = slice(c * XPOSE_TOKENS, (c + 1) * XPOSE_TOKENS)
        rows_sc[span, :] = buf[slot, :, span].T
    r_io = lax.broadcasted_iota(jnp.int32, (TAIL_ROWS, LANES), 0)
    rows_sc[past:past + TAIL_ROWS, :] = jnp.where(r_io == 0, jnp.broadcast_to(new_ref[...], (TAIL_ROWS, LANES)), 0.0)

    chunk = math.gcd(CMP_CHUNK, past // CMP_STRIDE)

    def cmp_chunk(c, carry):
        g0 = pl.multiple_of(c * chunk, chunk)
        lo, hi = _compress_rows(rows_sc, chunk, pe_ref, w_ref, base=g0 * CMP_STRIDE)
        lo_sc[pl.ds(g0, chunk), :] = lo
        hi_sc[pl.ds(g0, chunk), :] = hi
        return carry

    lax.fori_loop(0, past // (CMP_STRIDE * chunk), cmp_chunk, 0)
    n_tail = TAIL_ROWS // CMP_STRIDE
    g_tail = past // CMP_STRIDE
    lo, hi = _compress_rows(rows_sc, n_tail, pe_ref, w_ref, base=past)
    lo_sc[g_tail:g_tail + n_tail, :] = lo
    hi_sc[g_tail:g_tail + n_tail, :] = hi
    lo_sc[g_tail + n_tail:n_pad, :] = jnp.zeros((n_pad - g_tail - n_tail, LANES), F32)
    hi_sc[g_tail + n_tail:n_pad + 8, :] = jnp.zeros((n_pad + 8 - g_tail - n_tail, LANES), F32)
    kc = (lo_sc[...] + hi_sc[pl.ds(1, n_pad), :]).astype(BF)

    q8 = q_ref[...]
    qpos = past
    s = _dot_nt(q8, kc)
    blk_end = lax.broadcasted_iota(jnp.int32, (1, n_pad), 1) * CMP_STRIDE + (CMP_LEN - 1)
    p = _softmax_rows(s, blk_end <= qpos)
    ocmp_ref[...] = _dot(p.astype(BF), kc)
    ps = jnp.broadcast_to(jnp.sum(p[0:N_HEADS], axis=0, keepdims=True), (8, n_pad))
    ms = ms_ref[...]
    p_slc = sum(_dot(part, ms) for part in _split3(ps))
    j_io = lax.broadcasted_iota(jnp.int32, p_slc.shape, 1)
    cur = qpos // SEL_BLOCK
    forced = (j_io == 0) | (j_io == cur) | (j_io == cur - 1)
    valid = j_io <= cur
    score = jnp.where(valid, jnp.where(forced, FORCE_SCORE, p_slc), NEG)
    sel_ref[...] = _topk_select(score, valid, N_TOPK, axis=1)


def nsa_cmp_sample(page_table, q8, new_cmp, pe, w_cmp, ms, cache_t, layer):
    n_seq = q8.shape[0]
    n_pages = page_table.shape[1]
    n_pad, n_sel_pad = ms.shape
    seq_spec = lambda w: pl.BlockSpec((None, 8, w), lambda b, pt: (b, 0, 0))
    const2 = lambda shape: pl.BlockSpec(shape, lambda b, pt: (0,) * len(shape))
    return pl.pallas_call(
        functools.partial(_nsa_cmp_sample_kernel, layer=layer, n_pages=n_pages),
        grid_spec=pltpu.PrefetchScalarGridSpec(
            num_scalar_prefetch=1, grid=(n_seq,),
            in_specs=[seq_spec(LANES), pl.BlockSpec((None, 1, LANES), lambda b, pt: (b, 0, 0)),
                      const2((8, CMP_STRIDE * LANES)), const2((CMP_STRIDE * LANES, 2 * LANES)),
                      const2((n_pad, n_sel_pad)),
                      pl.BlockSpec(memory_space=pl.ANY)],
            out_specs=[seq_spec(LANES), seq_spec(n_sel_pad)],
            scratch_shapes=[pltpu.VMEM((2, LANES, n_pages * PAGE), F32), pltpu.SemaphoreType.DMA((2,)),
                            pltpu.VMEM((n_pages * PAGE + TAIL_ROWS, LANES), F32),
                            pltpu.VMEM((n_pad, LANES), F32), pltpu.VMEM((n_pad + 8, LANES), F32)]),
        out_shape=[jax.ShapeDtypeStruct((n_seq, 8, LANES), F32), jax.ShapeDtypeStruct((n_seq, 8, n_sel_pad), F32)],
        compiler_params=_params(("arbitrary",)),
        name="nsa_cmp_sample",
    )(page_table, q8, new_cmp, pe, w_cmp, ms, cache_t)


def _nsa_sel_sample_kernel(pt_ref, q_ref, sel_ref, ocmp_ref, gate_ref, news_ref, neww_ref, e_ref, win_ref,
                           cache_ref, o_ref, buf, sem, kv_sc, *, layer, n_pages):
    b = pl.program_id(0)
    slot = b % 2

    def copies(seq, sl):
        return _slab_copies(cache_ref, layer, pt_ref, seq, n_pages, buf, sl, sem)

    @pl.when(b == 0)
    def _():
        for c in copies(0, 0):
            c.start()

    @pl.when(b + 1 < pl.num_programs(0))
    def _():
        for c in copies(b + 1, 1 - slot):
            c.start()

    for c in copies(b, slot):
        c.wait()

    past = n_pages * PAGE
    qpos = past
    q8 = q_ref[...]
    q8f = q8.astype(F32)
    sel = sel_ref[...]
    cast = math.gcd(SLAB_CAST, past)
    for c in range(past // cast):
        span = slice(c * cast, (c + 1) * cast)
        kv_sc[:, span] = buf[slot, :, span].astype(BF)
    kvt = kv_sc[...]

    def new_token(new_ref, allowed):
        kn = new_ref[...].astype(BF).astype(F32)
        s_new = jnp.sum(q8f * kn, axis=-1, keepdims=True)
        return kn, jnp.where(allowed, s_new, NEG)

    blk = e_ref.shape[0]
    sel_b = sel.astype(BF)
    selx = jnp.concatenate([_dot(sel_b[:, c * blk:(c + 1) * blk], e_ref[...])
                            for c in range(past // (blk * SEL_BLOCK))], axis=1)
    allow = (selx > 0.5) & (lax.broadcasted_iota(jnp.int32, (1, past), 1) <= qpos)
    sm = jnp.where(allow, _dot(q8, kvt), NEG)
    kn_s, s_new = new_token(news_ref, (sel[:, qpos // SEL_BLOCK:qpos // SEL_BLOCK + 1] > 0.5))
    m = jnp.maximum(jnp.max(sm, axis=-1, keepdims=True), s_new)
    e = jnp.where(allow, jnp.exp(sm - m), 0.0)
    e_new = jnp.where(s_new > 0.5 * NEG, jnp.exp(s_new - m), 0.0)
    l_sel = jnp.sum(e, axis=-1, keepdims=True) + e_new
    o_sel = (_dot_nt(e.astype(BF), kvt) + e_new * kn_s) / jnp.maximum(l_sel, 1e-30)

    n_buf = win_ref.shape[1]
    wb = win_ref[...].astype(BF)
    kwpos = past - n_buf + lax.broadcasted_iota(jnp.int32, (1, n_buf), 1)
    dist = qpos - kwpos
    allow_w = (dist >= 0) & (dist <= WINDOW) & (kwpos >= 0)
    sw = jnp.where(allow_w, _dot(q8, wb), NEG)
    kn_w, sw_new = new_token(neww_ref, True)
    mw = jnp.maximum(jnp.max(sw, axis=-1, keepdims=True), sw_new)
    ew = jnp.where(allow_w, jnp.exp(sw - mw), 0.0)
    ew_new = jnp.exp(sw_new - mw)
    l_w = jnp.sum(ew, axis=-1, keepdims=True) + ew_new
    o_win = (_dot_nt(ew.astype(BF), wb) + ew_new * kn_w) / jnp.maximum(l_w, 1e-30)

    g = jnp.broadcast_to(gate_ref[...], (8, LANES))
    g_row = lax.broadcasted_iota(jnp.int32, (8, LANES), 0)
    g_lane = lax.broadcasted_iota(jnp.int32, (8, LANES), 1)
    gk = [jnp.sum(jnp.where(g_lane == 3 * g_row + k, g, 0.0), axis=-1, keepdims=True) for k in range(3)]
    o_ref[...] = gk[0] * ocmp_ref[...] + gk[1] * o_sel + gk[2] * o_win


def nsa_sel_sample(page_table, q8, sel, o_cmp, gate, new_sel, new_win, e_mat, win_t, cache_t, layer):
    n_seq = q8.shape[0]
    n_pages = page_table.shape[1]
    n_buf = win_t.shape[3]
    seq_spec = lambda r, w: pl.BlockSpec((None, r, w), lambda b, pt: (b, 0, 0))
    return pl.pallas_call(
        functools.partial(_nsa_sel_sample_kernel, layer=layer, n_pages=n_pages),
        grid_spec=pltpu.PrefetchScalarGridSpec(
            num_scalar_prefetch=1, grid=(n_seq,),
            in_specs=[seq_spec(8, LANES), seq_spec(8, sel.shape[2]), seq_spec(8, LANES), seq_spec(1, LANES),
                      seq_spec(1, LANES), seq_spec(1, LANES),
                      pl.BlockSpec(e_mat.shape, lambda b, pt: (0, 0)),
                      pl.BlockSpec((None, None, LANES, n_buf), lambda b, pt: (layer, b, 0, 0)),
                      pl.BlockSpec(memory_space=pl.ANY)],
            out_specs=seq_spec(8, LANES),
            scratch_shapes=[pltpu.VMEM((2, LANES, n_pages * PAGE), F32), pltpu.SemaphoreType.DMA((2,)),
                            pltpu.VMEM((LANES, n_pages * PAGE), BF)]),
        out_shape=jax.ShapeDtypeStruct((n_seq, 8, LANES), F32),
        compiler_params=_params(("arbitrary",)),
        name="nsa_sel_sample",
    )(page_table, q8, sel, o_cmp, gate, new_sel, new_win, e_mat, win_t, cache_t)


def _gmlp_pool_sample_kernel(u_ref, v_ref, pin_ref, hist_ref, w00_ref, b0_ref, pw_ref, ps_ref, oc_ref, od_ref,
                             *, past_len):
    oc_ref[...] = u_ref[...] * (w00_ref[...] * v_ref[...] + b0_ref[...])
    x = pin_ref[...]
    lane = lax.broadcasted_iota(jnp.int32, (1, GROUP_W), 1)
    grp = lane // HEAD_DIM
    sums = []
    tot = x
    k = 1
    for wlen in POOL_WINDOWS:
        while k < wlen:
            tot = tot + hist_ref[POOL_BUF - k]
            k += 1
        sums.append(tot)
    tot = jnp.where(grp == 0, sums[0], jnp.where(grp == 1, sums[1], jnp.where(grp == 2, sums[2], sums[3])))
    wlen = jnp.where(grp == 0, POOL_WINDOWS[0], jnp.where(grp == 1, POOL_WINDOWS[1],
                                                          jnp.where(grp == 2, POOL_WINDOWS[2], POOL_WINDOWS[3])))
    cnt = jnp.minimum(wlen, past_len + 1).astype(F32)
    d = tot / cnt - x
    od_ref[...] = _dot(d.astype(BF), pw_ref[...]) * ps_ref[...]


def gmlp_pool_sample(u, v, pin, hist_t, layer, w00, b0, pool_w_bd, pool_scale, past_len):
    n_seq = u.shape[0]
    full = lambda shape: pl.BlockSpec(shape, lambda i: (0,) * len(shape))
    return pl.pallas_call(
        functools.partial(_gmlp_pool_sample_kernel, past_len=past_len),
        grid=(1,),
        in_specs=[full((n_seq, GROUP_W))] * 3 +
                 [pl.BlockSpec((None, POOL_BUF, n_seq, GROUP_W), lambda i: (layer, 0, 0, 0)),
                  full((1, GROUP_W)), full((1, GROUP_W)), full((GROUP_W, GROUP_W)), full((1, GROUP_W))],
        out_specs=[full((n_seq, GROUP_W))] * 2,
        out_shape=[jax.ShapeDtypeStruct((n_seq, GROUP_W), F32)] * 2,
        compiler_params=_params(("arbitrary",)),
        name="gmlp_pool_sample",
    )(u, v, pin, hist_t, w00, b0, pool_w_bd, pool_scale)


def sample_layer(x, mod, lw, tables, sel_consts, caches_t, layer, page_table):
    n_seq = x.shape[1]
    past_len = page_table.shape[1] * PAGE
    cmp_t, sel_t, win_t, sb_t, pool_t = caches_t
    ms, e_mat = sel_consts
    m = _mods(mod, (1, n_seq, D_MODEL))
    g = lw["norm_g"]
    x = ffn_half(x, m[0], m[1], m[2], g[0], g[1], lw["w_up"], lw["w_down"], (layer, 0), n_seq)
    pr = in_proj(x, m[3], m[4], g[2], tables[0], tables[1], lw["w_in"], lw["ln_g"], lw["ln_b"], n_seq)
    rows = {k: v[0] for k, v in pr.items()}
    q8 = jnp.pad(rows["qa"].reshape(n_seq, N_HEADS, LANES), ((0, 0), (0, 8 - N_HEADS), (0, 0)))
    o_cmp, sel = nsa_cmp_sample(page_table, q8, rows["kv_cmp"][:, None, :], lw["cmp_pe"], lw["cmp_w"], ms,
                                cmp_t, layer)
    o8 = nsa_sel_sample(page_table, q8, sel, o_cmp, rows["gate"][:, None, :], rows["kv_sel"][:, None, :],
                        rows["kv_win"][:, None, :], e_mat, win_t, sel_t, layer)
    o_a = o8[:, :N_HEADS, HEAD_DIM:].reshape(1, n_seq, GROUP_W).astype(BF)
    o_b = sb_sample(page_table, rows["qb"][:, None, :], sb_t, layer).reshape(1, n_seq, GROUP_W).astype(BF)
    o_c, o_d = gmlp_pool_sample(rows["u"], rows["v"], rows["pin"], pool_t, layer, lw["sg_w00"], lw["sg_b0"],
                                lw["pool_w"], lw["pool_scale"], past_len)
    x = out_proj_ffn(x, (o_a, o_b, o_c[None].astype(BF), o_d[None].astype(BF)), m[5], g[3], lw["w_out"], layer,
                     m[6], m[7], m[8], g[4], g[5], lw["w_up"], lw["w_down"], (layer, 1), n_seq)
    state = (rows["kv_cmp"], rows["kv_sel"], rows["kv_win"], rows["kvb"], rows["pin"], rows["v"])
    return x, state


PROMPT_TM = 512


def kernel(x_prompt, x_sample, cache_nsa_cmp, cache_nsa_sel, cache_nsa_win, cache_sb, state_pool, page_table,
           c_prompt, c_sample, w_ada, b_ada, norm_g, w_ffn_up, w_ffn_down, w_in, w_out,
           cmp_pe, cmp_w, sg_ln_g, sg_ln_b, sg_w, sg_b, pool_w, pool_scale):
    n_p, t_len, _ = x_prompt.shape
    n_s, t_dec, _ = x_sample.shape
    assert t_dec == 1, "the sample step advances one token per sequence"
    depth = w_ada.shape[0]
    n_phys = cache_sb.shape[1]
    past_len = page_table.shape[1] * PAGE
    c_all = jnp.concatenate([c_prompt, c_sample], axis=0)
    c_all = jnp.pad(c_all, ((0, (-c_all.shape[0]) % 8), (0, 0)))
    tables_p = _rope_tables(jnp.arange(t_len))
    tables_s = _rope_tables(jnp.full((n_s,), past_len))
    consts = _sel_constants(t_len // CMP_STRIDE, t_len // SEL_BLOCK, t_len)
    n_rows = -(-(past_len + t_dec) // SEL_BLOCK) * SEL_BLOCK
    n_cmp = n_rows // CMP_STRIDE - 1
    n_sel = (n_cmp + 1) // (SEL_BLOCK // CMP_STRIDE)
    ms = _sample_sel_weights(-(-((past_len + TAIL_ROWS) // CMP_STRIDE) // LANES) * LANES,
                             -(-n_sel // LANES) * LANES, n_cmp, n_sel)
    blk_s = math.gcd(LANES, past_len // SEL_BLOCK)
    sel_consts_s = (ms, _sel_constants(1, blk_s, blk_s * SEL_BLOCK)[1])
    caches_t = (jnp.transpose(cache_nsa_cmp, (0, 1, 3, 4, 2)).reshape(depth, n_phys, LANES, PAGE),
                jnp.transpose(cache_nsa_sel, (0, 1, 3, 4, 2)).reshape(depth, n_phys, LANES, PAGE),
                jnp.transpose(cache_nsa_win, (0, 1, 3, 4, 2)).reshape(depth, n_s, LANES, -1),
                jnp.transpose(cache_sb, (0, 1, 3, 4, 5, 2)).reshape(depth, n_phys, 2 * GROUP_W, PAGE),
                jnp.transpose(state_pool, (0, 2, 1, 3)))
    y_p, y_s = x_prompt, x_sample.reshape(1, n_s, D_MODEL)
    st_p, st_s = [], []
    w_ada, w_ffn_up, w_ffn_down, w_out = (w.astype(BF) for w in (w_ada, w_ffn_up, w_ffn_down, w_out))
    for l in range(depth):
        lw = _layer_weights(l, w_ada, b_ada, norm_g, w_ffn_up, w_ffn_down, w_in, w_out,
                            cmp_pe, cmp_w, sg_ln_g, sg_ln_b, sg_w, sg_b, pool_w, pool_scale)
        mod = ada_mod(c_all, lw["w_ada"], lw["b_ada"], l)
        y_p, s = prompt_layer(y_p, mod[:n_p], lw, tables_p, consts, PROMPT_TM)
        st_p.append(s)
        y_s, s = sample_layer(y_s, mod[n_p:n_p + n_s], lw, tables_s, sel_consts_s, caches_t, l, page_table)
        st_s.append(s)

    def stack(states, i):
        return jnp.stack([s[i] for s in states], axis=0)

    def kv_rows(x_t):
        d, bsz, _, t = x_t.shape
        return jnp.transpose(x_t.reshape(d, bsz, 2, HEAD_DIM, t), (0, 1, 4, 2, 3))

    sb_p = stack(st_p, 3)
    sb_p = jnp.transpose(sb_p.reshape(depth, n_p, 2, N_HEADS, HEAD_DIM, t_len), (0, 1, 5, 2, 3, 4))
    win_new = stack(st_s, 2).reshape(depth, n_s, 1, 2, HEAD_DIM)
    pool_new = stack(st_s, 4)[:, :, None, :]
    return (y_p, y_s.reshape(n_s, 1, D_MODEL),
            kv_rows(stack(st_p, 0)), stack(st_s, 0).reshape(depth, n_s, 1, 2, HEAD_DIM),
            kv_rows(stack(st_p, 1)), stack(st_s, 1).reshape(depth, n_s, 1, 2, HEAD_DIM),
            kv_rows(stack(st_p, 2)), jnp.concatenate([cache_nsa_win[:, :, 1:], win_new], axis=2),
            sb_p, stack(st_s, 3).reshape(depth, n_s, 1, 2, N_HEADS, HEAD_DIM),
            stack(st_p, 4), jnp.concatenate([state_pool[:, :, 1:], pool_new], axis=2),
            stack(st_s, 5)[:, :, None, :])
```
